```python
import jax, jax.numpy as jnp
from jax import lax
import numpy as np

D_MODEL = 1024
BATCH = 2
SEQ = 8192
DEPTH = 4
DEC_BATCH = 32
DEC_SEQ = 32
PAST_LEN = 1024

CHUNK = 64
EPS = 1e-6
N_HEADS = 8
QK_NOPE = 64
QK_ROPE = 32
V_HEAD = 64
Q_LORA = 512
KV_LORA = 256
ROPE_THETA = 10000.0
Q_BLOCK = 128
ATTN_SCALE = (QK_NOPE + QK_ROPE) ** -0.5
NEG_INF = -1e30
POOL_WINDOWS = (2, 4, 8, 16)
POOL_GROUP = 128
POOL_W = POOL_GROUP * len(POOL_WINDOWS)
POOL_HIST = max(POOL_WINDOWS) - 1
CONV_W = 512
CONV_K = 3
BRANCH_W = 512
N_BRANCH = 3
COL_SIZES = (Q_LORA, KV_LORA, QK_ROPE, POOL_W, CONV_W, CONV_W, CONV_W, N_BRANCH * D_MODEL)
D_IN = sum(COL_SIZES)
PEER_HEADS = 8
N_KEYS = 128
N_EXPERTS = N_KEYS * N_KEYS
D_KEY = 256
PEER_TOPK = 16
TOKEN_BLOCK = 128

kernel_name = 'hybrid_mla_pool_conv_peer_stream_step'


def _rmsnorm(x, g):
    xf = x.astype(jnp.float32)
    y = xf * lax.rsqrt(jnp.mean(xf * xf, axis=-1, keepdims=True) + EPS)
    return (y * g.astype(jnp.float32)).astype(x.dtype)


def _modulate(x, g, shift, scale):
    return _rmsnorm(x, g) * (1 + scale[:, None, :]) + shift[:, None, :]


def _rope(x, pos):
    half = x.shape[-1] // 2
    inv = ROPE_THETA ** (-jnp.arange(half, dtype=jnp.float32) / half)
    ang = pos.astype(jnp.float32)[:, None] * inv[None, :]
    ang = ang.reshape((ang.shape[0],) + (1,) * (x.ndim - 3) + (half,))
    cos, sin = jnp.cos(ang), jnp.sin(ang)
    xf = x.astype(jnp.float32)
    x1, x2 = xf[..., :half], xf[..., half:]
    return jnp.concatenate([x1 * cos - x2 * sin, x1 * sin + x2 * cos], axis=-1).astype(x.dtype)


def _split_cols(z):
    outs, off = [], 0
    for n in COL_SIZES:
        outs.append(z[..., off:off + n])
        off += n
    return outs


def _attn_core(qn, qr, kn, kr, v, mask):
    s = (jnp.einsum('bqhd,bshd->bhqs', qn, kn).astype(jnp.float32)
         + jnp.einsum('bqhr,bsr->bhqs', qr, kr).astype(jnp.float32)) * ATTN_SCALE
    if mask is not None:
        s = jnp.where(mask[None, None], s, NEG_INF)
    p = jax.nn.softmax(s, axis=-1)
    return jnp.einsum('bhqs,bshd->bqhd', p.astype(v.dtype), v)


def _expand_latent(ckv, w_ukv):
    kv = jnp.einsum('bsl,lhd->bshd', ckv, w_ukv)
    return kv[..., :QK_NOPE], kv[..., QK_NOPE:]


def _mla_prompt(qn, qr, ckv, kr, w_ukv):
    B, T = qn.shape[0], qn.shape[1]
    kn, v = _expand_latent(ckv, w_ukv)
    k_chunk = jnp.arange(T) // CHUNK

    def one_block(i):
        s0 = i * Q_BLOCK
        qn_b = lax.dynamic_slice_in_dim(qn, s0, Q_BLOCK, axis=1)
        qr_b = lax.dynamic_slice_in_dim(qr, s0, Q_BLOCK, axis=1)
        q_chunk = (s0 + jnp.arange(Q_BLOCK)) // CHUNK
        mask = k_chunk[None, :] <= q_chunk[:, None]
        return _attn_core(qn_b, qr_b, kn, kr, v, mask)

    out = lax.map(one_block, jnp.arange(T // Q_BLOCK))
    return jnp.moveaxis(out, 0, 1).reshape(B, T, N_HEADS * V_HEAD)


def _mla_sample(qn, qr, ckv_all, kr_all, w_ukv):
    B, T = qn.shape[0], qn.shape[1]
    kn, v = _expand_latent(ckv_all, w_ukv)
    return _attn_core(qn, qr, kn, kr_all, v, None).reshape(B, T, N_HEADS * V_HEAD)


def _pool_mixer(u, hist, n_hist, w_pool, pool_scale):
    T = u.shape[1]
    ext = jnp.concatenate([hist, u], axis=1)
    extf = ext.astype(jnp.float32)
    cs = jnp.pad(jnp.cumsum(extf, axis=1), ((0, 0), (1, 0), (0, 0)))
    t = jnp.arange(T)
    outs = []
    for g, w in enumerate(POOL_WINDOWS):
        sl = slice(g * POOL_GROUP, (g + 1) * POOL_GROUP)
        hi = cs[:, POOL_HIST + 1:POOL_HIST + 1 + T, sl]
        lo = cs[:, POOL_HIST + 1 - w:POOL_HIST + 1 - w + T, sl]
        cnt = jnp.minimum(t + 1 + n_hist, w).astype(jnp.float32)[None, :, None]
        d = (hi - lo) / cnt - extf[:, POOL_HIST:, sl]
        outs.append(jnp.einsum('btc,cd->btd', d.astype(u.dtype), w_pool[g]))
    return jnp.concatenate(outs, axis=-1) * pool_scale, ext[:, -POOL_HIST:]


def _conv_mixer(b_gate, c_gate, h, hist, w):
    u = c_gate * h
    ext = jnp.concatenate([hist, u], axis=1)
    T = u.shape[1]
    y = w[0] * ext[:, 0:T] + w[1] * ext[:, 1:T + 1] + w[2] * ext[:, 2:T + 2]
    return b_gate * y, ext[:, -(CONV_K - 1):]


def _peer(h, w_pq, sub_keys, u_tab, v_tab):
    n_tok = h.shape[0]
    pad = (-n_tok) % TOKEN_BLOCK
    hb = jnp.pad(h, ((0, pad), (0, 0))).reshape(-1, TOKEN_BLOCK, D_MODEL)
    half = D_KEY // 2
    k1 = sub_keys[0].astype(jnp.float32)
    k2 = sub_keys[1].astype(jnp.float32)

    def one_block(xb):
        q = jnp.einsum('nd,dhk->nhk', xb, w_pq).astype(jnp.float32)
        s1 = jnp.einsum('nhk,mk->nhm', q[..., :half], k1)
        s2 = jnp.einsum('nhk,mk->nhm', q[..., half:], k2)
        v1, i1 = lax.top_k(s1, PEER_TOPK)
        v2, i2 = lax.top_k(s2, PEER_TOPK)
        nb = xb.shape[0]
        cand = (v1[..., :, None] + v2[..., None, :]).reshape(nb, PEER_HEADS, PEER_TOPK * PEER_TOPK)
        cidx = (i1[..., :, None] * N_KEYS + i2[..., None, :]).reshape(nb, PEER_HEADS, PEER_TOPK * PEER_TOPK)
        sc, sel = lax.top_k(cand, PEER_TOPK)
        e = jnp.take_along_axis(cidx, sel, axis=-1)
        g = jax.nn.softmax(sc, axis=-1)
        ue = u_tab[e]
        ve = v_tab[e]
        a = jax.nn.gelu(jnp.einsum('nhkd,nd->nhk', ue, xb).astype(jnp.float32), approximate=False)
        return jnp.einsum('nhk,nhkd->nd', (g * a).astype(ve.dtype), ve)

    out = lax.map(one_block, hb).reshape(-1, D_MODEL)
    return out[:n_tok]


def _trunk(x, c, pos, n_hist, cache_kv, cache_kr, hist_pool, hist_conv, p):
    B, T, _ = x.shape
    new_kv, new_kr, new_pool, new_conv = [], [], [], []
    c_act = jax.nn.silu(c)
    for l in range(DEPTH):
        mod = c_act @ p['w_ada'][l] + p['b_ada'][l]
        sh1, sc1, gt1, sh2, sc2, gt2 = jnp.split(mod, 6, axis=-1)
        h = _modulate(x, p['g_mix'][l], sh1, sc1)
        z = h @ p['w_in'][l]
        cq, ckv_raw, kr_raw, u_pool, b_gate, c_gate, h_conv, g_raw = _split_cols(z)
        q = jnp.einsum('btl,lhd->bthd', _rmsnorm(cq, p['g_q'][l]), p['w_uq'][l])
        q_nope = q[..., :QK_NOPE]
        q_rope = _rope(q[..., QK_NOPE:], pos)
        ckv = _rmsnorm(ckv_raw, p['g_kv'][l])
        k_rope = _rope(kr_raw, pos)
        new_kv.append(ckv)
        new_kr.append(k_rope)
        if cache_kv is None:
            attn = _mla_prompt(q_nope, q_rope, ckv, k_rope, p['w_ukv'][l])
        else:
            attn = _mla_sample(q_nope, q_rope,
                               jnp.concatenate([cache_kv[l], ckv], axis=1),
                               jnp.concatenate([cache_kr[l], k_rope], axis=1),
                               p['w_ukv'][l])
        pool, st_p = _pool_mixer(u_pool, hist_pool[l], n_hist, p['w_pool'][l], p['pool_scale'][l])
        conv, st_c = _conv_mixer(b_gate, c_gate, h_conv, hist_conv[l], p['conv_w'][l])
        new_pool.append(st_p)
        new_conv.append(st_c)
        br = jnp.einsum('btnc,ncd->btnd', jnp.stack([attn, pool, conv], axis=2), p['w_branch'][l])
        gates = jax.nn.sigmoid(g_raw.reshape(B, T, N_BRANCH, D_MODEL))
        mixed = jnp.sum(gates * br, axis=2) @ p['w_out'][l]
        x = x + gt1[:, None, :] * mixed
        h2 = _modulate(x, p['g_ffn'][l], sh2, sc2)
        ffn = _peer(h2.reshape(B * T, D_MODEL), p['peer_wq'][l], p['peer_keys'][l],
                    p['peer_u'][l], p['peer_v'][l]).reshape(B, T, D_MODEL)
        x = x + gt2[:, None, :] * ffn
    y = _rmsnorm(x, p['g_final'])
    return y, jnp.stack(new_kv), jnp.stack(new_kr), jnp.stack(new_pool), jnp.stack(new_conv)


def setup_inputs(seed: int = 0) -> dict:
    key = jax.random.key(seed)
    ks = jax.random.split(key, 32)
    f32 = jnp.float32
    nrm = lambda k, shape, s: jax.random.normal(k, shape, f32) * s
    return {
        'x_prompt': nrm(ks[0], (BATCH, SEQ, D_MODEL), 1.0),
        'x_sample': nrm(ks[1], (DEC_BATCH, DEC_SEQ, D_MODEL), 1.0),
        'cache_kv_latent': nrm(ks[2], (DEPTH, DEC_BATCH, PAST_LEN, KV_LORA), 1.0),
        'cache_k_rope': nrm(ks[3], (DEPTH, DEC_BATCH, PAST_LEN, QK_ROPE), 1.0),
        'state_pool': nrm(ks[4], (DEPTH, DEC_BATCH, POOL_HIST, POOL_W), 1.0),
        'state_conv': nrm(ks[5], (DEPTH, DEC_BATCH, CONV_K - 1, CONV_W), 1.0),
        'c_prompt': nrm(ks[6], (BATCH, D_MODEL), 1.0),
        'c_sample': nrm(ks[7], (DEC_BATCH, D_MODEL), 1.0),
        'w_ada': nrm(ks[8], (DEPTH, D_MODEL, 6 * D_MODEL), 0.5 * D_MODEL ** -0.5),
        'b_ada': nrm(ks[9], (DEPTH, 6 * D_MODEL), 0.01),
        'g_mix': 1.0 + nrm(ks[10], (DEPTH, D_MODEL), 0.05),
        'w_in': nrm(ks[11], (DEPTH, D_MODEL, D_IN), D_MODEL ** -0.5),
        'g_q': 1.0 + nrm(ks[12], (DEPTH, Q_LORA), 0.05),
        'w_uq': nrm(ks[13], (DEPTH, Q_LORA, N_HEADS, QK_NOPE + QK_ROPE), Q_LORA ** -0.5),
        'g_kv': 1.0 + nrm(ks[14], (DEPTH, KV_LORA), 0.05),
        'w_ukv': nrm(ks[15], (DEPTH, KV_LORA, N_HEADS, QK_NOPE + V_HEAD), KV_LORA ** -0.5),
        'w_pool': nrm(ks[16], (DEPTH, len(POOL_WINDOWS), POOL_GROUP, POOL_GROUP), POOL_GROUP ** -0.5),
        'pool_scale': 1.0 + nrm(ks[17], (DEPTH, POOL_W), 0.1),
        'conv_w': nrm(ks[18], (DEPTH, CONV_K, CONV_W), CONV_K ** -0.5),
        'w_branch': nrm(ks[19], (DEPTH, N_BRANCH, BRANCH_W, D_MODEL), BRANCH_W ** -0.5),
        'w_out': nrm(ks[20], (DEPTH, D_MODEL, D_MODEL), D_MODEL ** -0.5),
        'g_ffn': 1.0 + nrm(ks[21], (DEPTH, D_MODEL), 0.05),
        'peer_wq': nrm(ks[22], (DEPTH, D_MODEL, PEER_HEADS, D_KEY), D_MODEL ** -0.5),
        'peer_keys': nrm(ks[23], (DEPTH, 2, N_KEYS, D_KEY // 2), (D_KEY // 2) ** -0.5),
        'peer_u': nrm(ks[24], (DEPTH, N_EXPERTS, D_MODEL), D_MODEL ** -0.5),
        'peer_v': nrm(ks[25], (DEPTH, N_EXPERTS, D_MODEL), PEER_HEADS ** -0.5),
        'g_final': 1.0 + nrm(ks[26], (D_MODEL,), 0.05),
    }


def reference(x_prompt, x_sample, cache_kv_latent, cache_k_rope, state_pool, state_conv,
              c_prompt, c_sample, w_ada, b_ada, g_mix, w_in, g_q, w_uq, g_kv, w_ukv,
              w_pool, pool_scale, conv_w, w_branch, w_out, g_ffn, peer_wq, peer_keys,
              peer_u, peer_v, g_final):
    p = {'w_ada': w_ada, 'b_ada': b_ada, 'g_mix': g_mix, 'w_in': w_in, 'g_q': g_q,
         'w_uq': w_uq, 'g_kv': g_kv, 'w_ukv': w_ukv, 'w_pool': w_pool,
         'pool_scale': pool_scale, 'conv_w': conv_w, 'w_branch': w_branch,
         'w_out': w_out, 'g_ffn': g_ffn, 'peer_wq': peer_wq, 'peer_keys': peer_keys,
         'peer_u': peer_u, 'peer_v': peer_v, 'g_final': g_final}
    B, T = x_prompt.shape[0], x_prompt.shape[1]
    zp = jnp.zeros((DEPTH, B, POOL_HIST, POOL_W), x_prompt.dtype)
    zc = jnp.zeros((DEPTH, B, CONV_K - 1, CONV_W), x_prompt.dtype)
    y_prompt, p_kv, p_kr, p_pool, p_conv = _trunk(
        x_prompt, c_prompt, jnp.arange(T), 0, None, None, zp, zc, p)
    Ts = x_sample.shape[1]
    y_sample, s_kv, s_kr, s_pool, s_conv = _trunk(
        x_sample, c_sample, PAST_LEN + jnp.arange(Ts), min(PAST_LEN, POOL_HIST),
        cache_kv_latent, cache_k_rope, state_pool, state_conv, p)
    return (y_prompt, y_sample, p_kv, p_kr, p_pool, p_conv, s_kv, s_kr, s_pool, s_conv)
```

```python
import functools
import math

import jax
import jax.numpy as jnp
from jax import lax
from jax.experimental import pallas as pl
from jax.experimental.pallas import tpu as pltpu

F32 = jnp.float32
BF16 = jnp.bfloat16

EPS = 1e-6
N_HEADS = 8
QK_NOPE = 64
QK_ROPE = 32
V_HEAD = 64
Q_LORA = 512
KV_LORA = 256
ROPE_THETA = 10000.0
CHUNK = 64
ATTN_SCALE = (QK_NOPE + QK_ROPE) ** -0.5
NEG_INF = -1e30
POOL_WINDOWS = (2, 4, 8, 16)
POOL_GROUP = 128
POOL_W = 512
POOL_HIST = 15
CONV_W = 512
CONV_K = 3
PEER_HEADS = 8
N_KEYS = 128
D_KEY = 256
PEER_TOPK = 16

LANES = 128
VMEM_LIMIT_BYTES = 56 * 2**20

HEAD_W = LANES
HIST_ROWS = 16
CONV_HIST_ROWS = 8

COL_U, COL_B, COL_C, COL_H = 0, 512, 1024, 1536
COL_CQ = 2048
COL_CKV = 2560
COL_KR = 2816
COL_G = 3072
D_IN_PAD = 6144


def _cparams(sem):
    return pltpu.CompilerParams(dimension_semantics=sem, vmem_limit_bytes=VMEM_LIMIT_BYTES)


def _dot(a, b):
    return jnp.dot(a, b, preferred_element_type=F32)


def _dot_nt(a, b):
    return lax.dot_general(a, b, (((1,), (1,)), ((), ())), preferred_element_type=F32)


def _ada_kernel(c_ref, w_ref, b_ref, o_ref):
    c = c_ref[...]
    act = c * jax.nn.sigmoid(c)
    o_ref[0] = _dot(act.astype(BF16), w_ref[0].astype(BF16)) + b_ref[0]


def _ada(c_all, w_ada, b_ada):
    depth, d, n6 = w_ada.shape
    bp = c_all.shape[0]
    tn = 1536
    return pl.pallas_call(
        _ada_kernel,
        grid=(depth, n6 // tn),
        in_specs=[
            pl.BlockSpec((bp, d), lambda l, j: (0, 0)),
            pl.BlockSpec((1, d, tn), lambda l, j: (l, 0, j)),
            pl.BlockSpec((1, 1, tn), lambda l, j: (l, 0, j)),
        ],
        out_specs=pl.BlockSpec((1, bp, tn), lambda l, j: (l, 0, j)),
        out_shape=jax.ShapeDtypeStruct((depth, bp, n6), F32),
        compiler_params=_cparams(("arbitrary", "arbitrary")),
        name="ada",
    )(c_all, w_ada, b_ada.reshape(depth, 1, n6))


def _modnorm(x, g, sh, sc):
    ms = jnp.mean(x * x, axis=-1, keepdims=True)
    y = x * lax.rsqrt(ms + EPS) * g
    return y * (1.0 + sc) + sh


def _rmsnorm2(x, g):
    ms = jnp.mean(x * x, axis=-1, keepdims=True)
    return x * lax.rsqrt(ms + EPS) * g


def _row_plan(b, t, rows):
    if t >= rows:
        assert t % rows == 0
        return 1, rows
    assert rows % t == 0 and b % (rows // t) == 0
    return rows // t, t


def _inproj_kernel(x_ref, g_ref, sh_ref, sc_ref, w_ref, o_ref, *, col_chunk):
    gg, tg, d = x_ref.shape
    h = _modnorm(x_ref[...], g_ref[...], sh_ref[...], sc_ref[...])
    hb = h.reshape(gg * tg, d).astype(BF16)
    n = w_ref.shape[1]
    for c in range(0, n, col_chunk):
        o_ref[:, c:c + col_chunk] = _dot(hb, w_ref[:, c:c + col_chunk])


def _inproj(x3, g, sh, sc, w, rows):
    b, t, d = x3.shape
    gg, tg = _row_plan(b, t, rows)
    nj = t // tg
    n = w.shape[1]
    return pl.pallas_call(
        functools.partial(_inproj_kernel, col_chunk=512),
        grid=(b // gg, nj),
        in_specs=[
            pl.BlockSpec((gg, tg, d), lambda i, j: (i, j, 0)),
            pl.BlockSpec((1, d), lambda i, j: (0, 0)),
            pl.BlockSpec((gg, 1, d), lambda i, j: (i, 0, 0)),
            pl.BlockSpec((gg, 1, d), lambda i, j: (i, 0, 0)),
            pl.BlockSpec((d, n), lambda i, j: (0, 0)),
        ],
        out_specs=pl.BlockSpec((gg * tg, n), lambda i, j: (i * nj + j, 0)),
        out_shape=jax.ShapeDtypeStruct((b * t, n), F32),
        compiler_params=_cparams(("arbitrary", "arbitrary")),
        name="inproj",
    )(x3, g, sh, sc, w)


def _rope_group(z, cc, ss):
    return z * cc + pltpu.roll(z, HEAD_W - QK_ROPE, axis=1) * ss


def _attn_prep_kernel(cq_ref, ckv_ref, kr_ref, cc_ref, ss_ref, gq_ref, wq_ref, gkv_ref,
                      wk_ref, wv_ref, q_ref, k_ref, v_ref, kv_ref, kro_ref):
    cck = cc_ref[...]
    ss = ss_ref[...]
    lane = lax.broadcasted_iota(jnp.int32, cck.shape, 1)
    ccq = jnp.where(lane < QK_NOPE, 1.0, cck)
    qn = _rmsnorm2(cq_ref[...], gq_ref[...]).astype(BF16)
    ckv = _rmsnorm2(ckv_ref[...], gkv_ref[...])
    kv_ref[...] = ckv
    ckv_b = ckv.astype(BF16)
    kr = _rope_group(kr_ref[...], cck, ss)
    kro_ref[...] = kr
    v_ref[...] = _dot(ckv_b, wv_ref[...]).astype(BF16)
    for h in range(N_HEADS):
        sl = slice(h * HEAD_W, (h + 1) * HEAD_W)
        zq = _dot(qn, wq_ref[:, sl])
        q_ref[:, sl] = (_rope_group(zq, ccq, ss) * ATTN_SCALE).astype(BF16)
        k_ref[:, sl] = (_dot(ckv_b, wk_ref[:, sl]) + kr).astype(BF16)


def _attn_prep(z, cc, ss, gq, wq, gkv, wk, wv, rows, t):
    n = z.shape[0]
    npos = cc.shape[0] // rows
    row = lambda i: (i, 0)
    const = lambda i: (0, 0)
    return pl.pallas_call(
        _attn_prep_kernel,
        grid=(n // rows,),
        in_specs=[
            pl.BlockSpec((rows, Q_LORA), lambda i: (i, COL_CQ // Q_LORA)),
            pl.BlockSpec((rows, KV_LORA), lambda i: (i, COL_CKV // KV_LORA)),
            pl.BlockSpec((rows, HEAD_W), lambda i: (i, COL_KR // HEAD_W)),
            pl.BlockSpec((rows, HEAD_W), lambda i: (i % npos, 0)),
            pl.BlockSpec((rows, HEAD_W), lambda i: (i % npos, 0)),
            pl.BlockSpec((1, Q_LORA), const),
            pl.BlockSpec(wq.shape, const),
            pl.BlockSpec((1, KV_LORA), const),
            pl.BlockSpec(wk.shape, const),
            pl.BlockSpec(wv.shape, const),
        ],
        out_specs=[
            pl.BlockSpec((rows, N_HEADS * HEAD_W), row),
            pl.BlockSpec((rows, N_HEADS * HEAD_W), row),
            pl.BlockSpec((rows, N_HEADS * V_HEAD), row),
            pl.BlockSpec((rows, KV_LORA), row),
            pl.BlockSpec((rows, HEAD_W), row),
        ],
        out_shape=[
            jax.ShapeDtypeStruct((n, N_HEADS * HEAD_W), BF16),
            jax.ShapeDtypeStruct((n, N_HEADS * HEAD_W), BF16),
            jax.ShapeDtypeStruct((n, N_HEADS * V_HEAD), BF16),
            jax.ShapeDtypeStruct((n, KV_LORA), F32),
            jax.ShapeDtypeStruct((n, HEAD_W), F32),
        ],
        compiler_params=_cparams(("arbitrary",)),
        name="attn_prep",
    )(z, z, z, cc, ss, gq, wq, gkv, wk, wv)


def _cache_expand_kernel(ckv_ref, kr_ref, wk_ref, wv_ref, place_ref, k_ref, v_ref):
    ckv_b = ckv_ref[...].astype(BF16)
    krp = _dot(kr_ref[...].astype(BF16), place_ref[...])
    v_ref[...] = _dot(ckv_b, wv_ref[...]).astype(BF16)
    for h in range(N_HEADS):
        sl = slice(h * HEAD_W, (h + 1) * HEAD_W)
        k_ref[:, sl] = (_dot(ckv_b, wk_ref[:, sl]) + krp).astype(BF16)


def _cache_expand(ckv, kr, wk, wv, place, rows):
    n = ckv.shape[0]
    row = lambda i: (i, 0)
    const = lambda i: (0, 0)
    return pl.pallas_call(
        _cache_expand_kernel,
        grid=(n // rows,),
        in_specs=[
            pl.BlockSpec((rows, KV_LORA), row),
            pl.BlockSpec((rows, QK_ROPE), row),
            pl.BlockSpec(wk.shape, const),
            pl.BlockSpec(wv.shape, const),
            pl.BlockSpec(place.shape, const),
        ],
        out_specs=[
            pl.BlockSpec((rows, N_HEADS * HEAD_W), row),
            pl.BlockSpec((rows, N_HEADS * V_HEAD), row),
        ],
        out_shape=[
            jax.ShapeDtypeStruct((n, N_HEADS * HEAD_W), BF16),
            jax.ShapeDtypeStruct((n, N_HEADS * V_HEAD), BF16),
        ],
        compiler_params=_cparams(("arbitrary",)),
        name="cache_expand",
    )(ckv, kr, wk, wv, place)


def _flash_step(q_ref, k_ref, v_ref, m_ref, l_ref, acc_ref, masked):
    tq, tk = q_ref.shape[0], k_ref.shape[0]
    if masked:
        rq = lax.broadcasted_iota(jnp.int32, (tq, tk), 0) // CHUNK
        ck = lax.broadcasted_iota(jnp.int32, (tq, tk), 1) // CHUNK
        keep = ck <= rq
    for h in range(N_HEADS):
        sl = slice(h * HEAD_W, (h + 1) * HEAD_W)
        vs = slice(h * V_HEAD, (h + 1) * V_HEAD)
        s = _dot_nt(q_ref[:, sl], k_ref[:, sl])
        if masked:
            s = jnp.where(keep, s, NEG_INF)
        m_prev = m_ref[h]
        m_new = jnp.maximum(m_prev, jnp.max(s, axis=1, keepdims=True))
        alpha = jnp.exp(m_prev - m_new)
        p = jnp.exp(s - m_new)
        l_ref[h] = alpha * l_ref[h] + jnp.sum(p, axis=1, keepdims=True)
        acc_ref[:, vs] = alpha * acc_ref[:, vs] + _dot(p.astype(BF16), v_ref[:, vs])
        m_ref[h] = m_new


def _flash_kernel(q_ref, k_ref, v_ref, o_ref, m_ref, l_ref, acc_ref):
    qi = pl.program_id(1)
    ki = pl.program_id(2)

    @pl.when(ki == 0)
    def _():
        m_ref[...] = jnp.full(m_ref.shape, NEG_INF, F32)
        l_ref[...] = jnp.zeros(l_ref.shape, F32)
        acc_ref[...] = jnp.zeros(acc_ref.shape, F32)

    @pl.when(ki < qi)
    def _():
        _flash_step(q_ref, k_ref, v_ref, m_ref, l_ref, acc_ref, masked=False)

    @pl.when(ki == qi)
    def _():
        _flash_step(q_ref, k_ref, v_ref, m_ref, l_ref, acc_ref, masked=True)
        for h in range(N_HEADS):
            vs = slice(h * V_HEAD, (h + 1) * V_HEAD)
            o_ref[:, vs] = (acc_ref[:, vs] / l_ref[h]).astype(o_ref.dtype)


def _flash_prompt(q, k, v, b, t, tile):
    nq = t // tile
    return pl.pallas_call(
        _flash_kernel,
        grid=(b, nq, nq),
        in_specs=[
            pl.BlockSpec((tile, N_HEADS * HEAD_W), lambda bi, qi, ki: (bi * nq + qi, 0)),
            pl.BlockSpec((tile, N_HEADS * HEAD_W),
                         lambda bi, qi, ki: (bi * nq + jnp.minimum(ki, qi), 0)),
            pl.BlockSpec((tile, N_HEADS * V_HEAD),
                         lambda bi, qi, ki: (bi * nq + jnp.minimum(ki, qi), 0)),
        ],
        out_specs=pl.BlockSpec((tile, N_HEADS * V_HEAD), lambda bi, qi, ki: (bi * nq + qi, 0)),
        out_shape=jax.ShapeDtypeStruct((b * t, N_HEADS * V_HEAD), BF16),
        scratch_shapes=[
            pltpu.VMEM((N_HEADS, tile, 1), F32),
            pltpu.VMEM((N_HEADS, tile, 1), F32),
            pltpu.VMEM((tile, N_HEADS * V_HEAD), F32),
        ],
        compiler_params=_cparams(("arbitrary", "arbitrary", "arbitrary")),
        name="flash_prompt",
    )(q, k, v)


def _attn_sample_kernel(q_ref, kc_ref, vc_ref, kn_ref, vn_ref, o_ref):
    for h in range(N_HEADS):
        sl = slice(h * HEAD_W, (h + 1) * HEAD_W)
        vs = slice(h * V_HEAD, (h + 1) * V_HEAD)
        qh = q_ref[:, sl]
        sc = _dot_nt(qh, kc_ref[:, sl])
        sn = _dot_nt(qh, kn_ref[:, sl])
        m = jnp.maximum(jnp.max(sc, axis=1, keepdims=True), jnp.max(sn, axis=1, keepdims=True))
        pc = jnp.exp(sc - m)
        pn = jnp.exp(sn - m)
        den = jnp.sum(pc, axis=1, keepdims=True) + jnp.sum(pn, axis=1, keepdims=True)
        o = _dot(pc.astype(BF16), vc_ref[:, vs]) + _dot(pn.astype(BF16), vn_ref[:, vs])
        o_ref[:, vs] = (o / den).astype(o_ref.dtype)


def _attn_sample(q, kc, vc, kn, vn, b, t, past):
    return pl.pallas_call(
        _attn_sample_kernel,
        grid=(b,),
        in_specs=[
            pl.BlockSpec((t, N_HEADS * HEAD_W), lambda i: (i, 0)),
            pl.BlockSpec((past, N_HEADS * HEAD_W), lambda i: (i, 0)),
            pl.BlockSpec((past, N_HEADS * V_HEAD), lambda i: (i, 0)),
            pl.BlockSpec((t, N_HEADS * HEAD_W), lambda i: (i, 0)),
            pl.BlockSpec((t, N_HEADS * V_HEAD), lambda i: (i, 0)),
        ],
        out_specs=pl.BlockSpec((t, N_HEADS * V_HEAD), lambda i: (i, 0)),
        out_shape=jax.ShapeDtypeStruct((b * t, N_HEADS * V_HEAD), BF16),
        compiler_params=_cparams(("arbitrary",)),
        name="attn_sample",
    )(q, kc, vc, kn, vn)


def _mix_kernel(x_ref, gt_ref, ubch_ref, g_ref, attn_ref, pu_ref, pch_ref, hp_ref, hc_ref,
                wpool_ref, pscale_ref, convw_ref, wbr_ref, wout_ref,
                xo_ref, ptail_ref, ctail_ref, extp_ref, extc_ref, *, n_hist):
    gg, tg, d = x_ref.shape
    j = pl.program_id(1)
    first = j == 0

    ubch = ubch_ref[...]
    u = ubch[:, COL_U:COL_U + POOL_W].reshape(gg, tg, POOL_W)
    bgate = ubch[:, COL_B:COL_B + CONV_W].reshape(gg, tg, CONV_W)
    cu = (ubch[:, COL_C:COL_C + CONV_W] * ubch[:, COL_H:COL_H + CONV_W]).reshape(gg, tg, CONV_W)

    hist_p = jnp.where(first, hp_ref[...], pu_ref[...].reshape(1, HIST_ROWS, POOL_W))
    pch = pch_ref[...]
    prev_cu = (pch[:, :CONV_W] * pch[:, CONV_W:])[HIST_ROWS - CONV_HIST_ROWS:]
    hist_c = jnp.where(first, hc_ref[...], prev_cu.reshape(1, CONV_HIST_ROWS, CONV_W))

    extp_ref[:, :HIST_ROWS, :] = hist_p
    extp_ref[:, HIST_ROWS:, :] = u
    extc_ref[:, :CONV_HIST_ROWS, :] = hist_c
    extc_ref[:, CONV_HIST_ROWS:, :] = cu
    ptail_ref[...] = u[:, tg - HIST_ROWS:, :]
    ctail_ref[...] = cu[:, tg - CONV_HIST_ROWS:, :]

    tpos = j * tg + lax.broadcasted_iota(jnp.int32, (1, tg, 1), 1)
    pooled = []
    for gi, w in enumerate(POOL_WINDOWS):
        cs = slice(gi * POOL_GROUP, (gi + 1) * POOL_GROUP)
        acc = u[:, :, cs]
        for kk in range(1, w):
            acc = acc + extp_ref[:, HIST_ROWS - kk:HIST_ROWS - kk + tg, cs]
        cnt = jnp.minimum(tpos + 1 + n_hist, w).astype(F32)
        dd = acc / cnt - u[:, :, cs]
        pooled.append(_dot(dd.reshape(gg * tg, POOL_GROUP).astype(BF16), wpool_ref[gi]))
    pool = jnp.concatenate(pooled, axis=1) * pscale_ref[...]

    cw = convw_ref[...]
    yc = (cw[0:1, :] * extc_ref[:, CONV_HIST_ROWS - 2:CONV_HIST_ROWS - 2 + tg, :]
          + cw[1:2, :] * extc_ref[:, CONV_HIST_ROWS - 1:CONV_HIST_ROWS - 1 + tg, :]
          + cw[2:3, :] * cu)
    conv = (bgate * yc).reshape(gg * tg, CONV_W)

    gates = jax.nn.sigmoid(g_ref[...])
    mixed = (gates[:, 0:d] * _dot(attn_ref[...], wbr_ref[0])
             + gates[:, d:2 * d] * _dot(pool.astype(BF16), wbr_ref[1])
             + gates[:, 2 * d:3 * d] * _dot(conv.astype(BF16), wbr_ref[2]))
    out = _dot(mixed.astype(BF16), wout_ref[...])
    xo_ref[...] = x_ref[...] + gt_ref[...] * out.reshape(gg, tg, d)


def _mix(x3, gt, z, attn, hist_p, hist_c, wpool, pscale, convw, wbr, wout, rows, n_hist):
    b, t, d = x3.shape
    gg, tg = _row_plan(b, t, rows)
    nj = t // tg
    rb = lambda i, j: i * nj + j
    hpb = tg // HIST_ROWS

    def prev_rows(i, j):
        return jnp.maximum(rb(i, j) * hpb - 1, 0)

    const2 = lambda i, j: (0, 0)
    const3 = lambda i, j: (0, 0, 0)
    return pl.pallas_call(
        functools.partial(_mix_kernel, n_hist=n_hist),
        grid=(b // gg, nj),
        in_specs=[
            pl.BlockSpec((gg, tg, d), lambda i, j: (i, j, 0)),
            pl.BlockSpec((gg, 1, d), lambda i, j: (i, 0, 0)),
            pl.BlockSpec((gg * tg, COL_CQ), lambda i, j: (rb(i, j), 0)),
            pl.BlockSpec((gg * tg, 3 * d), lambda i, j: (rb(i, j), COL_G // (3 * d))),
            pl.BlockSpec((gg * tg, N_HEADS * V_HEAD), lambda i, j: (rb(i, j), 0)),
            pl.BlockSpec((HIST_ROWS, POOL_W), lambda i, j: (prev_rows(i, j), 0)),
            pl.BlockSpec((HIST_ROWS, 2 * CONV_W), lambda i, j: (prev_rows(i, j), COL_C // (2 * CONV_W))),
            pl.BlockSpec((gg, HIST_ROWS, POOL_W), lambda i, j: (i, 0, 0)),
            pl.BlockSpec((gg, CONV_HIST_ROWS, CONV_W), lambda i, j: (i, 0, 0)),
            pl.BlockSpec(wpool.shape, const3),
            pl.BlockSpec(pscale.shape, const2),
            pl.BlockSpec(convw.shape, const2),
            pl.BlockSpec(wbr.shape, const3),
            pl.BlockSpec(wout.shape, const2),
        ],
        out_specs=[
            pl.BlockSpec((gg, tg, d), lambda i, j: (i, j, 0)),
            pl.BlockSpec((gg, HIST_ROWS, POOL_W), lambda i, j: (i, 0, 0)),
            pl.BlockSpec((gg, CONV_HIST_ROWS, CONV_W), lambda i, j: (i, 0, 0)),
        ],
        out_shape=[
            jax.ShapeDtypeStruct((b, t, d), F32),
            jax.ShapeDtypeStruct((b, HIST_ROWS, POOL_W), F32),
            jax.ShapeDtypeStruct((b, CONV_HIST_ROWS, CONV_W), F32),
        ],
        scratch_shapes=[
            pltpu.VMEM((gg, HIST_ROWS + tg, POOL_W), F32),
            pltpu.VMEM((gg, CONV_HIST_ROWS + tg, CONV_W), F32),
        ],
        compiler_params=_cparams(("arbitrary", "arbitrary")),
        name="mix_merge",
    )(x3, gt, z, z, attn, z, z, hist_p, hist_c, wpool, pscale, convw, wbr, wout)


def _peer_query_kernel(x_ref, g_ref, sh_ref, sc_ref, wq_ref, keys_ref, ht_ref, s1_ref, s2_ref):
    gg, tg, d = x_ref.shape
    h = _modnorm(x_ref[...], g_ref[...], sh_ref[...], sc_ref[...]).reshape(gg * tg, d)
    ht_ref[...] = h.T.astype(BF16)
    hb = h.astype(BF16)
    half = D_KEY // 2
    k1 = keys_ref[0]
    k2 = keys_ref[1]
    for hh in range(PEER_HEADS):
        q = _dot(hb, wq_ref[:, hh * D_KEY:(hh + 1) * D_KEY]).astype(BF16)
        s1_ref[hh] = _dot_nt(k1, q[:, :half])
        s2_ref[hh] = _dot_nt(k2, q[:, half:])


def _peer_query(x3, g, sh, sc, wq, keys, rows):
    b, t, d = x3.shape
    gg, tg = _row_plan(b, t, rows)
    nj = t // tg
    n = b * t
    tok = lambda i, j: (0, 0, i * nj + j)
    return pl.pallas_call(
        _peer_query_kernel,
        grid=(b // gg, nj),
        in_specs=[
            pl.BlockSpec((gg, tg, d), lambda i, j: (i, j, 0)),
            pl.BlockSpec((1, d), lambda i, j: (0, 0)),
            pl.BlockSpec((gg, 1, d), lambda i, j: (i, 0, 0)),
            pl.BlockSpec((gg, 1, d), lambda i, j: (i, 0, 0)),
            pl.BlockSpec(wq.shape, lambda i, j: (0, 0)),
            pl.BlockSpec(keys.shape, lambda i, j: (0, 0, 0)),
        ],
        out_specs=[
            pl.BlockSpec((d, gg * tg), lambda i, j: (0, i * nj + j)),
            pl.BlockSpec((PEER_HEADS, N_KEYS, gg * tg), tok),
            pl.BlockSpec((PEER_HEADS, N_KEYS, gg * tg), tok),
        ],
        out_shape=[
            jax.ShapeDtypeStruct((d, n), BF16),
            jax.ShapeDtypeStruct((PEER_HEADS, N_KEYS, n), F32),
            jax.ShapeDtypeStruct((PEER_HEADS, N_KEYS, n), F32),
        ],
        compiler_params=_cparams(("arbitrary", "arbitrary")),
        name="peer_query",
    )(x3, g, sh, sc, wq, keys)


def _top_ranks(s):
    nk, r = s.shape
    iota = lax.broadcasted_iota(jnp.int32, (nk, r), 0)
    iota_k = lax.broadcasted_iota(jnp.int32, (PEER_TOPK, r), 0)
    rank = jnp.full((nk, r), 99.0, F32)
    vals = jnp.zeros((PEER_TOPK, r), F32)
    x = s
    for it in range(PEER_TOPK):
        m = jnp.max(x, axis=0, keepdims=True)
        idx = jnp.min(jnp.where(x == m, iota, nk), axis=0, keepdims=True)
        hit = iota == idx
        rank = jnp.where(hit, float(it), rank)
        x = jnp.where(hit, -jnp.inf, x)
        vals = jnp.where(iota_k == it, m, vals)
    return rank, vals


def _peer_select_kernel(s1_ref, s2_ref, c_ref, lr_ref, rk2_ref, d_ref):
    s1 = s1_ref[0]
    s2 = s2_ref[0]
    r = s1.shape[1]
    rank1, v1 = _top_ranks(s1)
    rank2, v2 = _top_ranks(s2)
    iota_k = lax.broadcasted_iota(jnp.int32, (PEER_TOPK, r), 0)
    cnt = jnp.zeros((PEER_TOPK, r), jnp.int32)
    front = v1 + v2[0:1, :]
    top = front[0:1, :]
    zsum = jnp.zeros((1, r), F32)
    for _ in range(PEER_TOPK):
        m = jnp.max(front, axis=0, keepdims=True)
        a = jnp.min(jnp.where(front == m, iota_k, PEER_TOPK), axis=0, keepdims=True)
        hit = iota_k == a
        zsum = zsum + jnp.exp(m - top)
        cnt = jnp.where(hit, cnt + 1, cnt)
        nxt = jnp.full((PEER_TOPK, r), -jnp.inf, F32)
        for bcol in range(1, PEER_TOPK):
            nxt = jnp.where(cnt == bcol, v2[bcol:bcol + 1, :], nxt)
        front = jnp.where(hit, v1 + nxt, front)
    cntf = cnt.astype(F32)
    lr = jnp.zeros(rank1.shape, F32)
    for a in range(PEER_TOPK):
        lr = jnp.where(rank1 == float(a), cntf[a:a + 1, :], lr)
    c_ref[0] = jnp.exp(s1 - v1[0:1, :]) / zsum
    lr_ref[0] = lr
    rk2_ref[0] = rank2
    d_ref[0] = jnp.exp(s2 - v2[0:1, :])


def _peer_select(s1, s2, lanes):
    hh, nk, n = s1.shape
    spec = pl.BlockSpec((1, nk, lanes), lambda i, h: (h, 0, i))
    shp = jax.ShapeDtypeStruct((hh, nk, n), F32)
    return pl.pallas_call(
        _peer_select_kernel,
        grid=(n // lanes, hh),
        in_specs=[spec, spec],
        out_specs=[spec, spec, spec, spec],
        out_shape=[shp, shp, shp, shp],
        compiler_params=_cparams(("arbitrary", "arbitrary")),
        name="peer_select",
    )(s1, s2)


def _gelu(x):
    return 0.5 * x * (1.0 + lax.erf(x * (2.0 ** -0.5)))


def _peer_dense_kernel(x_ref, gt_ref, ht_ref, u_ref, vt_ref, c_ref, lr_ref, rk2_ref, d_ref,
                       xo_ref, w_ref, acc_ref, *, rows_per_chunk):
    gg, tg, d = x_ref.shape
    e = pl.program_id(2)

    @pl.when(e == 0)
    def _():
        acc_ref[...] = jnp.zeros(acc_ref.shape, F32)

    a_t = _dot(u_ref[...], ht_ref[...])
    act = _gelu(a_t)
    for ii in range(rows_per_chunk):
        gate = None
        for hh in range(PEER_HEADS):
            term = c_ref[hh, ii:ii + 1, :] * jnp.where(
                rk2_ref[hh] < lr_ref[hh, ii:ii + 1, :], d_ref[hh], 0.0)
            gate = term if gate is None else gate + term
        sl = slice(ii * N_KEYS, (ii + 1) * N_KEYS)
        w_ref[sl, :] = (act[sl, :] * gate).astype(BF16)
    acc_ref[...] += _dot(vt_ref[...], w_ref[...])

    @pl.when(e == pl.num_programs(2) - 1)
    def _():
        xo_ref[...] = x_ref[...] + gt_ref[...] * acc_ref[...].T.reshape(gg, tg, d)


def _peer_dense(x3, gt, ht, u, vt, c, lr, rk2, dd, rows, rows_per_chunk):
    b, t, d = x3.shape
    gg, tg = _row_plan(b, t, rows)
    nj = t // tg
    ne = u.shape[0]
    ec = rows_per_chunk * N_KEYS
    tokb = lambda i, j, e: i * nj + j
    full = pl.BlockSpec((PEER_HEADS, N_KEYS, rows), lambda i, j, e: (0, 0, tokb(i, j, e)))
    part = pl.BlockSpec((PEER_HEADS, rows_per_chunk, rows), lambda i, j, e: (0, e, tokb(i, j, e)))
    return pl.pallas_call(
        functools.partial(_peer_dense_kernel, rows_per_chunk=rows_per_chunk),
        grid=(b // gg, nj, ne // ec),
        in_specs=[
            pl.BlockSpec((gg, tg, d), lambda i, j, e: (i, j, 0)),
            pl.BlockSpec((gg, 1, d), lambda i, j, e: (i, 0, 0)),
            pl.BlockSpec((d, rows), lambda i, j, e: (0, tokb(i, j, e))),
            pl.BlockSpec((ec, d), lambda i, j, e: (e, 0)),
            pl.BlockSpec((d, ec), lambda i, j, e: (0, e)),
            part, part, full, full,
        ],
        out_specs=pl.BlockSpec((gg, tg, d), lambda i, j, e: (i, j, 0)),
        out_shape=jax.ShapeDtypeStruct((b, t, d), F32),
        scratch_shapes=[
            pltpu.VMEM((ec, rows), BF16),
            pltpu.VMEM((d, rows), F32),
        ],
        compiler_params=_cparams(("arbitrary", "arbitrary", "arbitrary")),
        name="peer_dense",
    )(x3, gt, ht, u, vt, c, lr, rk2, dd)


def _final_kernel(x_ref, g_ref, o_ref):
    o_ref[...] = _rmsnorm2(x_ref[...], g_ref[...])


def _final_norm(x3, g, rows):
    b, t, d = x3.shape
    x2 = x3.reshape(b * t, d)
    y = pl.pallas_call(
        _final_kernel,
        grid=(b * t // rows,),
        in_specs=[pl.BlockSpec((rows, d), lambda i: (i, 0)), pl.BlockSpec((1, d), lambda i: (0, 0))],
        out_specs=pl.BlockSpec((rows, d), lambda i: (i, 0)),
        out_shape=jax.ShapeDtypeStruct((b * t, d), F32),
        compiler_params=_cparams(("arbitrary",)),
        name="final_norm",
    )(x2, g)
    return y.reshape(b, t, d)


def _swap_halves(w):
    half = w.shape[-1] // 2
    return jnp.concatenate([w[..., half:], w[..., :half]], axis=-1)


def _prep_layer(p, l):
    d = p['w_in'].shape[1]
    w_in = p['w_in'][l]
    offs = [0]
    for nsz in (Q_LORA, KV_LORA, QK_ROPE, POOL_W, CONV_W, CONV_W, CONV_W, 3 * d):
        offs.append(offs[-1] + nsz)
    w_cq, w_ckv, w_kr, w_u, w_b, w_c, w_h, w_g = [w_in[:, offs[i]:offs[i + 1]] for i in range(8)]
    zeros = lambda n: jnp.zeros((d, n), w_in.dtype)
    w_krg = jnp.concatenate([zeros(QK_NOPE), w_kr, _swap_halves(w_kr)], axis=1)
    w_in_r = jnp.concatenate(
        [w_u, w_b, w_c, w_h, w_cq, w_ckv, w_krg, zeros(COL_G - COL_KR - HEAD_W), w_g], axis=1)
    assert w_in_r.shape[1] == D_IN_PAD

    w_uq = p['w_uq'][l]
    wq_r = jnp.concatenate(
        [w_uq[..., :QK_NOPE], w_uq[..., QK_NOPE:], _swap_halves(w_uq[..., QK_NOPE:])], axis=-1)
    wq_r = wq_r.reshape(Q_LORA, N_HEADS * HEAD_W)
    w_ukv = p['w_ukv'][l]
    wk_r = jnp.concatenate(
        [w_ukv[..., :QK_NOPE], jnp.zeros((KV_LORA, N_HEADS, HEAD_W - QK_NOPE), w_ukv.dtype)],
        axis=-1).reshape(KV_LORA, N_HEADS * HEAD_W)
    wv_r = w_ukv[..., QK_NOPE:].reshape(KV_LORA, N_HEADS * V_HEAD)
    return dict(
        w_in=w_in_r.astype(BF16), wq=wq_r.astype(BF16), wk=wk_r.astype(BF16), wv=wv_r.astype(BF16),
        g_mix=p['g_mix'][l][None, :], g_q=p['g_q'][l][None, :], g_kv=p['g_kv'][l][None, :],
        w_pool=p['w_pool'][l].astype(BF16), pool_scale=p['pool_scale'][l][None, :],
        conv_w=jnp.pad(p['conv_w'][l], ((0, 8 - CONV_K), (0, 0))),
        w_branch=p['w_branch'][l].astype(BF16), w_out=p['w_out'][l].astype(BF16),
        g_ffn=p['g_ffn'][l][None, :],
        peer_wq=p['peer_wq'][l].reshape(d, PEER_HEADS * D_KEY).astype(BF16),
        peer_keys=p['peer_keys'][l].astype(BF16),
        peer_u=p['peer_u'][l].astype(BF16),
        peer_vt=p['peer_v'][l].T.astype(BF16),
    )


def _rope_tables(pos):
    half = QK_ROPE // 2
    inv = ROPE_THETA ** (-jnp.arange(half, dtype=F32) / half)
    ang = pos.astype(F32)[:, None] * inv[None, :]
    cos, sin = jnp.cos(ang), jnp.sin(ang)
    z = lambda n: jnp.zeros((pos.shape[0], n), F32)
    cc = jnp.concatenate([z(QK_NOPE), cos, cos, z(QK_ROPE)], axis=1)
    ss = jnp.concatenate([z(QK_NOPE), -sin, sin, z(QK_ROPE)], axis=1)
    return cc, ss


def _trunk(x, mods, pos, n_hist, cache, hist_pool, hist_conv, layers, g_final, cfg):
    b, t, d = x.shape
    rows = cfg['rows']
    n = b * t
    cc, ss = _rope_tables(pos)
    if t < rows:
        cc = jnp.tile(cc, (rows // t, 1))
        ss = jnp.tile(ss, (rows // t, 1))
    place = jnp.concatenate(
        [jnp.zeros((QK_ROPE, QK_NOPE), F32), jnp.eye(QK_ROPE, dtype=F32),
         jnp.zeros((QK_ROPE, HEAD_W - QK_NOPE - QK_ROPE), F32)], axis=1).astype(BF16)
    new_kv, new_kr, new_pool, new_conv = [], [], [], []
    for l, lw in enumerate(layers):
        mod = mods[l].reshape(b, 1, 6 * d)
        sh1, sc1, gt1, sh2, sc2, gt2 = [mod[:, :, i * d:(i + 1) * d] for i in range(6)]
        z = _inproj(x, lw['g_mix'], sh1, sc1, lw['w_in'], rows)
        q, k, v, ckv, krg = _attn_prep(z, cc, ss, lw['g_q'], lw['wq'], lw['g_kv'], lw['wk'],
                                       lw['wv'], rows, t)
        new_kv.append(ckv.reshape(b, t, KV_LORA))
        new_kr.append(krg[:, QK_NOPE:QK_NOPE + QK_ROPE].reshape(b, t, QK_ROPE))
        if cache is None:
            attn = _flash_prompt(q, k, v, b, t, cfg['attn_tile'])
        else:
            ckv_c, kr_c = cache
            past = ckv_c.shape[2]
            kc, vc = _cache_expand(ckv_c[l].reshape(b * past, KV_LORA),
                                   kr_c[l].reshape(b * past, QK_ROPE),
                                   lw['wk'], lw['wv'], place, cfg['cache_rows'])
            attn = _attn_sample(q, kc, vc, k, v, b, t, past)
        hp = jnp.pad(hist_pool[l], ((0, 0), (HIST_ROWS - POOL_HIST, 0), (0, 0)))
        hc = jnp.pad(hist_conv[l], ((0, 0), (CONV_HIST_ROWS - (CONV_K - 1), 0), (0, 0)))
        x, ptail, ctail = _mix(x, gt1, z, attn, hp, hc, lw['w_pool'], lw['pool_scale'],
                               lw['conv_w'], lw['w_branch'], lw['w_out'], rows, n_hist)
        new_pool.append(ptail[:, HIST_ROWS - POOL_HIST:, :])
        new_conv.append(ctail[:, CONV_HIST_ROWS - (CONV_K - 1):, :])
        ht, s1, s2 = _peer_query(x, lw['g_ffn'], sh2, sc2, lw['peer_wq'], lw['peer_keys'], rows)
        c, lr, rk2, dd = _peer_select(s1, s2, cfg['select_lanes'])
        x = _peer_dense(x, gt2, ht, lw['peer_u'], lw['peer_vt'], c, lr, rk2, dd, rows,
                        cfg['rows_per_chunk'])
    y = _final_norm(x, g_final[None, :], rows)
    return y, jnp.stack(new_kv), jnp.stack(new_kr), jnp.stack(new_pool), jnp.stack(new_conv)


def _config(t_prompt):
    rows = min(256, t_prompt)
    return dict(rows=rows, attn_tile=min(256, t_prompt), cache_rows=256, select_lanes=min(256, rows),
                rows_per_chunk=8)


def kernel(x_prompt, x_sample, cache_kv_latent, cache_k_rope, state_pool, state_conv,
           c_prompt, c_sample, w_ada, b_ada, g_mix, w_in, g_q, w_uq, g_kv, w_ukv,
           w_pool, pool_scale, conv_w, w_branch, w_out, g_ffn, peer_wq, peer_keys,
           peer_u, peer_v, g_final):
    p = {'w_in': w_in, 'g_mix': g_mix, 'g_q': g_q, 'w_uq': w_uq, 'g_kv': g_kv, 'w_ukv': w_ukv,
         'w_pool': w_pool, 'pool_scale': pool_scale, 'conv_w': conv_w, 'w_branch': w_branch,
         'w_out': w_out, 'g_ffn': g_ffn, 'peer_wq': peer_wq, 'peer_keys': peer_keys,
         'peer_u': peer_u, 'peer_v': peer_v}
    depth = w_ada.shape[0]
    bp, tp, d = x_prompt.shape
    bs, ts, _ = x_sample.shape
    past = cache_kv_latent.shape[2]
    layers = [_prep_layer(p, l) for l in range(depth)]

    c_all = jnp.concatenate([c_prompt, c_sample], axis=0)
    pad = (-c_all.shape[0]) % 8
    c_all = jnp.pad(c_all, ((0, pad), (0, 0)))
    mods = _ada(c_all, w_ada, b_ada)
    mods_p, mods_s = mods[:, :bp], mods[:, bp:bp + bs]

    cfg = _config(tp)
    zp = jnp.zeros((depth, bp, POOL_HIST, POOL_W), x_prompt.dtype)
    zc = jnp.zeros((depth, bp, CONV_K - 1, CONV_W), x_prompt.dtype)
    y_p, p_kv, p_kr, p_pool, p_conv = _trunk(
        x_prompt, mods_p, jnp.arange(tp), 0, None, zp, zc, layers, g_final, cfg)
    y_s, s_kv, s_kr, s_pool, s_conv = _trunk(
        x_sample, mods_s, past + jnp.arange(ts), min(past, POOL_HIST),
        (cache_kv_latent, cache_k_rope), state_pool, state_conv, layers, g_final, cfg)
    return (y_p, y_s, p_kv, p_kr, p_pool, p_conv, s_kv, s_kr, s_pool, s_conv)
```

```python
import functools
import math

import jax
import jax.numpy as jnp
from jax import lax
from jax.experimental import pallas as pl
from jax.experimental.pallas import tpu as pltpu

F32 = jnp.float32
BF16 = jnp.bfloat16

EPS = 1e-6
N_HEADS = 8
QK_NOPE = 64
QK_ROPE = 32
V_HEAD = 64
Q_LORA = 512
KV_LORA = 256
ROPE_THETA = 10000.0
CHUNK = 64
ATTN_SCALE = (QK_NOPE + QK_ROPE) ** -0.5
NEG_INF = -1e30
POOL_WINDOWS = (2, 4, 8, 16)
POOL_GROUP = 128
POOL_W = 512
POOL_HIST = 15
CONV_W = 512
CONV_K = 3
PEER_HEADS = 8
N_KEYS = 128
D_KEY = 256
PEER_TOPK = 16

LANES = 128
VMEM_LIMIT_BYTES = 56 * 2**20

HEAD_W = LANES
V_ROWS = V_HEAD + 16
Q_SCALE = ATTN_SCALE * math.log2(math.e)
HIST_ROWS = 16
CONV_HIST_ROWS = 8

COL_U, COL_B, COL_C, COL_H = 0, 512, 1024, 1536
COL_CQ = 2048
COL_CKV = 2560
COL_KR = 2816
COL_G = 3072
D_IN_PAD = 6144


def _cparams(sem):
    return pltpu.CompilerParams(dimension_semantics=sem, vmem_limit_bytes=VMEM_LIMIT_BYTES)


def _dot(a, b):
    return jnp.dot(a, b, preferred_element_type=F32)


def _dot_nt(a, b):
    return lax.dot_general(a, b, (((1,), (1,)), ((), ())), preferred_element_type=F32)


def _ada_kernel(c_ref, w_ref, b_ref, o_ref):
    c = c_ref[...]
    act = c * jax.nn.sigmoid(c)
    o_ref[0] = _dot(act.astype(BF16), w_ref[0].astype(BF16)) + b_ref[0]


def _ada(c_all, w_ada, b_ada):
    depth, d, n6 = w_ada.shape
    bp = c_all.shape[0]
    tn = 1536
    return pl.pallas_call(
        _ada_kernel,
        grid=(depth, n6 // tn),
        in_specs=[
            pl.BlockSpec((bp, d), lambda l, j: (0, 0)),
            pl.BlockSpec((1, d, tn), lambda l, j: (l, 0, j)),
            pl.BlockSpec((1, 1, tn), lambda l, j: (l, 0, j)),
        ],
        out_specs=pl.BlockSpec((1, bp, tn), lambda l, j: (l, 0, j)),
        out_shape=jax.ShapeDtypeStruct((depth, bp, n6), F32),
        compiler_params=_cparams(("arbitrary", "arbitrary")),
        name="ada",
    )(c_all, w_ada, b_ada.reshape(depth, 1, n6))


def _modnorm(x, g, sh, sc):
    ms = jnp.mean(x * x, axis=-1, keepdims=True)
    y = x * lax.rsqrt(ms + EPS) * g
    return y * (1.0 + sc) + sh


def _rmsnorm2(x, g):
    ms = jnp.mean(x * x, axis=-1, keepdims=True)
    return x * lax.rsqrt(ms + EPS) * g


def _row_plan(b, t, rows):
    if t >= rows:
        assert t % rows == 0
        return 1, rows
    assert rows % t == 0 and b % (rows // t) == 0
    return rows // t, t


def _inproj_kernel(x_ref, g_ref, sh_ref, sc_ref, w_ref, o_ref, *, col_chunk):
    gg, tg, d = x_ref.shape
    h = _modnorm(x_ref[...], g_ref[...], sh_ref[...], sc_ref[...])
    hb = h.reshape(gg * tg, d).astype(BF16)
    n = w_ref.shape[1]
    for c in range(0, n, col_chunk):
        o_ref[:, c:c + col_chunk] = _dot(hb, w_ref[:, c:c + col_chunk])


def _inproj(x3, g, sh, sc, w, rows):
    b, t, d = x3.shape
    gg, tg = _row_plan(b, t, rows)
    nj = t // tg
    n = w.shape[1]
    return pl.pallas_call(
        functools.partial(_inproj_kernel, col_chunk=512),
        grid=(b // gg, nj),
        in_specs=[
            pl.BlockSpec((gg, tg, d), lambda i, j: (i, j, 0)),
            pl.BlockSpec((1, d), lambda i, j: (0, 0)),
            pl.BlockSpec((gg, 1, d), lambda i, j: (i, 0, 0)),
            pl.BlockSpec((gg, 1, d), lambda i, j: (i, 0, 0)),
            pl.BlockSpec((d, n), lambda i, j: (0, 0)),
        ],
        out_specs=pl.BlockSpec((gg * tg, n), lambda i, j: (i * nj + j, 0)),
        out_shape=jax.ShapeDtypeStruct((b * t, n), F32),
        compiler_params=_cparams(("arbitrary", "arbitrary")),
        name="inproj",
    )(x3, g, sh, sc, w)


def _rope_group(z, cc, ss):
    return z * cc + pltpu.roll(z, HEAD_W - QK_ROPE, axis=1) * ss


def _attn_prep_kernel(cq_ref, ckv_ref, kr_ref, cc_ref, ss_ref, gq_ref, wq_ref, gkv_ref,
                      wk_ref, wv_ref, q_ref, k_ref, v_ref, kv_ref, kro_ref, *, transposed):
    cck = cc_ref[...]
    ss = ss_ref[...]
    lane = lax.broadcasted_iota(jnp.int32, cck.shape, 1)
    ccq = jnp.where(lane < QK_NOPE, 1.0, cck)
    qn = _rmsnorm2(cq_ref[...], gq_ref[...]).astype(BF16)
    ckv = _rmsnorm2(ckv_ref[...], gkv_ref[...])
    kv_ref[...] = ckv
    ckv_b = ckv.astype(BF16)
    kr = _rope_group(kr_ref[...], cck, ss)
    kro_ref[...] = kr
    v = _dot(ckv_b, wv_ref[...])
    rows = v.shape[0]
    if transposed:
        ones = jnp.ones((V_ROWS - V_HEAD, rows), F32)
        vt = v.T
        for h in range(N_HEADS):
            v_ref[h * V_ROWS:(h + 1) * V_ROWS, :] = jnp.concatenate(
                [vt[h * V_HEAD:(h + 1) * V_HEAD, :], ones], axis=0).astype(BF16)
    else:
        v_ref[...] = v.astype(BF16)
    for h in range(N_HEADS):
        sl = slice(h * HEAD_W, (h + 1) * HEAD_W)
        zq = _dot(qn, wq_ref[:, sl])
        qh = _rope_group(zq, ccq, ss) * Q_SCALE
        if transposed:
            q_ref[sl, :] = qh.T.astype(BF16)
        else:
            q_ref[:, sl] = qh.astype(BF16)
        k_ref[:, sl] = (_dot(ckv_b, wk_ref[:, sl]) + kr).astype(BF16)


def _attn_prep(z, cc, ss, gq, wq, gkv, wk, wv, rows, transposed):
    n = z.shape[0]
    npos = cc.shape[0] // rows
    row = lambda i: (i, 0)
    col = lambda i: (0, i)
    const = lambda i: (0, 0)
    if transposed:
        q_spec = pl.BlockSpec((N_HEADS * HEAD_W, rows), col)
        q_shape = jax.ShapeDtypeStruct((N_HEADS * HEAD_W, n), BF16)
        v_spec = pl.BlockSpec((N_HEADS * V_ROWS, rows), col)
        v_shape = jax.ShapeDtypeStruct((N_HEADS * V_ROWS, n), BF16)
    else:
        q_spec = pl.BlockSpec((rows, N_HEADS * HEAD_W), row)
        q_shape = jax.ShapeDtypeStruct((n, N_HEADS * HEAD_W), BF16)
        v_spec = pl.BlockSpec((rows, N_HEADS * V_HEAD), row)
        v_shape = jax.ShapeDtypeStruct((n, N_HEADS * V_HEAD), BF16)
    return pl.pallas_call(
        functools.partial(_attn_prep_kernel, transposed=transposed),
        grid=(n // rows,),
        in_specs=[
            pl.BlockSpec((rows, Q_LORA), lambda i: (i, COL_CQ // Q_LORA)),
            pl.BlockSpec((rows, KV_LORA), lambda i: (i, COL_CKV // KV_LORA)),
            pl.BlockSpec((rows, HEAD_W), lambda i: (i, COL_KR // HEAD_W)),
            pl.BlockSpec((rows, HEAD_W), lambda i: (i % npos, 0)),
            pl.BlockSpec((rows, HEAD_W), lambda i: (i % npos, 0)),
            pl.BlockSpec((1, Q_LORA), const),
            pl.BlockSpec(wq.shape, const),
            pl.BlockSpec((1, KV_LORA), const),
            pl.BlockSpec(wk.shape, const),
            pl.BlockSpec(wv.shape, const),
        ],
        out_specs=[
            q_spec,
            pl.BlockSpec((rows, N_HEADS * HEAD_W), row),
            v_spec,
            pl.BlockSpec((rows, KV_LORA), row),
            pl.BlockSpec((rows, HEAD_W), row),
        ],
        out_shape=[
            q_shape,
            jax.ShapeDtypeStruct((n, N_HEADS * HEAD_W), BF16),
            v_shape,
            jax.ShapeDtypeStruct((n, KV_LORA), F32),
            jax.ShapeDtypeStruct((n, HEAD_W), F32),
        ],
        compiler_params=_cparams(("arbitrary",)),
        name="attn_prep",
    )(z, z, z, cc, ss, gq, wq, gkv, wk, wv)


def _cache_expand_kernel(ckv_ref, kr_ref, wk_ref, wv_ref, place_ref, k_ref, v_ref):
    ckv_b = ckv_ref[...].astype(BF16)
    krp = _dot(kr_ref[...].astype(BF16), place_ref[...])
    v_ref[...] = _dot(ckv_b, wv_ref[...]).astype(BF16)
    for h in range(N_HEADS):
        sl = slice(h * HEAD_W, (h + 1) * HEAD_W)
        k_ref[:, sl] = (_dot(ckv_b, wk_ref[:, sl]) + krp).astype(BF16)


def _cache_expand(ckv, kr, wk, wv, place, rows):
    n = ckv.shape[0]
    row = lambda i: (i, 0)
    const = lambda i: (0, 0)
    return pl.pallas_call(
        _cache_expand_kernel,
        grid=(n // rows,),
        in_specs=[
            pl.BlockSpec((rows, KV_LORA), row),
            pl.BlockSpec((rows, QK_ROPE), row),
            pl.BlockSpec(wk.shape, const),
            pl.BlockSpec(wv.shape, const),
            pl.BlockSpec(place.shape, const),
        ],
        out_specs=[
            pl.BlockSpec((rows, N_HEADS * HEAD_W), row),
            pl.BlockSpec((rows, N_HEADS * V_HEAD), row),
        ],
        out_shape=[
            jax.ShapeDtypeStruct((n, N_HEADS * HEAD_W), BF16),
            jax.ShapeDtypeStruct((n, N_HEADS * V_HEAD), BF16),
        ],
        compiler_params=_cparams(("arbitrary",)),
        name="cache_expand",
    )(ckv, kr, wk, wv, place)


def _flash_step(qt_ref, k_ref, vt_ref, m_ref, acc_ref, masked, q0, k0):
    tk, tq = k_ref.shape[0], qt_ref.shape[1]
    if masked:
        kc = (k0 + lax.broadcasted_iota(jnp.int32, (tk, tq), 0)) // CHUNK
        qc = (q0 + lax.broadcasted_iota(jnp.int32, (tk, tq), 1)) // CHUNK
        keep = kc <= qc
    def scores(h):
        sl = slice(h * HEAD_W, (h + 1) * HEAD_W)
        s = _dot(k_ref[:, sl], qt_ref[sl, :])
        return jnp.where(keep, s, NEG_INF) if masked else s

    def stats(h, s):
        m_prev = m_ref[h]
        m_new = jnp.maximum(m_prev, jnp.max(s, axis=0, keepdims=True))
        m_ref[h] = m_new
        return m_new, jnp.exp2(m_prev - m_new)

    s = {0: scores(0)}
    if N_HEADS > 1:
        s[1] = scores(1)
    st = {0: stats(0, s[0])}
    for h in range(N_HEADS):
        vs = slice(h * V_ROWS, (h + 1) * V_ROWS)
        if h + 2 < N_HEADS:
            s[h + 2] = scores(h + 2)
        if h + 1 < N_HEADS:
            st[h + 1] = stats(h + 1, s[h + 1])
        m_new, alpha = st.pop(h)
        p = jnp.exp2(s.pop(h) - m_new).astype(BF16)
        acc_ref[h] = alpha * acc_ref[h] + _dot(vt_ref[vs, :], p)


def _flash_kernel(qt_ref, k_ref, vt_ref, o_ref, m_ref, acc_ref, *, ratio):
    qi = pl.program_id(1)
    ki = pl.program_id(2)
    tk, tq = k_ref.shape[0], qt_ref.shape[1]

    @pl.when(ki == 0)
    def _():
        m_ref[...] = jnp.full(m_ref.shape, NEG_INF, F32)
        acc_ref[...] = jnp.zeros(acc_ref.shape, F32)

    @pl.when(ki < qi * ratio)
    def _():
        _flash_step(qt_ref, k_ref, vt_ref, m_ref, acc_ref, False, 0, 0)

    @pl.when(jnp.logical_and(ki >= qi * ratio, ki < (qi + 1) * ratio))
    def _():
        _flash_step(qt_ref, k_ref, vt_ref, m_ref, acc_ref, True, qi * tq, ki * tk)

    @pl.when(ki == (qi + 1) * ratio - 1)
    def _():
        for h in range(N_HEADS):
            a = acc_ref[h]
            o = a[:V_HEAD, :] / a[V_HEAD:V_HEAD + 1, :]
            o_ref[:, h * V_HEAD:(h + 1) * V_HEAD] = o.T.astype(o_ref.dtype)


def _flash_prompt(qt, k, vt, b, t, tq, tk):
    nq, nk = t // tq, t // tk
    ratio = tq // tk
    last = lambda qi, ki: jnp.minimum(ki, (qi + 1) * ratio - 1)
    return pl.pallas_call(
        functools.partial(_flash_kernel, ratio=ratio),
        grid=(b, nq, nk),
        in_specs=[
            pl.BlockSpec((N_HEADS * HEAD_W, tq), lambda bi, qi, ki: (0, bi * nq + qi)),
            pl.BlockSpec((tk, N_HEADS * HEAD_W), lambda bi, qi, ki: (bi * nk + last(qi, ki), 0)),
            pl.BlockSpec((N_HEADS * V_ROWS, tk), lambda bi, qi, ki: (0, bi * nk + last(qi, ki))),
        ],
        out_specs=pl.BlockSpec((tq, N_HEADS * V_HEAD), lambda bi, qi, ki: (bi * nq + qi, 0)),
        out_shape=jax.ShapeDtypeStruct((b * t, N_HEADS * V_HEAD), BF16),
        scratch_shapes=[
            pltpu.VMEM((N_HEADS, 1, tq), F32),
            pltpu.VMEM((N_HEADS, V_ROWS, tq), F32),
        ],
        compiler_params=_cparams(("arbitrary", "arbitrary", "arbitrary")),
        name="flash_prompt",
    )(qt, k, vt)


def _attn_sample_kernel(q_ref, kc_ref, vc_ref, kn_ref, vn_ref, o_ref):
    for h in range(N_HEADS):
        sl = slice(h * HEAD_W, (h + 1) * HEAD_W)
        vs = slice(h * V_HEAD, (h + 1) * V_HEAD)
        qh = q_ref[:, sl]
        sc = _dot_nt(qh, kc_ref[:, sl])
        sn = _dot_nt(qh, kn_ref[:, sl])
        m = jnp.maximum(jnp.max(sc, axis=1, keepdims=True), jnp.max(sn, axis=1, keepdims=True))
        pc = jnp.exp2(sc - m)
        pn = jnp.exp2(sn - m)
        den = jnp.sum(pc, axis=1, keepdims=True) + jnp.sum(pn, axis=1, keepdims=True)
        o = _dot(pc.astype(BF16), vc_ref[:, vs]) + _dot(pn.astype(BF16), vn_ref[:, vs])
        o_ref[:, vs] = (o / den).astype(o_ref.dtype)


def _attn_sample(q, kc, vc, kn, vn, b, t, past):
    return pl.pallas_call(
        _attn_sample_kernel,
        grid=(b,),
        in_specs=[
            pl.BlockSpec((t, N_HEADS * HEAD_W), lambda i: (i, 0)),
            pl.BlockSpec((past, N_HEADS * HEAD_W), lambda i: (i, 0)),
            pl.BlockSpec((past, N_HEADS * V_HEAD), lambda i: (i, 0)),
            pl.BlockSpec((t, N_HEADS * HEAD_W), lambda i: (i, 0)),
            pl.BlockSpec((t, N_HEADS * V_HEAD), lambda i: (i, 0)),
        ],
        out_specs=pl.BlockSpec((t, N_HEADS * V_HEAD), lambda i: (i, 0)),
        out_shape=jax.ShapeDtypeStruct((b * t, N_HEADS * V_HEAD), BF16),
        compiler_params=_cparams(("arbitrary",)),
        name="attn_sample",
    )(q, kc, vc, kn, vn)


def _mix_kernel(x_ref, gt_ref, ubch_ref, g_ref, attn_ref, pu_ref, pch_ref, hp_ref, hc_ref,
                wpool_ref, pscale_ref, convw_ref, wbr_ref, wout_ref,
                xo_ref, ptail_ref, ctail_ref, extp_ref, extc_ref, *, n_hist):
    gg, tg, d = x_ref.shape
    j = pl.program_id(1)
    first = j == 0

    ubch = ubch_ref[...]
    u = ubch[:, COL_U:COL_U + POOL_W].reshape(gg, tg, POOL_W)
    bgate = ubch[:, COL_B:COL_B + CONV_W].reshape(gg, tg, CONV_W)
    cu = (ubch[:, COL_C:COL_C + CONV_W] * ubch[:, COL_H:COL_H + CONV_W]).reshape(gg, tg, CONV_W)

    hist_p = jnp.where(first, hp_ref[...], pu_ref[...].reshape(1, HIST_ROWS, POOL_W))
    pch = pch_ref[...]
    prev_cu = (pch[:, :CONV_W] * pch[:, CONV_W:])[HIST_ROWS - CONV_HIST_ROWS:]
    hist_c = jnp.where(first, hc_ref[...], prev_cu.reshape(1, CONV_HIST_ROWS, CONV_W))

    extp_ref[:, :HIST_ROWS, :] = hist_p
    extp_ref[:, HIST_ROWS:, :] = u
    extc_ref[:, :CONV_HIST_ROWS, :] = hist_c
    extc_ref[:, CONV_HIST_ROWS:, :] = cu
    ptail_ref[...] = u[:, tg - HIST_ROWS:, :]
    ctail_ref[...] = cu[:, tg - CONV_HIST_ROWS:, :]

    tpos = j * tg + lax.broadcasted_iota(jnp.int32, (1, tg, 1), 1)
    pooled = []
    for gi, w in enumerate(POOL_WINDOWS):
        cs = slice(gi * POOL_GROUP, (gi + 1) * POOL_GROUP)
        acc = u[:, :, cs]
        for kk in range(1, w):
            acc = acc + extp_ref[:, HIST_ROWS - kk:HIST_ROWS - kk + tg, cs]
        cnt = jnp.minimum(tpos + 1 + n_hist, w).astype(F32)
        dd = acc / cnt - u[:, :, cs]
        pooled.append(_dot(dd.reshape(gg * tg, POOL_GROUP).astype(BF16), wpool_ref[gi]))
    pool = jnp.concatenate(pooled, axis=1) * pscale_ref[...]

    cw = convw_ref[...]
    yc = (cw[0:1, :] * extc_ref[:, CONV_HIST_ROWS - 2:CONV_HIST_ROWS - 2 + tg, :]
          + cw[1:2, :] * extc_ref[:, CONV_HIST_ROWS - 1:CONV_HIST_ROWS - 1 + tg, :]
          + cw[2:3, :] * cu)
    conv = (bgate * yc).reshape(gg * tg, CONV_W)

    gates = jax.nn.sigmoid(g_ref[...])
    mixed = (gates[:, 0:d] * _dot(attn_ref[...], wbr_ref[0])
             + gates[:, d:2 * d] * _dot(pool.astype(BF16), wbr_ref[1])
             + gates[:, 2 * d:3 * d] * _dot(conv.astype(BF16), wbr_ref[2]))
    out = _dot(mixed.astype(BF16), wout_ref[...])
    xo_ref[...] = x_ref[...] + gt_ref[...] * out.reshape(gg, tg, d)


def _mix(x3, gt, z, attn, hist_p, hist_c, wpool, pscale, convw, wbr, wout, rows, n_hist):
    b, t, d = x3.shape
    gg, tg = _row_plan(b, t, rows)
    nj = t // tg
    rb = lambda i, j: i * nj + j
    hpb = tg // HIST_ROWS

    def prev_rows(i, j):
        return jnp.maximum(rb(i, j) * hpb - 1, 0)

    const2 = lambda i, j: (0, 0)
    const3 = lambda i, j: (0, 0, 0)
    return pl.pallas_call(
        functools.partial(_mix_kernel, n_hist=n_hist),
        grid=(b // gg, nj),
        in_specs=[
            pl.BlockSpec((gg, tg, d), lambda i, j: (i, j, 0)),
            pl.BlockSpec((gg, 1, d), lambda i, j: (i, 0, 0)),
            pl.BlockSpec((gg * tg, COL_CQ), lambda i, j: (rb(i, j), 0)),
            pl.BlockSpec((gg * tg, 3 * d), lambda i, j: (rb(i, j), COL_G // (3 * d))),
            pl.BlockSpec((gg * tg, N_HEADS * V_HEAD), lambda i, j: (rb(i, j), 0)),
            pl.BlockSpec((HIST_ROWS, POOL_W), lambda i, j: (prev_rows(i, j), 0)),
            pl.BlockSpec((HIST_ROWS, 2 * CONV_W), lambda i, j: (prev_rows(i, j), COL_C // (2 * CONV_W))),
            pl.BlockSpec((gg, HIST_ROWS, POOL_W), lambda i, j: (i, 0, 0)),
            pl.BlockSpec((gg, CONV_HIST_ROWS, CONV_W), lambda i, j: (i, 0, 0)),
            pl.BlockSpec(wpool.shape, const3),
            pl.BlockSpec(pscale.shape, const2),
            pl.BlockSpec(convw.shape, const2),
            pl.BlockSpec(wbr.shape, const3),
            pl.BlockSpec(wout.shape, const2),
        ],
        out_specs=[
            pl.BlockSpec((gg, tg, d), lambda i, j: (i, j, 0)),
            pl.BlockSpec((gg, HIST_ROWS, POOL_W), lambda i, j: (i, 0, 0)),
            pl.BlockSpec((gg, CONV_HIST_ROWS, CONV_W), lambda i, j: (i, 0, 0)),
        ],
        out_shape=[
            jax.ShapeDtypeStruct((b, t, d), F32),
            jax.ShapeDtypeStruct((b, HIST_ROWS, POOL_W), F32),
            jax.ShapeDtypeStruct((b, CONV_HIST_ROWS, CONV_W), F32),
        ],
        scratch_shapes=[
            pltpu.VMEM((gg, HIST_ROWS + tg, POOL_W), F32),
            pltpu.VMEM((gg, CONV_HIST_ROWS + tg, CONV_W), F32),
        ],
        compiler_params=_cparams(("arbitrary", "arbitrary")),
        name="mix_merge",
    )(x3, gt, z, z, attn, z, z, hist_p, hist_c, wpool, pscale, convw, wbr, wout)


def _peer_query_kernel(x_ref, g_ref, sh_ref, sc_ref, wq_ref, keys_ref, ht_ref, s1_ref, s2_ref):
    gg, tg, d = x_ref.shape
    h = _modnorm(x_ref[...], g_ref[...], sh_ref[...], sc_ref[...]).reshape(gg * tg, d)
    ht_ref[...] = h.T.astype(BF16)
    hb = h.astype(BF16)
    half = D_KEY // 2
    k1 = keys_ref[0]
    k2 = keys_ref[1]
    for hh in range(PEER_HEADS):
        q = _dot(hb, wq_ref[:, hh * D_KEY:(hh + 1) * D_KEY]).astype(BF16)
        s1_ref[hh] = _dot_nt(k1, q[:, :half])
        s2_ref[hh] = _dot_nt(k2, q[:, half:])


def _peer_query(x3, g, sh, sc, wq, keys, rows):
    b, t, d = x3.shape
    gg, tg = _row_plan(b, t, rows)
    nj = t // tg
    n = b * t
    tok = lambda i, j: (0, 0, i * nj + j)
    return pl.pallas_call(
        _peer_query_kernel,
        grid=(b // gg, nj),
        in_specs=[
            pl.BlockSpec((gg, tg, d), lambda i, j: (i, j, 0)),
            pl.BlockSpec((1, d), lambda i, j: (0, 0)),
            pl.BlockSpec((gg, 1, d), lambda i, j: (i, 0, 0)),
            pl.BlockSpec((gg, 1, d), lambda i, j: (i, 0, 0)),
            pl.BlockSpec(wq.shape, lambda i, j: (0, 0)),
            pl.BlockSpec(keys.shape, lambda i, j: (0, 0, 0)),
        ],
        out_specs=[
            pl.BlockSpec((d, gg * tg), lambda i, j: (0, i * nj + j)),
            pl.BlockSpec((PEER_HEADS, N_KEYS, gg * tg), tok),
            pl.BlockSpec((PEER_HEADS, N_KEYS, gg * tg), tok),
        ],
        out_shape=[
            jax.ShapeDtypeStruct((d, n), BF16),
            jax.ShapeDtypeStruct((PEER_HEADS, N_KEYS, n), F32),
            jax.ShapeDtypeStruct((PEER_HEADS, N_KEYS, n), F32),
        ],
        compiler_params=_cparams(("arbitrary", "arbitrary")),
        name="peer_query",
    )(x3, g, sh, sc, wq, keys)


def _top_ranks(s):
    nk, r = s.shape
    iota = lax.broadcasted_iota(jnp.int32, (nk, r), 0)
    iota_k = lax.broadcasted_iota(jnp.int32, (PEER_TOPK, r), 0)
    rank = jnp.full((nk, r), 99.0, F32)
    vals = jnp.zeros((PEER_TOPK, r), F32)
    x = s
    for it in range(PEER_TOPK):
        m = jnp.max(x, axis=0, keepdims=True)
        idx = jnp.min(jnp.where(x == m, iota, nk), axis=0, keepdims=True)
        hit = iota == idx
        rank = jnp.where(hit, float(it), rank)
        x = jnp.where(hit, -jnp.inf, x)
        vals = jnp.where(iota_k == it, m, vals)
    return rank, vals


def _peer_select_kernel(s1_ref, s2_ref, c_ref, lr_ref, rk2_ref, d_ref):
    s1 = s1_ref[0]
    s2 = s2_ref[0]
    r = s1.shape[1]
    rank1, v1 = _top_ranks(s1)
    rank2, v2 = _top_ranks(s2)
    iota_k = lax.broadcasted_iota(jnp.int32, (PEER_TOPK, r), 0)
    cnt = jnp.zeros((PEER_TOPK, r), jnp.int32)
    front = v1 + v2[0:1, :]
    top = front[0:1, :]
    zsum = jnp.zeros((1, r), F32)
    for _ in range(PEER_TOPK):
        m = jnp.max(front, axis=0, keepdims=True)
        a = jnp.min(jnp.where(front == m, iota_k, PEER_TOPK), axis=0, keepdims=True)
        hit = iota_k == a
        zsum = zsum + jnp.exp(m - top)
        cnt = jnp.where(hit, cnt + 1, cnt)
        nxt = jnp.full((PEER_TOPK, r), -jnp.inf, F32)
        for bcol in range(1, PEER_TOPK):
            nxt = jnp.where(cnt == bcol, v2[bcol:bcol + 1, :], nxt)
        front = jnp.where(hit, v1 + nxt, front)
    cntf = cnt.astype(F32)
    lr = jnp.zeros(rank1.shape, F32)
    for a in range(PEER_TOPK):
        lr = jnp.where(rank1 == float(a), cntf[a:a + 1, :], lr)
    c_ref[0] = jnp.exp(s1 - v1[0:1, :]) / zsum
    lr_ref[0] = lr
    rk2_ref[0] = rank2
    d_ref[0] = jnp.exp(s2 - v2[0:1, :])


def _peer_select(s1, s2, lanes):
    hh, nk, n = s1.shape
    spec = pl.BlockSpec((1, nk, lanes), lambda i, h: (h, 0, i))
    shp = jax.ShapeDtypeStruct((hh, nk, n), F32)
    return pl.pallas_call(
        _peer_select_kernel,
        grid=(n // lanes, hh),
        in_specs=[spec, spec],
        out_specs=[spec, spec, spec, spec],
        out_shape=[shp, shp, shp, shp],
        compiler_params=_cparams(("arbitrary", "arbitrary")),
        name="peer_select",
    )(s1, s2)


def _gelu(x):
    return 0.5 * x * (1.0 + lax.erf(x * (2.0 ** -0.5)))


def _peer_dense_kernel(x_ref, gt_ref, ht_ref, u_ref, vt_ref, c_ref, lr_ref, rk2_ref, d_ref,
                       xo_ref, w_ref, acc_ref, *, rows_per_chunk):
    gg, tg, d = x_ref.shape
    e = pl.program_id(2)

    @pl.when(e == 0)
    def _():
        acc_ref[...] = jnp.zeros(acc_ref.shape, F32)

    a_t = _dot(u_ref[...], ht_ref[...])
    act = _gelu(a_t)
    for ii in range(rows_per_chunk):
        gate = None
        for hh in range(PEER_HEADS):
            term = c_ref[hh, ii:ii + 1, :] * jnp.where(
                rk2_ref[hh] < lr_ref[hh, ii:ii + 1, :], d_ref[hh], 0.0)
            gate = term if gate is None else gate + term
        sl = slice(ii * N_KEYS, (ii + 1) * N_KEYS)
        w_ref[sl, :] = (act[sl, :] * gate).astype(BF16)
    acc_ref[...] += _dot(vt_ref[...], w_ref[...])

    @pl.when(e == pl.num_programs(2) - 1)
    def _():
        xo_ref[...] = x_ref[...] + gt_ref[...] * acc_ref[...].T.reshape(gg, tg, d)


def _peer_dense(x3, gt, ht, u, vt, c, lr, rk2, dd, rows, rows_per_chunk):
    b, t, d = x3.shape
    gg, tg = _row_plan(b, t, rows)
    nj = t // tg
    ne = u.shape[0]
    ec = rows_per_chunk * N_KEYS
    tokb = lambda i, j, e: i * nj + j
    full = pl.BlockSpec((PEER_HEADS, N_KEYS, rows), lambda i, j, e: (0, 0, tokb(i, j, e)))
    part = pl.BlockSpec((PEER_HEADS, rows_per_chunk, rows), lambda i, j, e: (0, e, tokb(i, j, e)))
    return pl.pallas_call(
        functools.partial(_peer_dense_kernel, rows_per_chunk=rows_per_chunk),
        grid=(b // gg, nj, ne // ec),
        in_specs=[
            pl.BlockSpec((gg, tg, d), lambda i, j, e: (i, j, 0)),
            pl.BlockSpec((gg, 1, d), lambda i, j, e: (i, 0, 0)),
            pl.BlockSpec((d, rows), lambda i, j, e: (0, tokb(i, j, e))),
            pl.BlockSpec((ec, d), lambda i, j, e: (e, 0)),
            pl.BlockSpec((d, ec), lambda i, j, e: (0, e)),
            part, part, full, full,
        ],
        out_specs=pl.BlockSpec((gg, tg, d), lambda i, j, e: (i, j, 0)),
        out_shape=jax.ShapeDtypeStruct((b, t, d), F32),
        scratch_shapes=[
            pltpu.VMEM((ec, rows), BF16),
            pltpu.VMEM((d, rows), F32),
        ],
        compiler_params=_cparams(("arbitrary", "arbitrary", "arbitrary")),
        name="peer_dense",
    )(x3, gt, ht, u, vt, c, lr, rk2, dd)


def _final_kernel(x_ref, g_ref, o_ref):
    o_ref[...] = _rmsnorm2(x_ref[...], g_ref[...])


def _final_norm(x3, g, rows):
    b, t, d = x3.shape
    x2 = x3.reshape(b * t, d)
    y = pl.pallas_call(
        _final_kernel,
        grid=(b * t // rows,),
        in_specs=[pl.BlockSpec((rows, d), lambda i: (i, 0)), pl.BlockSpec((1, d), lambda i: (0, 0))],
        out_specs=pl.BlockSpec((rows, d), lambda i: (i, 0)),
        out_shape=jax.ShapeDtypeStruct((b * t, d), F32),
        compiler_params=_cparams(("arbitrary",)),
        name="final_norm",
    )(x2, g)
    return y.reshape(b, t, d)


def _swap_halves(w):
    half = w.shape[-1] // 2
    return jnp.concatenate([w[..., half:], w[..., :half]], axis=-1)


def _prep_layer(p, l):
    d = p['w_in'].shape[1]
    w_in = p['w_in'][l]
    offs = [0]
    for nsz in (Q_LORA, KV_LORA, QK_ROPE, POOL_W, CONV_W, CONV_W, CONV_W, 3 * d):
        offs.append(offs[-1] + nsz)
    w_cq, w_ckv, w_kr, w_u, w_b, w_c, w_h, w_g = [w_in[:, offs[i]:offs[i + 1]] for i in range(8)]
    zeros = lambda n: jnp.zeros((d, n), w_in.dtype)
    w_krg = jnp.concatenate([zeros(QK_NOPE), w_kr, _swap_halves(w_kr)], axis=1)
    w_in_r = jnp.concatenate(
        [w_u, w_b, w_c, w_h, w_cq, w_ckv, w_krg, zeros(COL_G - COL_KR - HEAD_W), w_g], axis=1)
    assert w_in_r.shape[1] == D_IN_PAD

    w_uq = p['w_uq'][l]
    wq_r = jnp.concatenate(
        [w_uq[..., :QK_NOPE], w_uq[..., QK_NOPE:], _swap_halves(w_uq[..., QK_NOPE:])], axis=-1)
    wq_r = wq_r.reshape(Q_LORA, N_HEADS * HEAD_W)
    w_ukv = p['w_ukv'][l]
    wk_r = jnp.concatenate(
        [w_ukv[..., :QK_NOPE], jnp.zeros((KV_LORA, N_HEADS, HEAD_W - QK_NOPE), w_ukv.dtype)],
        axis=-1).reshape(KV_LORA, N_HEADS * HEAD_W)
    wv_r = w_ukv[..., QK_NOPE:].reshape(KV_LORA, N_HEADS * V_HEAD)
    return dict(
        w_in=w_in_r.astype(BF16), wq=wq_r.astype(BF16), wk=wk_r.astype(BF16), wv=wv_r.astype(BF16),
        g_mix=p['g_mix'][l][None, :], g_q=p['g_q'][l][None, :], g_kv=p['g_kv'][l][None, :],
        w_pool=p['w_pool'][l].astype(BF16), pool_scale=p['pool_scale'][l][None, :],
        conv_w=jnp.pad(p['conv_w'][l], ((0, 8 - CONV_K), (0, 0))),
        w_branch=p['w_branch'][l].astype(BF16), w_out=p['w_out'][l].astype(BF16),
        g_ffn=p['g_ffn'][l][None, :],
        peer_wq=p['peer_wq'][l].reshape(d, PEER_HEADS * D_KEY).astype(BF16),
        peer_keys=p['peer_keys'][l].astype(BF16),
        peer_u=p['peer_u'][l].astype(BF16),
        peer_vt=p['peer_v'][l].T.astype(BF16),
    )


def _rope_tables(pos):
    half = QK_ROPE // 2
    inv = ROPE_THETA ** (-jnp.arange(half, dtype=F32) / half)
    ang = pos.astype(F32)[:, None] * inv[None, :]
    cos, sin = jnp.cos(ang), jnp.sin(ang)
    z = lambda n: jnp.zeros((pos.shape[0], n), F32)
    cc = jnp.concatenate([z(QK_NOPE), cos, cos, z(QK_ROPE)], axis=1)
    ss = jnp.concatenate([z(QK_NOPE), -sin, sin, z(QK_ROPE)], axis=1)
    return cc, ss


def _trunk(x, mods, pos, n_hist, cache, hist_pool, hist_conv, layers, g_final, cfg):
    b, t, d = x.shape
    rows = cfg['rows']
    n = b * t
    cc, ss = _rope_tables(pos)
    if t < rows:
        cc = jnp.tile(cc, (rows // t, 1))
        ss = jnp.tile(ss, (rows // t, 1))
    place = jnp.concatenate(
        [jnp.zeros((QK_ROPE, QK_NOPE), F32), jnp.eye(QK_ROPE, dtype=F32),
         jnp.zeros((QK_ROPE, HEAD_W - QK_NOPE - QK_ROPE), F32)], axis=1).astype(BF16)
    new_kv, new_kr, new_pool, new_conv = [], [], [], []
    for l, lw in enumerate(layers):
        mod = mods[l].reshape(b, 1, 6 * d)
        sh1, sc1, gt1, sh2, sc2, gt2 = [mod[:, :, i * d:(i + 1) * d] for i in range(6)]
        z = _inproj(x, lw['g_mix'], sh1, sc1, lw['w_in'], rows)
        q, k, v, ckv, krg = _attn_prep(z, cc, ss, lw['g_q'], lw['wq'], lw['g_kv'], lw['wk'],
                                       lw['wv'], rows, cache is None)
        new_kv.append(ckv.reshape(b, t, KV_LORA))
        new_kr.append(krg[:, QK_NOPE:QK_NOPE + QK_ROPE].reshape(b, t, QK_ROPE))
        if cache is None:
            attn = _flash_prompt(q, k, v, b, t, cfg['attn_tq'], cfg['attn_tk'])
        else:
            ckv_c, kr_c = cache
            past = ckv_c.shape[2]
            kc, vc = _cache_expand(ckv_c[l].reshape(b * past, KV_LORA),
                                   kr_c[l].reshape(b * past, QK_ROPE),
                                   lw['wk'], lw['wv'], place, cfg['cache_rows'])
            attn = _attn_sample(q, kc, vc, k, v, b, t, past)
        hp = jnp.pad(hist_pool[l], ((0, 0), (HIST_ROWS - POOL_HIST, 0), (0, 0)))
        hc = jnp.pad(hist_conv[l], ((0, 0), (CONV_HIST_ROWS - (CONV_K - 1), 0), (0, 0)))
        x, ptail, ctail = _mix(x, gt1, z, attn, hp, hc, lw['w_pool'], lw['pool_scale'],
                               lw['conv_w'], lw['w_branch'], lw['w_out'], rows, n_hist)
        new_pool.append(ptail[:, HIST_ROWS - POOL_HIST:, :])
        new_conv.append(ctail[:, CONV_HIST_ROWS - (CONV_K - 1):, :])
        ht, s1, s2 = _peer_query(x, lw['g_ffn'], sh2, sc2, lw['peer_wq'], lw['peer_keys'], rows)
        c, lr, rk2, dd = _peer_select(s1, s2, cfg['select_lanes'])
        x = _peer_dense(x, gt2, ht, lw['peer_u'], lw['peer_vt'], c, lr, rk2, dd, rows,
                        cfg['rows_per_chunk'])
    y = _final_norm(x, g_final[None, :], rows)
    return y, jnp.stack(new_kv), jnp.stack(new_kr), jnp.stack(new_pool), jnp.stack(new_conv)


def _config(t_prompt):
    rows = min(256, t_prompt)
    return dict(rows=rows, attn_tq=min(512, t_prompt), attn_tk=min(256, t_prompt), cache_rows=256, select_lanes=min(256, rows),
                rows_per_chunk=8)


def kernel(x_prompt, x_sample, cache_kv_latent, cache_k_rope, state_pool, state_conv,
           c_prompt, c_sample, w_ada, b_ada, g_mix, w_in, g_q, w_uq, g_kv, w_ukv,
           w_pool, pool_scale, conv_w, w_branch, w_out, g_ffn, peer_wq, peer_keys,
           peer_u, peer_v, g_final):
    p = {'w_in': w_in, 'g_mix': g_mix, 'g_q': g_q, 'w_uq': w_uq, 'g_kv': g_kv, 'w_ukv': w_ukv,
         'w_pool': w_pool, 'pool_scale': pool_scale, 'conv_w': conv_w, 'w_branch': w_branch,
         'w_out': w_out, 'g_ffn': g_ffn, 'peer_wq': peer_wq, 'peer_keys': peer_keys,
         'peer_u': peer_u, 'peer_v': peer_v}
    depth = w_ada.shape[0]
    bp, tp, d = x_prompt.shape
    bs, ts, _ = x_sample.shape
    past = cache_kv_latent.shape[2]
    layers = [_prep_layer(p, l) for l in range(depth)]

    c_all = jnp.concatenate([c_prompt, c_sample], axis=0)
    pad = (-c_all.shape[0]) % 8
    c_all = jnp.pad(c_all, ((0, pad), (0, 0)))
    mods = _ada(c_all, w_ada, b_ada)
    mods_p, mods_s = mods[:, :bp], mods[:, bp:bp + bs]

    cfg = _config(tp)
    zp = jnp.zeros((depth, bp, POOL_HIST, POOL_W), x_prompt.dtype)
    zc = jnp.zeros((depth, bp, CONV_K - 1, CONV_W), x_prompt.dtype)
    y_p, p_kv, p_kr, p_pool, p_conv = _trunk(
        x_prompt, mods_p, jnp.arange(tp), 0, None, zp, zc, layers, g_final, cfg)
    y_s, s_kv, s_kr, s_pool, s_conv = _trunk(
        x_sample, mods_s, past + jnp.arange(ts), min(past, POOL_HIST),
        (cache_kv_latent, cache_k_rope), state_pool, state_conv, layers, g_final, cfg)
    return (y_p, y_s, p_kv, p_kr, p_pool, p_conv, s_kv, s_kr, s_pool, s_conv)
```

```python
import functools
import math

import jax
import jax.numpy as jnp
from jax import lax
from jax.experimental import pallas as pl
from jax.experimental.pallas import tpu as pltpu

F32 = jnp.float32
BF16 = jnp.bfloat16

EPS = 1e-6
N_HEADS = 8
QK_NOPE = 64
QK_ROPE = 32
V_HEAD = 64
Q_LORA = 512
KV_LORA = 256
ROPE_THETA = 10000.0
CHUNK = 64
ATTN_SCALE = (QK_NOPE + QK_ROPE) ** -0.5
NEG_INF = -1e30
POOL_WINDOWS = (2, 4, 8, 16)
POOL_GROUP = 128
POOL_W = 512
POOL_HIST = 15
CONV_W = 512
CONV_K = 3
PEER_HEADS = 8
N_KEYS = 128
D_KEY = 256
PEER_TOPK = 16
NOT_TOP = 99.0

LANES = 128
VMEM_LIMIT_BYTES = 56 * 2**20

HEAD_W = LANES
V_ROWS = V_HEAD + 16
Q_SCALE = ATTN_SCALE * math.log2(math.e)
HIST_ROWS = 16
CONV_HIST_ROWS = 8

COL_U, COL_B, COL_C, COL_H = 0, 512, 1024, 1536
COL_CQ = 2048
COL_CKV = 2560
COL_KR = 2816
COL_G = 3072
D_IN_PAD = 6144


def _cparams(sem):
    return pltpu.CompilerParams(dimension_semantics=sem, vmem_limit_bytes=VMEM_LIMIT_BYTES)


def _dot(a, b):
    return jnp.dot(a, b, preferred_element_type=F32)


def _dot_nt(a, b):
    return lax.dot_general(a, b, (((1,), (1,)), ((), ())), preferred_element_type=F32)


def _ada_kernel(c_ref, w_ref, b_ref, o_ref):
    c = c_ref[...]
    act = c * jax.nn.sigmoid(c)
    o_ref[0] = _dot(act.astype(BF16), w_ref[0].astype(BF16)) + b_ref[0]


def _ada(c_all, w_ada, b_ada):
    depth, d, n6 = w_ada.shape
    bp = c_all.shape[0]
    tn = 1536
    return pl.pallas_call(
        _ada_kernel,
        grid=(depth, n6 // tn),
        in_specs=[
            pl.BlockSpec((bp, d), lambda l, j: (0, 0)),
            pl.BlockSpec((1, d, tn), lambda l, j: (l, 0, j)),
            pl.BlockSpec((1, 1, tn), lambda l, j: (l, 0, j)),
        ],
        out_specs=pl.BlockSpec((1, bp, tn), lambda l, j: (l, 0, j)),
        out_shape=jax.ShapeDtypeStruct((depth, bp, n6), F32),
        compiler_params=_cparams(("arbitrary", "arbitrary")),
        name="ada",
    )(c_all, w_ada, b_ada.reshape(depth, 1, n6))


def _modnorm(x, g, sh, sc):
    ms = jnp.mean(x * x, axis=-1, keepdims=True)
    y = x * lax.rsqrt(ms + EPS) * g
    return y * (1.0 + sc) + sh


def _rmsnorm2(x, g):
    ms = jnp.mean(x * x, axis=-1, keepdims=True)
    return x * lax.rsqrt(ms + EPS) * g


def _row_plan(b, t, rows):
    if t >= rows:
        assert t % rows == 0
        return 1, rows
    assert rows % t == 0 and b % (rows // t) == 0
    return rows // t, t


def _inproj_kernel(x_ref, g_ref, sh_ref, sc_ref, w_ref, o_ref, *, col_chunk):
    gg, tg, d = x_ref.shape
    h = _modnorm(x_ref[...], g_ref[...], sh_ref[...], sc_ref[...])
    hb = h.reshape(gg * tg, d).astype(BF16)
    n = w_ref.shape[1]
    for c in range(0, n, col_chunk):
        o_ref[:, c:c + col_chunk] = _dot(hb, w_ref[:, c:c + col_chunk])


def _inproj(x3, g, sh, sc, w, rows):
    b, t, d = x3.shape
    gg, tg = _row_plan(b, t, rows)
    nj = t // tg
    n = w.shape[1]
    return pl.pallas_call(
        functools.partial(_inproj_kernel, col_chunk=512),
        grid=(b // gg, nj),
        in_specs=[
            pl.BlockSpec((gg, tg, d), lambda i, j: (i, j, 0)),
            pl.BlockSpec((1, d), lambda i, j: (0, 0)),
            pl.BlockSpec((gg, 1, d), lambda i, j: (i, 0, 0)),
            pl.BlockSpec((gg, 1, d), lambda i, j: (i, 0, 0)),
            pl.BlockSpec((d, n), lambda i, j: (0, 0)),
        ],
        out_specs=pl.BlockSpec((gg * tg, n), lambda i, j: (i * nj + j, 0)),
        out_shape=jax.ShapeDtypeStruct((b * t, n), F32),
        compiler_params=_cparams(("arbitrary", "arbitrary")),
        name="inproj",
    )(x3, g, sh, sc, w)


def _rope_group(z, cc, ss):
    return z * cc + pltpu.roll(z, HEAD_W - QK_ROPE, axis=1) * ss


def _attn_prep_kernel(cq_ref, ckv_ref, kr_ref, cc_ref, ss_ref, gq_ref, wq_ref, gkv_ref,
                      wk_ref, wv_ref, q_ref, k_ref, v_ref, kv_ref, kro_ref, *, transposed):
    cck = cc_ref[...]
    ss = ss_ref[...]
    lane = lax.broadcasted_iota(jnp.int32, cck.shape, 1)
    ccq = jnp.where(lane < QK_NOPE, 1.0, cck)
    qn = _rmsnorm2(cq_ref[...], gq_ref[...]).astype(BF16)
    ckv = _rmsnorm2(ckv_ref[...], gkv_ref[...])
    kv_ref[...] = ckv
    ckv_b = ckv.astype(BF16)
    kr = _rope_group(kr_ref[...], cck, ss)
    kro_ref[...] = kr
    v = _dot(ckv_b, wv_ref[...])
    rows = v.shape[0]
    if transposed:
        ones = jnp.ones((V_ROWS - V_HEAD, rows), F32)
        vt = v.T
        for h in range(N_HEADS):
            v_ref[h * V_ROWS:(h + 1) * V_ROWS, :] = jnp.concatenate(
                [vt[h * V_HEAD:(h + 1) * V_HEAD, :], ones], axis=0).astype(BF16)
    else:
        v_ref[...] = v.astype(BF16)
    for h in range(N_HEADS):
        sl = slice(h * HEAD_W, (h + 1) * HEAD_W)
        zq = _dot(qn, wq_ref[:, sl])
        qh = _rope_group(zq, ccq, ss) * Q_SCALE
        if transposed:
            q_ref[sl, :] = qh.T.astype(BF16)
        else:
            q_ref[:, sl] = qh.astype(BF16)
        k_ref[:, sl] = (_dot(ckv_b, wk_ref[:, sl]) + kr).astype(BF16)


def _attn_prep(z, cc, ss, gq, wq, gkv, wk, wv, rows, transposed):
    n = z.shape[0]
    npos = cc.shape[0] // rows
    row = lambda i: (i, 0)
    col = lambda i: (0, i)
    const = lambda i: (0, 0)
    if transposed:
        q_spec = pl.BlockSpec((N_HEADS * HEAD_W, rows), col)
        q_shape = jax.ShapeDtypeStruct((N_HEADS * HEAD_W, n), BF16)
        v_spec = pl.BlockSpec((N_HEADS * V_ROWS, rows), col)
        v_shape = jax.ShapeDtypeStruct((N_HEADS * V_ROWS, n), BF16)
    else:
        q_spec = pl.BlockSpec((rows, N_HEADS * HEAD_W), row)
        q_shape = jax.ShapeDtypeStruct((n, N_HEADS * HEAD_W), BF16)
        v_spec = pl.BlockSpec((rows, N_HEADS * V_HEAD), row)
        v_shape = jax.ShapeDtypeStruct((n, N_HEADS * V_HEAD), BF16)
    return pl.pallas_call(
        functools.partial(_attn_prep_kernel, transposed=transposed),
        grid=(n // rows,),
        in_specs=[
            pl.BlockSpec((rows, Q_LORA), lambda i: (i, COL_CQ // Q_LORA)),
            pl.BlockSpec((rows, KV_LORA), lambda i: (i, COL_CKV // KV_LORA)),
            pl.BlockSpec((rows, HEAD_W), lambda i: (i, COL_KR // HEAD_W)),
            pl.BlockSpec((rows, HEAD_W), lambda i: (i % npos, 0)),
            pl.BlockSpec((rows, HEAD_W), lambda i: (i % npos, 0)),
            pl.BlockSpec((1, Q_LORA), const),
            pl.BlockSpec(wq.shape, const),
            pl.BlockSpec((1, KV_LORA), const),
            pl.BlockSpec(wk.shape, const),
            pl.BlockSpec(wv.shape, const),
        ],
        out_specs=[
            q_spec,
            pl.BlockSpec((rows, N_HEADS * HEAD_W), row),
            v_spec,
            pl.BlockSpec((rows, KV_LORA), row),
            pl.BlockSpec((rows, HEAD_W), row),
        ],
        out_shape=[
            q_shape,
            jax.ShapeDtypeStruct((n, N_HEADS * HEAD_W), BF16),
            v_shape,
            jax.ShapeDtypeStruct((n, KV_LORA), F32),
            jax.ShapeDtypeStruct((n, HEAD_W), F32),
        ],
        compiler_params=_cparams(("arbitrary",)),
        name="attn_prep",
    )(z, z, z, cc, ss, gq, wq, gkv, wk, wv)


def _cache_expand_kernel(ckv_ref, kr_ref, wk_ref, wv_ref, place_ref, k_ref, v_ref):
    ckv_b = ckv_ref[...].astype(BF16)
    krp = _dot(kr_ref[...].astype(BF16), place_ref[...])
    v_ref[...] = _dot(ckv_b, wv_ref[...]).astype(BF16)
    for h in range(N_HEADS):
        sl = slice(h * HEAD_W, (h + 1) * HEAD_W)
        k_ref[:, sl] = (_dot(ckv_b, wk_ref[:, sl]) + krp).astype(BF16)


def _cache_expand(ckv, kr, wk, wv, place, rows):
    n = ckv.shape[0]
    row = lambda i: (i, 0)
    const = lambda i: (0, 0)
    return pl.pallas_call(
        _cache_expand_kernel,
        grid=(n // rows,),
        in_specs=[
            pl.BlockSpec((rows, KV_LORA), row),
            pl.BlockSpec((rows, QK_ROPE), row),
            pl.BlockSpec(wk.shape, const),
            pl.BlockSpec(wv.shape, const),
            pl.BlockSpec(place.shape, const),
        ],
        out_specs=[
            pl.BlockSpec((rows, N_HEADS * HEAD_W), row),
            pl.BlockSpec((rows, N_HEADS * V_HEAD), row),
        ],
        out_shape=[
            jax.ShapeDtypeStruct((n, N_HEADS * HEAD_W), BF16),
            jax.ShapeDtypeStruct((n, N_HEADS * V_HEAD), BF16),
        ],
        compiler_params=_cparams(("arbitrary",)),
        name="cache_expand",
    )(ckv, kr, wk, wv, place)


def _flash_step(qt_ref, k_ref, vt_ref, m_ref, acc_ref, masked, q0, k0):
    tk, tq = k_ref.shape[0], qt_ref.shape[1]
    if masked:
        kc = (k0 + lax.broadcasted_iota(jnp.int32, (tk, tq), 0)) // CHUNK
        qc = (q0 + lax.broadcasted_iota(jnp.int32, (tk, tq), 1)) // CHUNK
        keep = kc <= qc
    def scores(h):
        sl = slice(h * HEAD_W, (h + 1) * HEAD_W)
        s = _dot(k_ref[:, sl], qt_ref[sl, :])
        return jnp.where(keep, s, NEG_INF) if masked else s

    def stats(h, s):
        m_prev = m_ref[h]
        m_new = jnp.maximum(m_prev, jnp.max(s, axis=0, keepdims=True))
        m_ref[h] = m_new
        return m_new, jnp.exp2(m_prev - m_new)

    s = {0: scores(0)}
    if N_HEADS > 1:
        s[1] = scores(1)
    st = {0: stats(0, s[0])}
    for h in range(N_HEADS):
        vs = slice(h * V_ROWS, (h + 1) * V_ROWS)
        if h + 2 < N_HEADS:
            s[h + 2] = scores(h + 2)
        if h + 1 < N_HEADS:
            st[h + 1] = stats(h + 1, s[h + 1])
        m_new, alpha = st.pop(h)
        p = jnp.exp2(s.pop(h) - m_new).astype(BF16)
        acc_ref[h] = alpha * acc_ref[h] + _dot(vt_ref[vs, :], p)


def _flash_kernel(qt_ref, k_ref, vt_ref, o_ref, m_ref, acc_ref, *, ratio):
    qi = pl.program_id(1)
    ki = pl.program_id(2)
    tk, tq = k_ref.shape[0], qt_ref.shape[1]

    @pl.when(ki == 0)
    def _():
        m_ref[...] = jnp.full(m_ref.shape, NEG_INF, F32)
        acc_ref[...] = jnp.zeros(acc_ref.shape, F32)

    @pl.when(ki < qi * ratio)
    def _():
        _flash_step(qt_ref, k_ref, vt_ref, m_ref, acc_ref, False, 0, 0)

    @pl.when(jnp.logical_and(ki >= qi * ratio, ki < (qi + 1) * ratio))
    def _():
        _flash_step(qt_ref, k_ref, vt_ref, m_ref, acc_ref, True, qi * tq, ki * tk)

    @pl.when(ki == (qi + 1) * ratio - 1)
    def _():
        for h in range(N_HEADS):
            a = acc_ref[h]
            o = a[:V_HEAD, :] / a[V_HEAD:V_HEAD + 1, :]
            o_ref[:, h * V_HEAD:(h + 1) * V_HEAD] = o.T.astype(o_ref.dtype)


def _flash_prompt(qt, k, vt, b, t, tq, tk):
    nq, nk = t // tq, t // tk
    ratio = tq // tk
    last = lambda qi, ki: jnp.minimum(ki, (qi + 1) * ratio - 1)
    return pl.pallas_call(
        functools.partial(_flash_kernel, ratio=ratio),
        grid=(b, nq, nk),
        in_specs=[
            pl.BlockSpec((N_HEADS * HEAD_W, tq), lambda bi, qi, ki: (0, bi * nq + qi)),
            pl.BlockSpec((tk, N_HEADS * HEAD_W), lambda bi, qi, ki: (bi * nk + last(qi, ki), 0)),
            pl.BlockSpec((N_HEADS * V_ROWS, tk), lambda bi, qi, ki: (0, bi * nk + last(qi, ki))),
        ],
        out_specs=pl.BlockSpec((tq, N_HEADS * V_HEAD), lambda bi, qi, ki: (bi * nq + qi, 0)),
        out_shape=jax.ShapeDtypeStruct((b * t, N_HEADS * V_HEAD), BF16),
        scratch_shapes=[
            pltpu.VMEM((N_HEADS, 1, tq), F32),
            pltpu.VMEM((N_HEADS, V_ROWS, tq), F32),
        ],
        compiler_params=_cparams(("arbitrary", "arbitrary", "arbitrary")),
        name="flash_prompt",
    )(qt, k, vt)


def _attn_sample_kernel(q_ref, kc_ref, vc_ref, kn_ref, vn_ref, o_ref):
    for h in range(N_HEADS):
        sl = slice(h * HEAD_W, (h + 1) * HEAD_W)
        vs = slice(h * V_HEAD, (h + 1) * V_HEAD)
        qh = q_ref[:, sl]
        sc = _dot_nt(qh, kc_ref[:, sl])
        sn = _dot_nt(qh, kn_ref[:, sl])
        m = jnp.maximum(jnp.max(sc, axis=1, keepdims=True), jnp.max(sn, axis=1, keepdims=True))
        pc = jnp.exp2(sc - m)
        pn = jnp.exp2(sn - m)
        den = jnp.sum(pc, axis=1, keepdims=True) + jnp.sum(pn, axis=1, keepdims=True)
        o = _dot(pc.astype(BF16), vc_ref[:, vs]) + _dot(pn.astype(BF16), vn_ref[:, vs])
        o_ref[:, vs] = (o / den).astype(o_ref.dtype)


def _attn_sample(q, kc, vc, kn, vn, b, t, past):
    return pl.pallas_call(
        _attn_sample_kernel,
        grid=(b,),
        in_specs=[
            pl.BlockSpec((t, N_HEADS * HEAD_W), lambda i: (i, 0)),
            pl.BlockSpec((past, N_HEADS * HEAD_W), lambda i: (i, 0)),
            pl.BlockSpec((past, N_HEADS * V_HEAD), lambda i: (i, 0)),
            pl.BlockSpec((t, N_HEADS * HEAD_W), lambda i: (i, 0)),
            pl.BlockSpec((t, N_HEADS * V_HEAD), lambda i: (i, 0)),
        ],
        out_specs=pl.BlockSpec((t, N_HEADS * V_HEAD), lambda i: (i, 0)),
        out_shape=jax.ShapeDtypeStruct((b * t, N_HEADS * V_HEAD), BF16),
        compiler_params=_cparams(("arbitrary",)),
        name="attn_sample",
    )(q, kc, vc, kn, vn)


def _mix_kernel(x_ref, gt_ref, ubch_ref, g_ref, attn_ref, pu_ref, pch_ref, hp_ref, hc_ref,
                wpool_ref, pscale_ref, convw_ref, wbr_ref, wout_ref,
                xo_ref, ptail_ref, ctail_ref, extp_ref, extc_ref, *, n_hist):
    gg, tg, d = x_ref.shape
    j = pl.program_id(1)
    first = j == 0

    ubch = ubch_ref[...]
    u = ubch[:, COL_U:COL_U + POOL_W].reshape(gg, tg, POOL_W)
    bgate = ubch[:, COL_B:COL_B + CONV_W].reshape(gg, tg, CONV_W)
    cu = (ubch[:, COL_C:COL_C + CONV_W] * ubch[:, COL_H:COL_H + CONV_W]).reshape(gg, tg, CONV_W)

    hist_p = jnp.where(first, hp_ref[...], pu_ref[...].reshape(1, HIST_ROWS, POOL_W))
    pch = pch_ref[...]
    prev_cu = (pch[:, :CONV_W] * pch[:, CONV_W:])[HIST_ROWS - CONV_HIST_ROWS:]
    hist_c = jnp.where(first, hc_ref[...], prev_cu.reshape(1, CONV_HIST_ROWS, CONV_W))

    extp_ref[:, :HIST_ROWS, :] = hist_p
    extp_ref[:, HIST_ROWS:, :] = u
    extc_ref[:, :CONV_HIST_ROWS, :] = hist_c
    extc_ref[:, CONV_HIST_ROWS:, :] = cu
    ptail_ref[...] = u[:, tg - HIST_ROWS:, :]
    ctail_ref[...] = cu[:, tg - CONV_HIST_ROWS:, :]

    tpos = j * tg + lax.broadcasted_iota(jnp.int32, (1, tg, 1), 1)
    pooled = []
    for gi, w in enumerate(POOL_WINDOWS):
        cs = slice(gi * POOL_GROUP, (gi + 1) * POOL_GROUP)
        acc = u[:, :, cs]
        for kk in range(1, w):
            acc = acc + extp_ref[:, HIST_ROWS - kk:HIST_ROWS - kk + tg, cs]
        cnt = jnp.minimum(tpos + 1 + n_hist, w).astype(F32)
        dd = acc / cnt - u[:, :, cs]
        pooled.append(_dot(dd.reshape(gg * tg, POOL_GROUP).astype(BF16), wpool_ref[gi]))
    pool = jnp.concatenate(pooled, axis=1) * pscale_ref[...]

    cw = convw_ref[...]
    yc = (cw[0:1, :] * extc_ref[:, CONV_HIST_ROWS - 2:CONV_HIST_ROWS - 2 + tg, :]
          + cw[1:2, :] * extc_ref[:, CONV_HIST_ROWS - 1:CONV_HIST_ROWS - 1 + tg, :]
          + cw[2:3, :] * cu)
    conv = (bgate * yc).reshape(gg * tg, CONV_W)

    gates = jax.nn.sigmoid(g_ref[...])
    mixed = (gates[:, 0:d] * _dot(attn_ref[...], wbr_ref[0])
             + gates[:, d:2 * d] * _dot(pool.astype(BF16), wbr_ref[1])
             + gates[:, 2 * d:3 * d] * _dot(conv.astype(BF16), wbr_ref[2]))
    out = _dot(mixed.astype(BF16), wout_ref[...])
    xo_ref[...] = x_ref[...] + gt_ref[...] * out.reshape(gg, tg, d)


def _mix(x3, gt, z, attn, hist_p, hist_c, wpool, pscale, convw, wbr, wout, rows, n_hist):
    b, t, d = x3.shape
    gg, tg = _row_plan(b, t, rows)
    nj = t // tg
    rb = lambda i, j: i * nj + j
    hpb = tg // HIST_ROWS

    def prev_rows(i, j):
        return jnp.maximum(rb(i, j) * hpb - 1, 0)

    const2 = lambda i, j: (0, 0)
    const3 = lambda i, j: (0, 0, 0)
    return pl.pallas_call(
        functools.partial(_mix_kernel, n_hist=n_hist),
        grid=(b // gg, nj),
        in_specs=[
            pl.BlockSpec((gg, tg, d), lambda i, j: (i, j, 0)),
            pl.BlockSpec((gg, 1, d), lambda i, j: (i, 0, 0)),
            pl.BlockSpec((gg * tg, COL_CQ), lambda i, j: (rb(i, j), 0)),
            pl.BlockSpec((gg * tg, 3 * d), lambda i, j: (rb(i, j), COL_G // (3 * d))),
            pl.BlockSpec((gg * tg, N_HEADS * V_HEAD), lambda i, j: (rb(i, j), 0)),
            pl.BlockSpec((HIST_ROWS, POOL_W), lambda i, j: (prev_rows(i, j), 0)),
            pl.BlockSpec((HIST_ROWS, 2 * CONV_W), lambda i, j: (prev_rows(i, j), COL_C // (2 * CONV_W))),
            pl.BlockSpec((gg, HIST_ROWS, POOL_W), lambda i, j: (i, 0, 0)),
            pl.BlockSpec((gg, CONV_HIST_ROWS, CONV_W), lambda i, j: (i, 0, 0)),
            pl.BlockSpec(wpool.shape, const3),
            pl.BlockSpec(pscale.shape, const2),
            pl.BlockSpec(convw.shape, const2),
            pl.BlockSpec(wbr.shape, const3),
            pl.BlockSpec(wout.shape, const2),
        ],
        out_specs=[
            pl.BlockSpec((gg, tg, d), lambda i, j: (i, j, 0)),
            pl.BlockSpec((gg, HIST_ROWS, POOL_W), lambda i, j: (i, 0, 0)),
            pl.BlockSpec((gg, CONV_HIST_ROWS, CONV_W), lambda i, j: (i, 0, 0)),
        ],
        out_shape=[
            jax.ShapeDtypeStruct((b, t, d), F32),
            jax.ShapeDtypeStruct((b, HIST_ROWS, POOL_W), F32),
            jax.ShapeDtypeStruct((b, CONV_HIST_ROWS, CONV_W), F32),
        ],
        scratch_shapes=[
            pltpu.VMEM((gg, HIST_ROWS + tg, POOL_W), F32),
            pltpu.VMEM((gg, CONV_HIST_ROWS + tg, CONV_W), F32),
        ],
        compiler_params=_cparams(("arbitrary", "arbitrary")),
        name="mix_merge",
    )(x3, gt, z, z, attn, z, z, hist_p, hist_c, wpool, pscale, convw, wbr, wout)


def _peer_query_kernel(x_ref, g_ref, sh_ref, sc_ref, wq_ref, keys_ref, ht_ref, s1_ref, s2_ref):
    gg, tg, d = x_ref.shape
    h = _modnorm(x_ref[...], g_ref[...], sh_ref[...], sc_ref[...]).reshape(gg * tg, d)
    ht_ref[0] = h.T.astype(BF16)
    hb = h.astype(BF16)
    half = D_KEY // 2
    k1 = keys_ref[0]
    k2 = keys_ref[1]
    for hh in range(PEER_HEADS):
        q = _dot(hb, wq_ref[:, hh * D_KEY:(hh + 1) * D_KEY]).astype(BF16)
        s1_ref[hh, 0] = _dot_nt(k1, q[:, :half])
        s2_ref[hh, 0] = _dot_nt(k2, q[:, half:])


def _peer_query(x3, g, sh, sc, wq, keys, rows):
    b, t, d = x3.shape
    gg, tg = _row_plan(b, t, rows)
    nj = t // tg
    nt = b * t // rows
    tok = lambda i, j: (0, i * nj + j, 0, 0)
    return pl.pallas_call(
        _peer_query_kernel,
        grid=(b // gg, nj),
        in_specs=[
            pl.BlockSpec((gg, tg, d), lambda i, j: (i, j, 0)),
            pl.BlockSpec((1, d), lambda i, j: (0, 0)),
            pl.BlockSpec((gg, 1, d), lambda i, j: (i, 0, 0)),
            pl.BlockSpec((gg, 1, d), lambda i, j: (i, 0, 0)),
            pl.BlockSpec(wq.shape, lambda i, j: (0, 0)),
            pl.BlockSpec(keys.shape, lambda i, j: (0, 0, 0)),
        ],
        out_specs=[
            pl.BlockSpec((1, d, rows), lambda i, j: (i * nj + j, 0, 0)),
            pl.BlockSpec((PEER_HEADS, 1, N_KEYS, rows), tok),
            pl.BlockSpec((PEER_HEADS, 1, N_KEYS, rows), tok),
        ],
        out_shape=[
            jax.ShapeDtypeStruct((nt, d, rows), BF16),
            jax.ShapeDtypeStruct((PEER_HEADS, nt, N_KEYS, rows), F32),
            jax.ShapeDtypeStruct((PEER_HEADS, nt, N_KEYS, rows), F32),
        ],
        compiler_params=_cparams(("arbitrary", "arbitrary")),
        name="peer_query",
    )(x3, g, sh, sc, wq, keys)


def _top_ranks(s):
    nk, r = s.shape
    iota = lax.broadcasted_iota(jnp.int32, (nk, r), 0)
    iota_k = lax.broadcasted_iota(jnp.int32, (PEER_TOPK, r), 0)
    rank = jnp.full((nk, r), NOT_TOP, F32)
    vals = jnp.zeros((PEER_TOPK, r), F32)
    x = s
    for it in range(PEER_TOPK):
        m = jnp.max(x, axis=0, keepdims=True)
        idx = jnp.min(jnp.where(x == m, iota, nk), axis=0, keepdims=True)
        hit = iota == idx
        rank = jnp.where(hit, float(it), rank)
        x = jnp.where(hit, -jnp.inf, x)
        vals = jnp.where(iota_k == it, m, vals)
    return rank, vals


def _peer_select_kernel(s1_ref, s2_ref, c_ref, lr_ref, rk2_ref, d_ref):
    s1 = s1_ref[0, 0]
    s2 = s2_ref[0, 0]
    r = s1.shape[1]
    rank1, v1 = _top_ranks(s1)
    rank2, v2 = _top_ranks(s2)
    iota_k = lax.broadcasted_iota(jnp.int32, (PEER_TOPK, r), 0)
    cnt = jnp.zeros((PEER_TOPK, r), jnp.int32)
    front = v1 + v2[0:1, :]
    top = front[0:1, :]
    zsum = jnp.zeros((1, r), F32)
    for _ in range(PEER_TOPK):
        m = jnp.max(front, axis=0, keepdims=True)
        a = jnp.min(jnp.where(front == m, iota_k, PEER_TOPK), axis=0, keepdims=True)
        hit = iota_k == a
        zsum = zsum + jnp.exp(m - top)
        cnt = jnp.where(hit, cnt + 1, cnt)
        nxt = jnp.full((PEER_TOPK, r), -jnp.inf, F32)
        for bcol in range(1, PEER_TOPK):
            nxt = jnp.where(cnt == bcol, v2[bcol:bcol + 1, :], nxt)
        front = jnp.where(hit, v1 + nxt, front)
    cntf = cnt.astype(F32)
    lr = jnp.zeros(rank1.shape, F32)
    for a in range(PEER_TOPK):
        lr = jnp.where(rank1 == float(a), cntf[a:a + 1, :], lr)
    c_ref[0, 0] = jnp.exp(s1 - v1[0:1, :]) / zsum
    lr_ref[0, 0] = lr
    rk2_ref[0, 0] = rank2.astype(BF16)
    d_ref[0, 0] = jnp.exp(s2 - v2[0:1, :]).astype(BF16)


def _peer_select(s1, s2):
    hh, nt, nk, lanes = s1.shape
    spec = pl.BlockSpec((1, 1, nk, lanes), lambda i, h: (h, i, 0, 0))
    f32 = jax.ShapeDtypeStruct(s1.shape, F32)
    b16 = jax.ShapeDtypeStruct(s1.shape, BF16)
    return pl.pallas_call(
        _peer_select_kernel,
        grid=(nt, hh),
        in_specs=[spec, spec],
        out_specs=[spec, spec, spec, spec],
        out_shape=[f32, f32, b16, b16],
        compiler_params=_cparams(("arbitrary", "arbitrary")),
        name="peer_select",
    )(s1, s2)


def _gelu(x):
    return 0.5 * x * (1.0 + lax.erf(x * (2.0 ** -0.5)))


def _peer_dense_kernel(x_ref, gt_ref, ht_ref, u_ref, vt_ref, c_ref, lr_ref, rk2_ref, d_ref,
                       xo_ref, a_ref, w_ref, acc_ref, *, rows_per_chunk):
    gg, tg, d = x_ref.shape
    nt, _, lanes = ht_ref.shape
    e = pl.program_id(1)
    slab = 16
    n_slab = N_KEYS // slab

    @pl.when(e == 0)
    def _():
        acc_ref[...] = jnp.zeros(acc_ref.shape, F32)

    w_ref[(nt - 1) % 2] = jnp.zeros(w_ref.shape[1:], BF16)

    def tile(lt, carry):
        prev = (lt + nt - 1) % nt
        a_ref[...] = _dot(u_ref[...], ht_ref[lt])
        acc_ref[prev] += _dot(vt_ref[...], w_ref[prev % 2])
        w_cur = w_ref.at[lt % 2]
        for ii in range(rows_per_chunk):
            gate = [None] * n_slab
            for hh in range(PEER_HEADS):
                c16 = jnp.broadcast_to(c_ref[hh, lt, ii:ii + 1, :], (slab, lanes)).astype(BF16)
                l16 = jnp.broadcast_to(lr_ref[hh, lt, ii:ii + 1, :], (slab, lanes)).astype(BF16)
                for jv in range(n_slab):
                    rs = slice(jv * slab, (jv + 1) * slab)
                    dd = d_ref[hh, lt, rs, :]
                    term = c16 * jnp.where(rk2_ref[hh, lt, rs, :] < l16, dd, jnp.zeros_like(dd))
                    gate[jv] = term if gate[jv] is None else gate[jv] + term
            for jv in range(n_slab):
                rows = slice(ii * N_KEYS + jv * slab, ii * N_KEYS + (jv + 1) * slab)
                w_cur[rows, :] = _gelu(a_ref[rows, :]).astype(BF16) * gate[jv]
        return carry

    for lt in range(nt):
        tile(lt, 0)
    acc_ref[nt - 1] += _dot(vt_ref[...], w_ref[(nt - 1) % 2])

    @pl.when(e == pl.num_programs(1) - 1)
    def _():
        def finish(lt, carry):
            upd = acc_ref[lt].T
            if gg == 1:
                rows = pl.ds(pl.multiple_of(lt * lanes, lanes), lanes)
                xo_ref[0, rows, :] = x_ref[0, rows, :] + gt_ref[0] * upd
            else:
                per = lanes // tg
                rows = pl.ds(pl.multiple_of(lt * per, per), per)
                xo_ref[rows] = x_ref[rows] + gt_ref[rows] * upd.reshape(per, tg, d)
            return carry

        lax.fori_loop(0, nt, finish, 0)


def _peer_dense(x3, gt, ht, u, vt, c, lr, rk2, dd, tiles_per_block, rows_per_chunk):
    b, t, d = x3.shape
    nt_all, _, lanes = ht.shape
    gg, tg = _row_plan(b, t, tiles_per_block * lanes)
    nj = t // tg
    ne = u.shape[0]
    ec = rows_per_chunk * N_KEYS
    full = pl.BlockSpec((PEER_HEADS, tiles_per_block, N_KEYS, lanes), lambda i, e: (0, i, 0, 0))
    part = pl.BlockSpec((PEER_HEADS, tiles_per_block, rows_per_chunk, lanes),
                        lambda i, e: (0, i, e, 0))
    return pl.pallas_call(
        functools.partial(_peer_dense_kernel, rows_per_chunk=rows_per_chunk),
        grid=(nt_all // tiles_per_block, ne // ec),
        in_specs=[
            pl.BlockSpec((gg, tg, d), lambda i, e: (i // nj, i % nj, 0)),
            pl.BlockSpec((gg, 1, d), lambda i, e: (i // nj, 0, 0)),
            pl.BlockSpec((tiles_per_block, d, lanes), lambda i, e: (i, 0, 0)),
            pl.BlockSpec((ec, d), lambda i, e: (e, 0)),
            pl.BlockSpec((d, ec), lambda i, e: (0, e)),
            part, part, full, full,
        ],
        out_specs=pl.BlockSpec((gg, tg, d), lambda i, e: (i // nj, i % nj, 0)),
        out_shape=jax.ShapeDtypeStruct((b, t, d), F32),
        scratch_shapes=[
            pltpu.VMEM((ec, lanes), F32),
            pltpu.VMEM((2, ec, lanes), BF16),
            pltpu.VMEM((tiles_per_block, d, lanes), F32),
        ],
        compiler_params=_cparams(("arbitrary", "arbitrary")),
        name="peer_dense",
    )(x3, gt, ht, u, vt, c, lr, rk2, dd)


def _final_kernel(x_ref, g_ref, o_ref):
    o_ref[...] = _rmsnorm2(x_ref[...], g_ref[...])


def _final_norm(x3, g, rows):
    b, t, d = x3.shape
    x2 = x3.reshape(b * t, d)
    y = pl.pallas_call(
        _final_kernel,
        grid=(b * t // rows,),
        in_specs=[pl.BlockSpec((rows, d), lambda i: (i, 0)), pl.BlockSpec((1, d), lambda i: (0, 0))],
        out_specs=pl.BlockSpec((rows, d), lambda i: (i, 0)),
        out_shape=jax.ShapeDtypeStruct((b * t, d), F32),
        compiler_params=_cparams(("arbitrary",)),
        name="final_norm",
    )(x2, g)
    return y.reshape(b, t, d)


def _swap_halves(w):
    half = w.shape[-1] // 2
    return jnp.concatenate([w[..., half:], w[..., :half]], axis=-1)


def _prep_layer(p, l):
    d = p['w_in'].shape[1]
    w_in = p['w_in'][l]
    offs = [0]
    for nsz in (Q_LORA, KV_LORA, QK_ROPE, POOL_W, CONV_W, CONV_W, CONV_W, 3 * d):
        offs.append(offs[-1] + nsz)
    w_cq, w_ckv, w_kr, w_u, w_b, w_c, w_h, w_g = [w_in[:, offs[i]:offs[i + 1]] for i in range(8)]
    zeros = lambda n: jnp.zeros((d, n), w_in.dtype)
    w_krg = jnp.concatenate([zeros(QK_NOPE), w_kr, _swap_halves(w_kr)], axis=1)
    w_in_r = jnp.concatenate(
        [w_u, w_b, w_c, w_h, w_cq, w_ckv, w_krg, zeros(COL_G - COL_KR - HEAD_W), w_g], axis=1)
    assert w_in_r.shape[1] == D_IN_PAD

    w_uq = p['w_uq'][l]
    wq_r = jnp.concatenate(
        [w_uq[..., :QK_NOPE], w_uq[..., QK_NOPE:], _swap_halves(w_uq[..., QK_NOPE:])], axis=-1)
    wq_r = wq_r.reshape(Q_LORA, N_HEADS * HEAD_W)
    w_ukv = p['w_ukv'][l]
    wk_r = jnp.concatenate(
        [w_ukv[..., :QK_NOPE], jnp.zeros((KV_LORA, N_HEADS, HEAD_W - QK_NOPE), w_ukv.dtype)],
        axis=-1).reshape(KV_LORA, N_HEADS * HEAD_W)
    wv_r = w_ukv[..., QK_NOPE:].reshape(KV_LORA, N_HEADS * V_HEAD)
    return dict(
        w_in=w_in_r.astype(BF16), wq=wq_r.astype(BF16), wk=wk_r.astype(BF16), wv=wv_r.astype(BF16),
        g_mix=p['g_mix'][l][None, :], g_q=p['g_q'][l][None, :], g_kv=p['g_kv'][l][None, :],
        w_pool=p['w_pool'][l].astype(BF16), pool_scale=p['pool_scale'][l][None, :],
        conv_w=jnp.pad(p['conv_w'][l], ((0, 8 - CONV_K), (0, 0))),
        w_branch=p['w_branch'][l].astype(BF16), w_out=p['w_out'][l].astype(BF16),
        g_ffn=p['g_ffn'][l][None, :],
        peer_wq=p['peer_wq'][l].reshape(d, PEER_HEADS * D_KEY).astype(BF16),
        peer_keys=p['peer_keys'][l].astype(BF16),
        peer_u=p['peer_u'][l].astype(BF16),
        peer_vt=p['peer_v'][l].T.astype(BF16),
    )


def _rope_tables(pos):
    half = QK_ROPE // 2
    inv = ROPE_THETA ** (-jnp.arange(half, dtype=F32) / half)
    ang = pos.astype(F32)[:, None] * inv[None, :]
    cos, sin = jnp.cos(ang), jnp.sin(ang)
    z = lambda n: jnp.zeros((pos.shape[0], n), F32)
    cc = jnp.concatenate([z(QK_NOPE), cos, cos, z(QK_ROPE)], axis=1)
    ss = jnp.concatenate([z(QK_NOPE), -sin, sin, z(QK_ROPE)], axis=1)
    return cc, ss


def _trunk(x, mods, pos, n_hist, cache, hist_pool, hist_conv, layers, g_final, cfg):
    b, t, d = x.shape
    rows = cfg['rows']
    n = b * t
    cc, ss = _rope_tables(pos)
    if t < rows:
        cc = jnp.tile(cc, (rows // t, 1))
        ss = jnp.tile(ss, (rows // t, 1))
    place = jnp.concatenate(
        [jnp.zeros((QK_ROPE, QK_NOPE), F32), jnp.eye(QK_ROPE, dtype=F32),
         jnp.zeros((QK_ROPE, HEAD_W - QK_NOPE - QK_ROPE), F32)], axis=1).astype(BF16)
    new_kv, new_kr, new_pool, new_conv = [], [], [], []
    for l, lw in enumerate(layers):
        mod = mods[l].reshape(b, 1, 6 * d)
        sh1, sc1, gt1, sh2, sc2, gt2 = [mod[:, :, i * d:(i + 1) * d] for i in range(6)]
        z = _inproj(x, lw['g_mix'], sh1, sc1, lw['w_in'], rows)
        q, k, v, ckv, krg = _attn_prep(z, cc, ss, lw['g_q'], lw['wq'], lw['g_kv'], lw['wk'],
                                       lw['wv'], rows, cache is None)
        new_kv.append(ckv.reshape(b, t, KV_LORA))
        new_kr.append(krg[:, QK_NOPE:QK_NOPE + QK_ROPE].reshape(b, t, QK_ROPE))
        if cache is None:
            attn = _flash_prompt(q, k, v, b, t, cfg['attn_tq'], cfg['attn_tk'])
        else:
            ckv_c, kr_c = cache
            past = ckv_c.shape[2]
            kc, vc = _cache_expand(ckv_c[l].reshape(b * past, KV_LORA),
                                   kr_c[l].reshape(b * past, QK_ROPE),
                                   lw['wk'], lw['wv'], place, cfg['cache_rows'])
            attn = _attn_sample(q, kc, vc, k, v, b, t, past)
        hp = jnp.pad(hist_pool[l], ((0, 0), (HIST_ROWS - POOL_HIST, 0), (0, 0)))
        hc = jnp.pad(hist_conv[l], ((0, 0), (CONV_HIST_ROWS - (CONV_K - 1), 0), (0, 0)))
        x, ptail, ctail = _mix(x, gt1, z, attn, hp, hc, lw['w_pool'], lw['pool_scale'],
                               lw['conv_w'], lw['w_branch'], lw['w_out'], rows, n_hist)
        new_pool.append(ptail[:, HIST_ROWS - POOL_HIST:, :])
        new_conv.append(ctail[:, CONV_HIST_ROWS - (CONV_K - 1):, :])
        ht, s1, s2 = _peer_query(x, lw['g_ffn'], sh2, sc2, lw['peer_wq'], lw['peer_keys'], rows)
        c, lr, rk2, dd = _peer_select(s1, s2)
        x = _peer_dense(x, gt2, ht, lw['peer_u'], lw['peer_vt'], c, lr, rk2, dd,
                        min(cfg['peer_tiles'], n // rows), cfg['rows_per_chunk'])
    y = _final_norm(x, g_final[None, :], rows)
    return y, jnp.stack(new_kv), jnp.stack(new_kr), jnp.stack(new_pool), jnp.stack(new_conv)


def _config(t_prompt):
    rows = min(256, t_prompt)
    return dict(rows=rows, attn_tq=min(512, t_prompt), attn_tk=min(256, t_prompt), cache_rows=256, peer_tiles=4,
                rows_per_chunk=8)


def kernel(x_prompt, x_sample, cache_kv_latent, cache_k_rope, state_pool, state_conv,
           c_prompt, c_sample, w_ada, b_ada, g_mix, w_in, g_q, w_uq, g_kv, w_ukv,
           w_pool, pool_scale, conv_w, w_branch, w_out, g_ffn, peer_wq, peer_keys,
           peer_u, peer_v, g_final):
    p = {'w_in': w_in, 'g_mix': g_mix, 'g_q': g_q, 'w_uq': w_uq, 'g_kv': g_kv, 'w_ukv': w_ukv,
         'w_pool': w_pool, 'pool_scale': pool_scale, 'conv_w': conv_w, 'w_branch': w_branch,
         'w_out': w_out, 'g_ffn': g_ffn, 'peer_wq': peer_wq, 'peer_keys': peer_keys,
         'peer_u': peer_u, 'peer_v': peer_v}
    depth = w_ada.shape[0]
    bp, tp, d = x_prompt.shape
    bs, ts, _ = x_sample.shape
    past = cache_kv_latent.shape[2]
    layers = [_prep_layer(p, l) for l in range(depth)]

    c_all = jnp.concatenate([c_prompt, c_sample], axis=0)
    pad = (-c_all.shape[0]) % 8
    c_all = jnp.pad(c_all, ((0, pad), (0, 0)))
    mods = _ada(c_all, w_ada, b_ada)
    mods_p, mods_s = mods[:, :bp], mods[:, bp:bp + bs]

    cfg = _config(tp)
    zp = jnp.zeros((depth, bp, POOL_HIST, POOL_W), x_prompt.dtype)
    zc = jnp.zeros((depth, bp, CONV_K - 1, CONV_W), x_prompt.dtype)
    y_p, p_kv, p_kr, p_pool, p_conv = _trunk(
        x_prompt, mods_p, jnp.arange(tp), 0, None, zp, zc, layers, g_final, cfg)
    y_s, s_kv, s_kr, s_pool, s_conv = _trunk(
        x_sample, mods_s, past + jnp.arange(ts), min(past, POOL_HIST),
        (cache_kv_latent, cache_k_rope), state_pool, state_conv, layers, g_final, cfg)
    return (y_p, y_s, p_kv, p_kr, p_pool, p_conv, s_kv, s_kr, s_pool, s_conv)
```

```python
import functools
import math

import jax
import jax.numpy as jnp
from jax import lax
from jax.experimental import pallas as pl
from jax.experimental.pallas import tpu as pltpu

F32 = jnp.float32
BF16 = jnp.bfloat16

EPS = 1e-6
N_HEADS = 8
QK_NOPE = 64
QK_ROPE = 32
V_HEAD = 64
Q_LORA = 512
KV_LORA = 256
ROPE_THETA = 10000.0
CHUNK = 64
ATTN_SCALE = (QK_NOPE + QK_ROPE) ** -0.5
NEG_INF = -1e30
POOL_WINDOWS = (2, 4, 8, 16)
POOL_GROUP = 128
POOL_W = 512
POOL_HIST = 15
CONV_W = 512
CONV_K = 3
PEER_HEADS = 8
N_KEYS = 128
D_KEY = 256
PEER_TOPK = 16
NOT_TOP = 99.0

LANES = 128
VMEM_LIMIT_BYTES = 56 * 2**20

HEAD_W = LANES
V_ROWS = V_HEAD + 16
Q_SCALE = ATTN_SCALE * math.log2(math.e)
HIST_ROWS = 16
CONV_HIST_ROWS = 8

COL_U, COL_B, COL_C, COL_H = 0, 512, 1024, 1536
COL_CQ = 2048
COL_CKV = 2560
COL_KR = 2816
COL_G = 3072
D_IN_PAD = 6144


def _cparams(sem):
    return pltpu.CompilerParams(dimension_semantics=sem, vmem_limit_bytes=VMEM_LIMIT_BYTES)


def _dot(a, b):
    return jnp.dot(a, b, preferred_element_type=F32)


def _dot_nt(a, b):
    return lax.dot_general(a, b, (((1,), (1,)), ((), ())), preferred_element_type=F32)


def _ada_kernel(c_ref, w_ref, b_ref, o_ref):
    c = c_ref[...]
    act = c * jax.nn.sigmoid(c)
    o_ref[0] = _dot(act.astype(BF16), w_ref[0].astype(BF16)) + b_ref[0]


def _ada(c_all, w_ada, b_ada):
    depth, d, n6 = w_ada.shape
    bp = c_all.shape[0]
    tn = 1536
    return pl.pallas_call(
        _ada_kernel,
        grid=(depth, n6 // tn),
        in_specs=[
            pl.BlockSpec((bp, d), lambda l, j: (0, 0)),
            pl.BlockSpec((1, d, tn), lambda l, j: (l, 0, j)),
            pl.BlockSpec((1, 1, tn), lambda l, j: (l, 0, j)),
        ],
        out_specs=pl.BlockSpec((1, bp, tn), lambda l, j: (l, 0, j)),
        out_shape=jax.ShapeDtypeStruct((depth, bp, n6), F32),
        compiler_params=_cparams(("arbitrary", "arbitrary")),
        name="ada",
    )(c_all, w_ada, b_ada.reshape(depth, 1, n6))


def _modnorm(x, g, sh, sc):
    ms = jnp.mean(x * x, axis=-1, keepdims=True)
    y = x * lax.rsqrt(ms + EPS) * g
    return y * (1.0 + sc) + sh


def _rmsnorm2(x, g):
    ms = jnp.mean(x * x, axis=-1, keepdims=True)
    return x * lax.rsqrt(ms + EPS) * g


def _row_plan(b, t, rows):
    if t >= rows:
        assert t % rows == 0
        return 1, rows
    assert rows % t == 0 and b % (rows // t) == 0
    return rows // t, t


def _inproj_kernel(x_ref, g_ref, sh_ref, sc_ref, w_ref, o_ref, *, col_chunk):
    gg, tg, d = x_ref.shape
    h = _modnorm(x_ref[...], g_ref[...], sh_ref[...], sc_ref[...])
    hb = h.reshape(gg * tg, d).astype(BF16)
    n = w_ref.shape[1]
    for c in range(0, n, col_chunk):
        o_ref[:, c:c + col_chunk] = _dot(hb, w_ref[:, c:c + col_chunk])


def _inproj(x3, g, sh, sc, w, rows):
    b, t, d = x3.shape
    gg, tg = _row_plan(b, t, rows)
    nj = t // tg
    n = w.shape[1]
    return pl.pallas_call(
        functools.partial(_inproj_kernel, col_chunk=512),
        grid=(b // gg, nj),
        in_specs=[
            pl.BlockSpec((gg, tg, d), lambda i, j: (i, j, 0)),
            pl.BlockSpec((1, d), lambda i, j: (0, 0)),
            pl.BlockSpec((gg, 1, d), lambda i, j: (i, 0, 0)),
            pl.BlockSpec((gg, 1, d), lambda i, j: (i, 0, 0)),
            pl.BlockSpec((d, n), lambda i, j: (0, 0)),
        ],
        out_specs=pl.BlockSpec((gg * tg, n), lambda i, j: (i * nj + j, 0)),
        out_shape=jax.ShapeDtypeStruct((b * t, n), F32),
        compiler_params=_cparams(("arbitrary", "arbitrary")),
        name="inproj",
    )(x3, g, sh, sc, w)


def _rope_group(z, cc, ss):
    return z * cc + pltpu.roll(z, HEAD_W - QK_ROPE, axis=1) * ss


def _attn_prep_kernel(cq_ref, ckv_ref, kr_ref, cc_ref, ss_ref, gq_ref, wq_ref, gkv_ref,
                      wk_ref, wv_ref, q_ref, k_ref, v_ref, kv_ref, kro_ref, *, transposed):
    cck = cc_ref[...]
    ss = ss_ref[...]
    lane = lax.broadcasted_iota(jnp.int32, cck.shape, 1)
    ccq = jnp.where(lane < QK_NOPE, 1.0, cck)
    qn = _rmsnorm2(cq_ref[...], gq_ref[...]).astype(BF16)
    ckv = _rmsnorm2(ckv_ref[...], gkv_ref[...])
    kv_ref[...] = ckv
    ckv_b = ckv.astype(BF16)
    kr = _rope_group(kr_ref[...], cck, ss)
    kro_ref[...] = kr
    v = _dot(ckv_b, wv_ref[...])
    rows = v.shape[0]
    if transposed:
        ones = jnp.ones((V_ROWS - V_HEAD, rows), F32)
        vt = v.T
        for h in range(N_HEADS):
            v_ref[h * V_ROWS:(h + 1) * V_ROWS, :] = jnp.concatenate(
                [vt[h * V_HEAD:(h + 1) * V_HEAD, :], ones], axis=0).astype(BF16)
    else:
        v_ref[...] = v.astype(BF16)
    for h in range(N_HEADS):
        sl = slice(h * HEAD_W, (h + 1) * HEAD_W)
        zq = _dot(qn, wq_ref[:, sl])
        qh = _rope_group(zq, ccq, ss) * Q_SCALE
        if transposed:
            q_ref[sl, :] = qh.T.astype(BF16)
        else:
            q_ref[:, sl] = qh.astype(BF16)
        k_ref[:, sl] = (_dot(ckv_b, wk_ref[:, sl]) + kr).astype(BF16)


def _attn_prep(z, cc, ss, gq, wq, gkv, wk, wv, rows, transposed):
    n = z.shape[0]
    npos = cc.shape[0] // rows
    row = lambda i: (i, 0)
    col = lambda i: (0, i)
    const = lambda i: (0, 0)
    if transposed:
        q_spec = pl.BlockSpec((N_HEADS * HEAD_W, rows), col)
        q_shape = jax.ShapeDtypeStruct((N_HEADS * HEAD_W, n), BF16)
        v_spec = pl.BlockSpec((N_HEADS * V_ROWS, rows), col)
        v_shape = jax.ShapeDtypeStruct((N_HEADS * V_ROWS, n), BF16)
    else:
        q_spec = pl.BlockSpec((rows, N_HEADS * HEAD_W), row)
        q_shape = jax.ShapeDtypeStruct((n, N_HEADS * HEAD_W), BF16)
        v_spec = pl.BlockSpec((rows, N_HEADS * V_HEAD), row)
        v_shape = jax.ShapeDtypeStruct((n, N_HEADS * V_HEAD), BF16)
    return pl.pallas_call(
        functools.partial(_attn_prep_kernel, transposed=transposed),
        grid=(n // rows,),
        in_specs=[
            pl.BlockSpec((rows, Q_LORA), lambda i: (i, COL_CQ // Q_LORA)),
            pl.BlockSpec((rows, KV_LORA), lambda i: (i, COL_CKV // KV_LORA)),
            pl.BlockSpec((rows, HEAD_W), lambda i: (i, COL_KR // HEAD_W)),
            pl.BlockSpec((rows, HEAD_W), lambda i: (i % npos, 0)),
            pl.BlockSpec((rows, HEAD_W), lambda i: (i % npos, 0)),
            pl.BlockSpec((1, Q_LORA), const),
            pl.BlockSpec(wq.shape, const),
            pl.BlockSpec((1, KV_LORA), const),
            pl.BlockSpec(wk.shape, const),
            pl.BlockSpec(wv.shape, const),
        ],
        out_specs=[
            q_spec,
            pl.BlockSpec((rows, N_HEADS * HEAD_W), row),
            v_spec,
            pl.BlockSpec((rows, KV_LORA), row),
            pl.BlockSpec((rows, HEAD_W), row),
        ],
        out_shape=[
            q_shape,
            jax.ShapeDtypeStruct((n, N_HEADS * HEAD_W), BF16),
            v_shape,
            jax.ShapeDtypeStruct((n, KV_LORA), F32),
            jax.ShapeDtypeStruct((n, HEAD_W), F32),
        ],
        compiler_params=_cparams(("arbitrary",)),
        name="attn_prep",
    )(z, z, z, cc, ss, gq, wq, gkv, wk, wv)


def _cache_expand_kernel(ckv_ref, kr_ref, wk_ref, wv_ref, place_ref, k_ref, v_ref):
    ckv_b = ckv_ref[...].astype(BF16)
    krp = _dot(kr_ref[...].astype(BF16), place_ref[...])
    v_ref[...] = _dot(ckv_b, wv_ref[...]).astype(BF16)
    for h in range(N_HEADS):
        sl = slice(h * HEAD_W, (h + 1) * HEAD_W)
        k_ref[:, sl] = (_dot(ckv_b, wk_ref[:, sl]) + krp).astype(BF16)


def _cache_expand(ckv, kr, wk, wv, place, rows):
    n = ckv.shape[0]
    row = lambda i: (i, 0)
    const = lambda i: (0, 0)
    return pl.pallas_call(
        _cache_expand_kernel,
        grid=(n // rows,),
        in_specs=[
            pl.BlockSpec((rows, KV_LORA), row),
            pl.BlockSpec((rows, QK_ROPE), row),
            pl.BlockSpec(wk.shape, const),
            pl.BlockSpec(wv.shape, const),
            pl.BlockSpec(place.shape, const),
        ],
        out_specs=[
            pl.BlockSpec((rows, N_HEADS * HEAD_W), row),
            pl.BlockSpec((rows, N_HEADS * V_HEAD), row),
        ],
        out_shape=[
            jax.ShapeDtypeStruct((n, N_HEADS * HEAD_W), BF16),
            jax.ShapeDtypeStruct((n, N_HEADS * V_HEAD), BF16),
        ],
        compiler_params=_cparams(("arbitrary",)),
        name="cache_expand",
    )(ckv, kr, wk, wv, place)


def _flash_step(qt_ref, k_ref, vt_ref, m_ref, acc_ref, masked, q0, k0):
    tk, tq = k_ref.shape[0], qt_ref.shape[1]
    if masked:
        kc = (k0 + lax.broadcasted_iota(jnp.int32, (tk, tq), 0)) // CHUNK
        qc = (q0 + lax.broadcasted_iota(jnp.int32, (tk, tq), 1)) // CHUNK
        keep = kc <= qc
    def scores(h):
        sl = slice(h * HEAD_W, (h + 1) * HEAD_W)
        s = _dot(k_ref[:, sl], qt_ref[sl, :])
        return jnp.where(keep, s, NEG_INF) if masked else s

    def stats(h, s):
        m_prev = m_ref[h]
        m_new = jnp.maximum(m_prev, jnp.max(s, axis=0, keepdims=True))
        m_ref[h] = m_new
        return m_new, jnp.exp2(m_prev - m_new)

    s = {0: scores(0)}
    if N_HEADS > 1:
        s[1] = scores(1)
    st = {0: stats(0, s[0])}
    for h in range(N_HEADS):
        vs = slice(h * V_ROWS, (h + 1) * V_ROWS)
        if h + 2 < N_HEADS:
            s[h + 2] = scores(h + 2)
        if h + 1 < N_HEADS:
            st[h + 1] = stats(h + 1, s[h + 1])
        m_new, alpha = st.pop(h)
        p = jnp.exp2(s.pop(h) - m_new).astype(BF16)
        acc_ref[h] = alpha * acc_ref[h] + _dot(vt_ref[vs, :], p)


def _flash_kernel(qt_ref, k_ref, vt_ref, o_ref, m_ref, acc_ref, *, ratio):
    qi = pl.program_id(1)
    ki = pl.program_id(2)
    tk, tq = k_ref.shape[0], qt_ref.shape[1]

    @pl.when(ki == 0)
    def _():
        m_ref[...] = jnp.full(m_ref.shape, NEG_INF, F32)
        acc_ref[...] = jnp.zeros(acc_ref.shape, F32)

    @pl.when(ki < qi * ratio)
    def _():
        _flash_step(qt_ref, k_ref, vt_ref, m_ref, acc_ref, False, 0, 0)

    @pl.when(jnp.logical_and(ki >= qi * ratio, ki < (qi + 1) * ratio))
    def _():
        _flash_step(qt_ref, k_ref, vt_ref, m_ref, acc_ref, True, qi * tq, ki * tk)

    @pl.when(ki == (qi + 1) * ratio - 1)
    def _():
        for h in range(N_HEADS):
            a = acc_ref[h]
            o = a[:V_HEAD, :] / a[V_HEAD:V_HEAD + 1, :]
            o_ref[:, h * V_HEAD:(h + 1) * V_HEAD] = o.T.astype(o_ref.dtype)


def _flash_prompt(qt, k, vt, b, t, tq, tk):
    nq, nk = t // tq, t // tk
    ratio = tq // tk
    last = lambda qi, ki: jnp.minimum(ki, (qi + 1) * ratio - 1)
    return pl.pallas_call(
        functools.partial(_flash_kernel, ratio=ratio),
        grid=(b, nq, nk),
        in_specs=[
            pl.BlockSpec((N_HEADS * HEAD_W, tq), lambda bi, qi, ki: (0, bi * nq + qi)),
            pl.BlockSpec((tk, N_HEADS * HEAD_W), lambda bi, qi, ki: (bi * nk + last(qi, ki), 0)),
            pl.BlockSpec((N_HEADS * V_ROWS, tk), lambda bi, qi, ki: (0, bi * nk + last(qi, ki))),
        ],
        out_specs=pl.BlockSpec((tq, N_HEADS * V_HEAD), lambda bi, qi, ki: (bi * nq + qi, 0)),
        out_shape=jax.ShapeDtypeStruct((b * t, N_HEADS * V_HEAD), BF16),
        scratch_shapes=[
            pltpu.VMEM((N_HEADS, 1, tq), F32),
            pltpu.VMEM((N_HEADS, V_ROWS, tq), F32),
        ],
        compiler_params=_cparams(("arbitrary", "arbitrary", "arbitrary")),
        name="flash_prompt",
    )(qt, k, vt)


def _attn_sample_kernel(q_ref, kc_ref, vc_ref, kn_ref, vn_ref, o_ref):
    for h in range(N_HEADS):
        sl = slice(h * HEAD_W, (h + 1) * HEAD_W)
        vs = slice(h * V_HEAD, (h + 1) * V_HEAD)
        qh = q_ref[:, sl]
        sc = _dot_nt(qh, kc_ref[:, sl])
        sn = _dot_nt(qh, kn_ref[:, sl])
        m = jnp.maximum(jnp.max(sc, axis=1, keepdims=True), jnp.max(sn, axis=1, keepdims=True))
        pc = jnp.exp2(sc - m)
        pn = jnp.exp2(sn - m)
        den = jnp.sum(pc, axis=1, keepdims=True) + jnp.sum(pn, axis=1, keepdims=True)
        o = _dot(pc.astype(BF16), vc_ref[:, vs]) + _dot(pn.astype(BF16), vn_ref[:, vs])
        o_ref[:, vs] = (o / den).astype(o_ref.dtype)


def _attn_sample(q, kc, vc, kn, vn, b, t, past):
    return pl.pallas_call(
        _attn_sample_kernel,
        grid=(b,),
        in_specs=[
            pl.BlockSpec((t, N_HEADS * HEAD_W), lambda i: (i, 0)),
            pl.BlockSpec((past, N_HEADS * HEAD_W), lambda i: (i, 0)),
            pl.BlockSpec((past, N_HEADS * V_HEAD), lambda i: (i, 0)),
            pl.BlockSpec((t, N_HEADS * HEAD_W), lambda i: (i, 0)),
            pl.BlockSpec((t, N_HEADS * V_HEAD), lambda i: (i, 0)),
        ],
        out_specs=pl.BlockSpec((t, N_HEADS * V_HEAD), lambda i: (i, 0)),
        out_shape=jax.ShapeDtypeStruct((b * t, N_HEADS * V_HEAD), BF16),
        compiler_params=_cparams(("arbitrary",)),
        name="attn_sample",
    )(q, kc, vc, kn, vn)


def _mix_kernel(x_ref, gt_ref, ubch_ref, g_ref, attn_ref, pu_ref, pch_ref, hp_ref, hc_ref,
                wpool_ref, pscale_ref, convw_ref, wbr_ref, wout_ref,
                xo_ref, ptail_ref, ctail_ref, extp_ref, extc_ref, *, n_hist):
    gg, tg, d = x_ref.shape
    j = pl.program_id(1)
    first = j == 0

    ubch = ubch_ref[...]
    u = ubch[:, COL_U:COL_U + POOL_W].reshape(gg, tg, POOL_W)
    bgate = ubch[:, COL_B:COL_B + CONV_W].reshape(gg, tg, CONV_W)
    cu = (ubch[:, COL_C:COL_C + CONV_W] * ubch[:, COL_H:COL_H + CONV_W]).reshape(gg, tg, CONV_W)

    hist_p = jnp.where(first, hp_ref[...], pu_ref[...].reshape(1, HIST_ROWS, POOL_W))
    pch = pch_ref[...]
    prev_cu = (pch[:, :CONV_W] * pch[:, CONV_W:])[HIST_ROWS - CONV_HIST_ROWS:]
    hist_c = jnp.where(first, hc_ref[...], prev_cu.reshape(1, CONV_HIST_ROWS, CONV_W))

    extp_ref[:, :HIST_ROWS, :] = hist_p
    extp_ref[:, HIST_ROWS:, :] = u
    extc_ref[:, :CONV_HIST_ROWS, :] = hist_c
    extc_ref[:, CONV_HIST_ROWS:, :] = cu
    ptail_ref[...] = u[:, tg - HIST_ROWS:, :]
    ctail_ref[...] = cu[:, tg - CONV_HIST_ROWS:, :]

    tpos = j * tg + lax.broadcasted_iota(jnp.int32, (1, tg, 1), 1)
    pooled = []
    for gi, w in enumerate(POOL_WINDOWS):
        cs = slice(gi * POOL_GROUP, (gi + 1) * POOL_GROUP)
        acc = u[:, :, cs]
        for kk in range(1, w):
            acc = acc + extp_ref[:, HIST_ROWS - kk:HIST_ROWS - kk + tg, cs]
        cnt = jnp.minimum(tpos + 1 + n_hist, w).astype(F32)
        dd = acc / cnt - u[:, :, cs]
        pooled.append(_dot(dd.reshape(gg * tg, POOL_GROUP).astype(BF16), wpool_ref[gi]))
    pool = jnp.concatenate(pooled, axis=1) * pscale_ref[...]

    cw = convw_ref[...]
    yc = (cw[0:1, :] * extc_ref[:, CONV_HIST_ROWS - 2:CONV_HIST_ROWS - 2 + tg, :]
          + cw[1:2, :] * extc_ref[:, CONV_HIST_ROWS - 1:CONV_HIST_ROWS - 1 + tg, :]
          + cw[2:3, :] * cu)
    conv = (bgate * yc).reshape(gg * tg, CONV_W)

    gates = jax.nn.sigmoid(g_ref[...])
    mixed = (gates[:, 0:d] * _dot(attn_ref[...], wbr_ref[0])
             + gates[:, d:2 * d] * _dot(pool.astype(BF16), wbr_ref[1])
             + gates[:, 2 * d:3 * d] * _dot(conv.astype(BF16), wbr_ref[2]))
    out = _dot(mixed.astype(BF16), wout_ref[...])
    xo_ref[...] = x_ref[...] + gt_ref[...] * out.reshape(gg, tg, d)


def _mix(x3, gt, z, attn, hist_p, hist_c, wpool, pscale, convw, wbr, wout, rows, n_hist):
    b, t, d = x3.shape
    gg, tg = _row_plan(b, t, rows)
    nj = t // tg
    rb = lambda i, j: i * nj + j
    hpb = tg // HIST_ROWS

    def prev_rows(i, j):
        return jnp.maximum(rb(i, j) * hpb - 1, 0)

    const2 = lambda i, j: (0, 0)
    const3 = lambda i, j: (0, 0, 0)
    return pl.pallas_call(
        functools.partial(_mix_kernel, n_hist=n_hist),
        grid=(b // gg, nj),
        in_specs=[
            pl.BlockSpec((gg, tg, d), lambda i, j: (i, j, 0)),
            pl.BlockSpec((gg, 1, d), lambda i, j: (i, 0, 0)),
            pl.BlockSpec((gg * tg, COL_CQ), lambda i, j: (rb(i, j), 0)),
            pl.BlockSpec((gg * tg, 3 * d), lambda i, j: (rb(i, j), COL_G // (3 * d))),
            pl.BlockSpec((gg * tg, N_HEADS * V_HEAD), lambda i, j: (rb(i, j), 0)),
            pl.BlockSpec((HIST_ROWS, POOL_W), lambda i, j: (prev_rows(i, j), 0)),
            pl.BlockSpec((HIST_ROWS, 2 * CONV_W), lambda i, j: (prev_rows(i, j), COL_C // (2 * CONV_W))),
            pl.BlockSpec((gg, HIST_ROWS, POOL_W), lambda i, j: (i, 0, 0)),
            pl.BlockSpec((gg, CONV_HIST_ROWS, CONV_W), lambda i, j: (i, 0, 0)),
            pl.BlockSpec(wpool.shape, const3),
            pl.BlockSpec(pscale.shape, const2),
            pl.BlockSpec(convw.shape, const2),
            pl.BlockSpec(wbr.shape, const3),
            pl.BlockSpec(wout.shape, const2),
        ],
        out_specs=[
            pl.BlockSpec((gg, tg, d), lambda i, j: (i, j, 0)),
            pl.BlockSpec((gg, HIST_ROWS, POOL_W), lambda i, j: (i, 0, 0)),
            pl.BlockSpec((gg, CONV_HIST_ROWS, CONV_W), lambda i, j: (i, 0, 0)),
        ],
        out_shape=[
            jax.ShapeDtypeStruct((b, t, d), F32),
            jax.ShapeDtypeStruct((b, HIST_ROWS, POOL_W), F32),
            jax.ShapeDtypeStruct((b, CONV_HIST_ROWS, CONV_W), F32),
        ],
        scratch_shapes=[
            pltpu.VMEM((gg, HIST_ROWS + tg, POOL_W), F32),
            pltpu.VMEM((gg, CONV_HIST_ROWS + tg, CONV_W), F32),
        ],
        compiler_params=_cparams(("arbitrary", "arbitrary")),
        name="mix_merge",
    )(x3, gt, z, z, attn, z, z, hist_p, hist_c, wpool, pscale, convw, wbr, wout)


def _peer_query_kernel(x_ref, g_ref, sh_ref, sc_ref, wq_ref, keys_ref, ht_ref, s1_ref, s2_ref):
    gg, tg, d = x_ref.shape
    h = _modnorm(x_ref[...], g_ref[...], sh_ref[...], sc_ref[...]).reshape(gg * tg, d)
    ht_ref[0] = h.T.astype(BF16)
    hb = h.astype(BF16)
    half = D_KEY // 2
    k1 = keys_ref[0]
    k2 = keys_ref[1]
    for hh in range(PEER_HEADS):
        q = _dot(hb, wq_ref[:, hh * D_KEY:(hh + 1) * D_KEY]).astype(BF16)
        s1_ref[hh, 0] = _dot_nt(k1, q[:, :half])
        s2_ref[hh, 0] = _dot_nt(k2, q[:, half:])


def _peer_query(x3, g, sh, sc, wq, keys, rows):
    b, t, d = x3.shape
    gg, tg = _row_plan(b, t, rows)
    nj = t // tg
    nt = b * t // rows
    tok = lambda i, j: (0, i * nj + j, 0, 0)
    return pl.pallas_call(
        _peer_query_kernel,
        grid=(b // gg, nj),
        in_specs=[
            pl.BlockSpec((gg, tg, d), lambda i, j: (i, j, 0)),
            pl.BlockSpec((1, d), lambda i, j: (0, 0)),
            pl.BlockSpec((gg, 1, d), lambda i, j: (i, 0, 0)),
            pl.BlockSpec((gg, 1, d), lambda i, j: (i, 0, 0)),
            pl.BlockSpec(wq.shape, lambda i, j: (0, 0)),
            pl.BlockSpec(keys.shape, lambda i, j: (0, 0, 0)),
        ],
        out_specs=[
            pl.BlockSpec((1, d, rows), lambda i, j: (i * nj + j, 0, 0)),
            pl.BlockSpec((PEER_HEADS, 1, N_KEYS, rows), tok),
            pl.BlockSpec((PEER_HEADS, 1, N_KEYS, rows), tok),
        ],
        out_shape=[
            jax.ShapeDtypeStruct((nt, d, rows), BF16),
            jax.ShapeDtypeStruct((PEER_HEADS, nt, N_KEYS, rows), F32),
            jax.ShapeDtypeStruct((PEER_HEADS, nt, N_KEYS, rows), F32),
        ],
        compiler_params=_cparams(("arbitrary", "arbitrary")),
        name="peer_query",
    )(x3, g, sh, sc, wq, keys)


def _top_extract(s, with_rank):
    nk, r = s.shape
    iota = lax.broadcasted_iota(jnp.int32, (nk, r), 0).astype(F32)
    iota_k = lax.broadcasted_iota(jnp.int32, (PEER_TOPK, r), 0)
    rank = jnp.full((nk, r), NOT_TOP, F32) if with_rank else None
    vals = jnp.zeros((PEER_TOPK, r), F32)
    idxs = jnp.zeros((PEER_TOPK, r), F32)
    x = s
    for it in range(PEER_TOPK):
        m = jnp.max(x, axis=0, keepdims=True)
        idx = jnp.min(jnp.where(x == m, iota, float(nk)), axis=0, keepdims=True)
        hit = iota == idx
        if with_rank:
            rank = jnp.where(hit, float(it), rank)
        x = jnp.where(hit, -jnp.inf, x)
        vals = jnp.where(iota_k == it, m, vals)
        idxs = jnp.where(iota_k == it, idx, idxs)
    return rank, vals, idxs


def _peer_select_kernel(s1_ref, s2_ref, c_ref, lr_ref, rk2_ref, d_ref):
    for h in range(s1_ref.shape[0]):
        _peer_select_head(h, s1_ref, s2_ref, c_ref, lr_ref, rk2_ref, d_ref)


def _peer_select_head(h, s1_ref, s2_ref, c_ref, lr_ref, rk2_ref, d_ref):
    s1 = s1_ref[h, 0]
    s2 = s2_ref[h, 0]
    nk, r = s1.shape
    _, v1, idx1 = _top_extract(s1, False)
    rank2, v2, _ = _top_extract(s2, True)
    iota_k = lax.broadcasted_iota(jnp.int32, (PEER_TOPK, r), 0).astype(F32)
    cnt = jnp.zeros((PEER_TOPK, r), F32)
    front = v1 + v2[0:1, :]
    top = front[0:1, :]
    zsum = jnp.zeros((1, r), F32)
    for _ in range(PEER_TOPK):
        m = jnp.max(front, axis=0, keepdims=True)
        a = jnp.min(jnp.where(front == m, iota_k, float(PEER_TOPK)), axis=0, keepdims=True)
        hit = iota_k == a
        zsum = zsum + jnp.exp(m - top)
        cnt = jnp.where(hit, cnt + 1.0, cnt)
        c_hit = jnp.max(jnp.where(hit, cnt, 0.0), axis=0, keepdims=True)
        v1_hit = jnp.max(jnp.where(hit, v1, -jnp.inf), axis=0, keepdims=True)
        nxt = jnp.full((1, r), -jnp.inf, F32)
        for bcol in range(1, PEER_TOPK):
            nxt = jnp.where(c_hit == float(bcol), v2[bcol:bcol + 1, :], nxt)
        front = jnp.where(hit, v1_hit + nxt, front)
    iota = lax.broadcasted_iota(jnp.int32, (nk, r), 0).astype(F32)
    lr = jnp.zeros((nk, r), F32)
    for a in range(PEER_TOPK):
        lr = jnp.where(iota == idx1[a:a + 1, :], cnt[a:a + 1, :], lr)
    c_ref[h, 0] = jnp.exp(s1 - v1[0:1, :]) / zsum
    lr_ref[h, 0] = lr
    rk2_ref[h, 0] = rank2.astype(BF16)
    d_ref[h, 0] = jnp.exp(s2 - v2[0:1, :]).astype(BF16)


def _peer_select(s1, s2):
    hh, nt, nk, lanes = s1.shape
    heads_per_step = 4
    spec = pl.BlockSpec((heads_per_step, 1, nk, lanes), lambda i, h: (h, i, 0, 0))
    f32 = jax.ShapeDtypeStruct(s1.shape, F32)
    b16 = jax.ShapeDtypeStruct(s1.shape, BF16)
    return pl.pallas_call(
        _peer_select_kernel,
        grid=(nt, hh // heads_per_step),
        in_specs=[spec, spec],
        out_specs=[spec, spec, spec, spec],
        out_shape=[f32, f32, b16, b16],
        compiler_params=_cparams(("arbitrary", "arbitrary")),
        name="peer_select",
    )(s1, s2)


def _gelu(x):
    return 0.5 * x * (1.0 + lax.erf(x * (2.0 ** -0.5)))


def _peer_dense_kernel(x_ref, gt_ref, ht_ref, u_ref, vt_ref, c_ref, lr_ref, rk2_ref, d_ref,
                       xo_ref, a_ref, w_ref, acc_ref, *, rows_per_chunk):
    gg, tg, d = x_ref.shape
    nt, _, lanes = ht_ref.shape
    e = pl.program_id(1)
    slab = 16
    n_slab = N_KEYS // slab

    gsz = acc_ref.shape[2] // lanes
    ng = nt // gsz

    @pl.when(e == 0)
    def _():
        acc_ref[...] = jnp.zeros(acc_ref.shape, F32)

    w_ref[(ng - 1) % 2] = jnp.zeros(w_ref.shape[1:], BF16)

    def group(g):
        prev = (g + ng - 1) % ng
        rhs = jnp.concatenate([ht_ref[g * gsz + k] for k in range(gsz)], axis=1)
        w_prev = w_ref.at[prev % 2]
        w_cur = w_ref.at[g % 2]
        ec = a_ref.shape[0]
        n_part = 1
        pe, pd = ec // n_part, d // n_part
        ii_per_part = rows_per_chunk // n_part

        def key_part(q):
            a_ref[q * pe:(q + 1) * pe, :] = _dot(u_ref[q * pe:(q + 1) * pe, :], rhs)

        def value_part(q):
            acc_ref[prev, q * pd:(q + 1) * pd, :] += _dot(vt_ref[q * pd:(q + 1) * pd, :], w_prev[...])

        key_part(0)
        for q in range(n_part):
            if q + 1 < n_part:
                key_part(q + 1)
            value_part(q)
            for k in range(gsz):
                lt = g * gsz + k
                ls = slice(k * lanes, (k + 1) * lanes)
                for ii in range(q * ii_per_part, (q + 1) * ii_per_part):
                    cl = []
                    for hh in range(PEER_HEADS):
                        cl.append((
                            jnp.broadcast_to(c_ref[hh, lt, ii:ii + 1, :], (slab, lanes)).astype(BF16),
                            jnp.broadcast_to(lr_ref[hh, lt, ii:ii + 1, :], (slab, lanes)).astype(BF16)))
                    for jv in range(n_slab):
                        rs = slice(jv * slab, (jv + 1) * slab)
                        gate = None
                        for hh in range(PEER_HEADS):
                            dd = d_ref[hh, lt, rs, :]
                            term = cl[hh][0] * jnp.where(rk2_ref[hh, lt, rs, :] < cl[hh][1], dd,
                                                         jnp.zeros_like(dd))
                            gate = term if gate is None else gate + term
                        rows = slice(ii * N_KEYS + jv * slab, ii * N_KEYS + (jv + 1) * slab)
                        w_cur[rows, ls] = _gelu(a_ref[rows, ls]).astype(BF16) * gate

    for g in range(ng):
        group(g)
    acc_ref[ng - 1] += _dot(vt_ref[...], w_ref[(ng - 1) % 2])

    @pl.when(e == pl.num_programs(1) - 1)
    def _():
        for g in range(ng):
            for k in range(gsz):
                lt = g * gsz + k
                upd = acc_ref[g, :, k * lanes:(k + 1) * lanes].T
                if gg == 1:
                    rows = slice(lt * lanes, (lt + 1) * lanes)
                    xo_ref[0, rows, :] = x_ref[0, rows, :] + gt_ref[0] * upd
                else:
                    per = lanes // tg
                    rows = slice(lt * per, (lt + 1) * per)
                    xo_ref[rows] = x_ref[rows] + gt_ref[rows] * upd.reshape(per, tg, d)


def _peer_dense(x3, gt, ht, u, vt, c, lr, rk2, dd, tiles_per_block, rows_per_chunk):
    b, t, d = x3.shape
    nt_all, _, lanes = ht.shape
    gg, tg = _row_plan(b, t, tiles_per_block * lanes)
    nj = t // tg
    ne = u.shape[0]
    ec = rows_per_chunk * N_KEYS
    gsz = 1
    full = pl.BlockSpec((PEER_HEADS, tiles_per_block, N_KEYS, lanes), lambda i, e: (0, i, 0, 0))
    part = pl.BlockSpec((PEER_HEADS, tiles_per_block, rows_per_chunk, lanes),
                        lambda i, e: (0, i, e, 0))
    return pl.pallas_call(
        functools.partial(_peer_dense_kernel, rows_per_chunk=rows_per_chunk),
        grid=(nt_all // tiles_per_block, ne // ec),
        in_specs=[
            pl.BlockSpec((gg, tg, d), lambda i, e: (i // nj, i % nj, 0)),
            pl.BlockSpec((gg, 1, d), lambda i, e: (i // nj, 0, 0)),
            pl.BlockSpec((tiles_per_block, d, lanes), lambda i, e: (i, 0, 0)),
            pl.BlockSpec((ec, d), lambda i, e: (e, 0)),
            pl.BlockSpec((d, ec), lambda i, e: (0, e)),
            part, part, full, full,
        ],
        out_specs=pl.BlockSpec((gg, tg, d), lambda i, e: (i // nj, i % nj, 0)),
        out_shape=jax.ShapeDtypeStruct((b, t, d), F32),
        scratch_shapes=[
            pltpu.VMEM((ec, gsz * lanes), F32),
            pltpu.VMEM((2, ec, gsz * lanes), BF16),
            pltpu.VMEM((tiles_per_block // gsz, d, gsz * lanes), F32),
        ],
        compiler_params=_cparams(("arbitrary", "arbitrary")),
        name="peer_dense",
    )(x3, gt, ht, u, vt, c, lr, rk2, dd)


def _final_kernel(x_ref, g_ref, o_ref):
    o_ref[...] = _rmsnorm2(x_ref[...], g_ref[...])


def _final_norm(x3, g, rows):
    b, t, d = x3.shape
    x2 = x3.reshape(b * t, d)
    y = pl.pallas_call(
        _final_kernel,
        grid=(b * t // rows,),
        in_specs=[pl.BlockSpec((rows, d), lambda i: (i, 0)), pl.BlockSpec((1, d), lambda i: (0, 0))],
        out_specs=pl.BlockSpec((rows, d), lambda i: (i, 0)),
        out_shape=jax.ShapeDtypeStruct((b * t, d), F32),
        compiler_params=_cparams(("arbitrary",)),
        name="final_norm",
    )(x2, g)
    return y.reshape(b, t, d)


def _swap_halves(w):
    half = w.shape[-1] // 2
    return jnp.concatenate([w[..., half:], w[..., :half]], axis=-1)


def _prep_layer(p, l):
    d = p['w_in'].shape[1]
    w_in = p['w_in'][l]
    offs = [0]
    for nsz in (Q_LORA, KV_LORA, QK_ROPE, POOL_W, CONV_W, CONV_W, CONV_W, 3 * d):
        offs.append(offs[-1] + nsz)
    w_cq, w_ckv, w_kr, w_u, w_b, w_c, w_h, w_g = [w_in[:, offs[i]:offs[i + 1]] for i in range(8)]
    zeros = lambda n: jnp.zeros((d, n), w_in.dtype)
    w_krg = jnp.concatenate([zeros(QK_NOPE), w_kr, _swap_halves(w_kr)], axis=1)
    w_in_r = jnp.concatenate(
        [w_u, w_b, w_c, w_h, w_cq, w_ckv, w_krg, zeros(COL_G - COL_KR - HEAD_W), w_g], axis=1)
    assert w_in_r.shape[1] == D_IN_PAD

    w_uq = p['w_uq'][l]
    wq_r = jnp.concatenate(
        [w_uq[..., :QK_NOPE], w_uq[..., QK_NOPE:], _swap_halves(w_uq[..., QK_NOPE:])], axis=-1)
    wq_r = wq_r.reshape(Q_LORA, N_HEADS * HEAD_W)
    w_ukv = p['w_ukv'][l]
    wk_r = jnp.concatenate(
        [w_ukv[..., :QK_NOPE], jnp.zeros((KV_LORA, N_HEADS, HEAD_W - QK_NOPE), w_ukv.dtype)],
        axis=-1).reshape(KV_LORA, N_HEADS * HEAD_W)
    wv_r = w_ukv[..., QK_NOPE:].reshape(KV_LORA, N_HEADS * V_HEAD)
    return dict(
        w_in=w_in_r.astype(BF16), wq=wq_r.astype(BF16), wk=wk_r.astype(BF16), wv=wv_r.astype(BF16),
        g_mix=p['g_mix'][l][None, :], g_q=p['g_q'][l][None, :], g_kv=p['g_kv'][l][None, :],
        w_pool=p['w_pool'][l].astype(BF16), pool_scale=p['pool_scale'][l][None, :],
        conv_w=jnp.pad(p['conv_w'][l], ((0, 8 - CONV_K), (0, 0))),
        w_branch=p['w_branch'][l].astype(BF16), w_out=p['w_out'][l].astype(BF16),
        g_ffn=p['g_ffn'][l][None, :],
        peer_wq=p['peer_wq'][l].reshape(d, PEER_HEADS * D_KEY).astype(BF16),
        peer_keys=p['peer_keys'][l].astype(BF16),
        peer_u=p['peer_u'][l].astype(BF16),
        peer_vt=p['peer_v'][l].T.astype(BF16),
    )


def _rope_tables(pos):
    half = QK_ROPE // 2
    inv = ROPE_THETA ** (-jnp.arange(half, dtype=F32) / half)
    ang = pos.astype(F32)[:, None] * inv[None, :]
    cos, sin = jnp.cos(ang), jnp.sin(ang)
    z = lambda n: jnp.zeros((pos.shape[0], n), F32)
    cc = jnp.concatenate([z(QK_NOPE), cos, cos, z(QK_ROPE)], axis=1)
    ss = jnp.concatenate([z(QK_NOPE), -sin, sin, z(QK_ROPE)], axis=1)
    return cc, ss


def _trunk(x, mods, pos, n_hist, cache, hist_pool, hist_conv, layers, g_final, cfg):
    b, t, d = x.shape
    rows = cfg['rows']
    n = b * t
    cc, ss = _rope_tables(pos)
    if t < rows:
        cc = jnp.tile(cc, (rows // t, 1))
        ss = jnp.tile(ss, (rows // t, 1))
    place = jnp.concatenate(
        [jnp.zeros((QK_ROPE, QK_NOPE), F32), jnp.eye(QK_ROPE, dtype=F32),
         jnp.zeros((QK_ROPE, HEAD_W - QK_NOPE - QK_ROPE), F32)], axis=1).astype(BF16)
    new_kv, new_kr, new_pool, new_conv = [], [], [], []
    for l, lw in enumerate(layers):
        mod = mods[l].reshape(b, 1, 6 * d)
        sh1, sc1, gt1, sh2, sc2, gt2 = [mod[:, :, i * d:(i + 1) * d] for i in range(6)]
        z = _inproj(x, lw['g_mix'], sh1, sc1, lw['w_in'], rows)
        q, k, v, ckv, krg = _attn_prep(z, cc, ss, lw['g_q'], lw['wq'], lw['g_kv'], lw['wk'],
                                       lw['wv'], rows, cache is None)
        new_kv.append(ckv.reshape(b, t, KV_LORA))
        new_kr.append(krg[:, QK_NOPE:QK_NOPE + QK_ROPE].reshape(b, t, QK_ROPE))
        if cache is None:
            attn = _flash_prompt(q, k, v, b, t, cfg['attn_tq'], cfg['attn_tk'])
        else:
            ckv_c, kr_c = cache
            past = ckv_c.shape[2]
            kc, vc = _cache_expand(ckv_c[l].reshape(b * past, KV_LORA),
                                   kr_c[l].reshape(b * past, QK_ROPE),
                                   lw['wk'], lw['wv'], place, cfg['cache_rows'])
            attn = _attn_sample(q, kc, vc, k, v, b, t, past)
        hp = jnp.pad(hist_pool[l], ((0, 0), (HIST_ROWS - POOL_HIST, 0), (0, 0)))
        hc = jnp.pad(hist_conv[l], ((0, 0), (CONV_HIST_ROWS - (CONV_K - 1), 0), (0, 0)))
        x, ptail, ctail = _mix(x, gt1, z, attn, hp, hc, lw['w_pool'], lw['pool_scale'],
                               lw['conv_w'], lw['w_branch'], lw['w_out'], rows, n_hist)
        new_pool.append(ptail[:, HIST_ROWS - POOL_HIST:, :])
        new_conv.append(ctail[:, CONV_HIST_ROWS - (CONV_K - 1):, :])
        ht, s1, s2 = _peer_query(x, lw['g_ffn'], sh2, sc2, lw['peer_wq'], lw['peer_keys'], rows)
        c, lr, rk2, dd = _peer_select(s1, s2)
        x = _peer_dense(x, gt2, ht, lw['peer_u'], lw['peer_vt'], c, lr, rk2, dd,
                        min(cfg['peer_tiles'], n // rows), cfg['rows_per_chunk'])
    y = _final_norm(x, g_final[None, :], rows)
    return y, jnp.stack(new_kv), jnp.stack(new_kr), jnp.stack(new_pool), jnp.stack(new_conv)


def _config(t_prompt):
    rows = min(256, t_prompt)
    return dict(rows=rows, attn_tq=min(512, t_prompt), attn_tk=min(256, t_prompt), cache_rows=256, peer_tiles=4,
                rows_per_chunk=8)


def kernel(x_prompt, x_sample, cache_kv_latent, cache_k_rope, state_pool, state_conv,
           c_prompt, c_sample, w_ada, b_ada, g_mix, w_in, g_q, w_uq, g_kv, w_ukv,
           w_pool, pool_scale, conv_w, w_branch, w_out, g_ffn, peer_wq, peer_keys,
           peer_u, peer_v, g_final):
    p = {'w_in': w_in, 'g_mix': g_mix, 'g_q': g_q, 'w_uq': w_uq, 'g_kv': g_kv, 'w_ukv': w_ukv,
         'w_pool': w_pool, 'pool_scale': pool_scale, 'conv_w': conv_w, 'w_branch': w_branch,
         'w_out': w_out, 'g_ffn': g_ffn, 'peer_wq': peer_wq, 'peer_keys': peer_keys,
         'peer_u': peer_u, 'peer_v': peer_v}
    depth = w_ada.shape[0]
    bp, tp, d = x_prompt.shape
    bs, ts, _ = x_sample.shape
    past = cache_kv_latent.shape[2]
    layers = [_prep_layer(p, l) for l in range(depth)]

    c_all = jnp.concatenate([c_prompt, c_sample], axis=0)
    pad = (-c_all.shape[0]) % 8
    c_all = jnp.pad(c_all, ((0, pad), (0, 0)))
    mods = _ada(c_all, w_ada, b_ada)
    mods_p, mods_s = mods[:, :bp], mods[:, bp:bp + bs]

    cfg = _config(tp)
    zp = jnp.zeros((depth, bp, POOL_HIST, POOL_W), x_prompt.dtype)
    zc = jnp.zeros((depth, bp, CONV_K - 1, CONV_W), x_prompt.dtype)
    y_p, p_kv, p_kr, p_pool, p_conv = _trunk(
        x_prompt, mods_p, jnp.arange(tp), 0, None, zp, zc, layers, g_final, cfg)
    y_s, s_kv, s_kr, s_pool, s_conv = _trunk(
        x_sample, mods_s, past + jnp.arange(ts), min(past, POOL_HIST),
        (cache_kv_latent, cache_k_rope), state_pool, state_conv, layers, g_final, cfg)
    return (y_p, y_s, p_kv, p_kr, p_pool, p_conv, s_kv, s_kr, s_pool, s_conv)
```

```python
import functools
import math

import jax
import jax.numpy as jnp
from jax import lax
from jax.experimental import pallas as pl
from jax.experimental.pallas import tpu as pltpu

F32 = jnp.float32
BF16 = jnp.bfloat16

EPS = 1e-6
N_HEADS = 8
QK_NOPE = 64
QK_ROPE = 32
V_HEAD = 64
Q_LORA = 512
KV_LORA = 256
ROPE_THETA = 10000.0
CHUNK = 64
ATTN_SCALE = (QK_NOPE + QK_ROPE) ** -0.5
NEG_INF = -1e30
POOL_WINDOWS = (2, 4, 8, 16)
POOL_GROUP = 128
POOL_W = 512
POOL_HIST = 15
CONV_W = 512
CONV_K = 3
PEER_HEADS = 8
N_KEYS = 128
D_KEY = 256
PEER_TOPK = 16
NOT_TOP = 99.0

LANES = 128
VMEM_LIMIT_BYTES = 56 * 2**20

HEAD_W = LANES
V_ROWS = V_HEAD + 16
Q_SCALE = ATTN_SCALE * math.log2(math.e)
HIST_ROWS = 16
CONV_HIST_ROWS = 8

COL_U, COL_B, COL_C, COL_H = 0, 512, 1024, 1536
COL_CQ = 2048
COL_CKV = 2560
COL_KR = 2816
COL_G = 3072
D_IN_PAD = 6144


def _cparams(sem):
    return pltpu.CompilerParams(dimension_semantics=sem, vmem_limit_bytes=VMEM_LIMIT_BYTES)


def _dot(a, b):
    return jnp.dot(a, b, preferred_element_type=F32)


def _dot_nt(a, b):
    return lax.dot_general(a, b, (((1,), (1,)), ((), ())), preferred_element_type=F32)


def _ada_kernel(c_ref, w_ref, b_ref, o_ref):
    c = c_ref[...]
    act = c * jax.nn.sigmoid(c)
    o_ref[0] = _dot(act.astype(BF16), w_ref[0].astype(BF16)) + b_ref[0]


def _ada(c_all, w_ada, b_ada):
    depth, d, n6 = w_ada.shape
    bp = c_all.shape[0]
    tn = 1536
    return pl.pallas_call(
        _ada_kernel,
        grid=(depth, n6 // tn),
        in_specs=[
            pl.BlockSpec((bp, d), lambda l, j: (0, 0)),
            pl.BlockSpec((1, d, tn), lambda l, j: (l, 0, j)),
            pl.BlockSpec((1, 1, tn), lambda l, j: (l, 0, j)),
        ],
        out_specs=pl.BlockSpec((1, bp, tn), lambda l, j: (l, 0, j)),
        out_shape=jax.ShapeDtypeStruct((depth, bp, n6), F32),
        compiler_params=_cparams(("arbitrary", "arbitrary")),
        name="ada",
    )(c_all, w_ada, b_ada.reshape(depth, 1, n6))


def _modnorm(x, g, sh, sc):
    ms = jnp.mean(x * x, axis=-1, keepdims=True)
    y = x * lax.rsqrt(ms + EPS) * g
    return y * (1.0 + sc) + sh


def _rmsnorm2(x, g):
    ms = jnp.mean(x * x, axis=-1, keepdims=True)
    return x * lax.rsqrt(ms + EPS) * g


def _row_plan(b, t, rows):
    if t >= rows:
        assert t % rows == 0
        return 1, rows
    assert rows % t == 0 and b % (rows // t) == 0
    return rows // t, t


def _inproj_kernel(x_ref, g_ref, sh_ref, sc_ref, w_ref, o_ref, *, col_chunk):
    gg, tg, d = x_ref.shape
    h = _modnorm(x_ref[...], g_ref[...], sh_ref[...], sc_ref[...])
    hb = h.reshape(gg * tg, d).astype(BF16)
    n = w_ref.shape[1]
    for c in range(0, n, col_chunk):
        o_ref[:, c:c + col_chunk] = _dot(hb, w_ref[:, c:c + col_chunk])


def _inproj(x3, g, sh, sc, w, rows):
    b, t, d = x3.shape
    gg, tg = _row_plan(b, t, rows)
    nj = t // tg
    n = w.shape[1]
    return pl.pallas_call(
        functools.partial(_inproj_kernel, col_chunk=512),
        grid=(b // gg, nj),
        in_specs=[
            pl.BlockSpec((gg, tg, d), lambda i, j: (i, j, 0)),
            pl.BlockSpec((1, d), lambda i, j: (0, 0)),
            pl.BlockSpec((gg, 1, d), lambda i, j: (i, 0, 0)),
            pl.BlockSpec((gg, 1, d), lambda i, j: (i, 0, 0)),
            pl.BlockSpec((d, n), lambda i, j: (0, 0)),
        ],
        out_specs=pl.BlockSpec((gg * tg, n), lambda i, j: (i * nj + j, 0)),
        out_shape=jax.ShapeDtypeStruct((b * t, n), F32),
        compiler_params=_cparams(("arbitrary", "arbitrary")),
        name="inproj",
    )(x3, g, sh, sc, w)


def _rope_group(z, cc, ss):
    return z * cc + pltpu.roll(z, HEAD_W - QK_ROPE, axis=1) * ss


def _attn_prep_kernel(cq_ref, ckv_ref, kr_ref, cc_ref, ss_ref, gq_ref, wq_ref, gkv_ref,
                      wk_ref, wv_ref, q_ref, k_ref, v_ref, kv_ref, kro_ref, *, transposed):
    cck = cc_ref[...]
    ss = ss_ref[...]
    lane = lax.broadcasted_iota(jnp.int32, cck.shape, 1)
    ccq = jnp.where(lane < QK_NOPE, 1.0, cck)
    qn = _rmsnorm2(cq_ref[...], gq_ref[...]).astype(BF16)
    ckv = _rmsnorm2(ckv_ref[...], gkv_ref[...])
    kv_ref[...] = ckv
    ckv_b = ckv.astype(BF16)
    kr = _rope_group(kr_ref[...], cck, ss)
    kro_ref[...] = kr
    v = _dot(ckv_b, wv_ref[...])
    rows = v.shape[0]
    if transposed:
        ones = jnp.ones((V_ROWS - V_HEAD, rows), F32)
        vt = v.T
        for h in range(N_HEADS):
            v_ref[0, h * V_ROWS:(h + 1) * V_ROWS, :] = jnp.concatenate(
                [vt[h * V_HEAD:(h + 1) * V_HEAD, :], ones], axis=0).astype(BF16)
    else:
        v_ref[...] = v.astype(BF16)
    for h in range(N_HEADS):
        sl = slice(h * HEAD_W, (h + 1) * HEAD_W)
        zq = _dot(qn, wq_ref[:, sl])
        qh = _rope_group(zq, ccq, ss) * Q_SCALE
        if transposed:
            q_ref[sl, :] = qh.T.astype(BF16)
        else:
            q_ref[:, sl] = qh.astype(BF16)
        k_ref[:, sl] = (_dot(ckv_b, wk_ref[:, sl]) + kr).astype(BF16)


def _attn_prep(z, cc, ss, gq, wq, gkv, wk, wv, rows, transposed):
    n = z.shape[0]
    npos = cc.shape[0] // rows
    row = lambda i: (i, 0)
    col = lambda i: (0, i)
    const = lambda i: (0, 0)
    if transposed:
        q_spec = pl.BlockSpec((N_HEADS * HEAD_W, rows), col)
        q_shape = jax.ShapeDtypeStruct((N_HEADS * HEAD_W, n), BF16)
        v_spec = pl.BlockSpec((1, N_HEADS * V_ROWS, rows), lambda i: (i, 0, 0))
        v_shape = jax.ShapeDtypeStruct((n // rows, N_HEADS * V_ROWS, rows), BF16)
    else:
        q_spec = pl.BlockSpec((rows, N_HEADS * HEAD_W), row)
        q_shape = jax.ShapeDtypeStruct((n, N_HEADS * HEAD_W), BF16)
        v_spec = pl.BlockSpec((rows, N_HEADS * V_HEAD), row)
        v_shape = jax.ShapeDtypeStruct((n, N_HEADS * V_HEAD), BF16)
    return pl.pallas_call(
        functools.partial(_attn_prep_kernel, transposed=transposed),
        grid=(n // rows,),
        in_specs=[
            pl.BlockSpec((rows, Q_LORA), lambda i: (i, COL_CQ // Q_LORA)),
            pl.BlockSpec((rows, KV_LORA), lambda i: (i, COL_CKV // KV_LORA)),
            pl.BlockSpec((rows, HEAD_W), lambda i: (i, COL_KR // HEAD_W)),
            pl.BlockSpec((rows, HEAD_W), lambda i: (i % npos, 0)),
            pl.BlockSpec((rows, HEAD_W), lambda i: (i % npos, 0)),
            pl.BlockSpec((1, Q_LORA), const),
            pl.BlockSpec(wq.shape, const),
            pl.BlockSpec((1, KV_LORA), const),
            pl.BlockSpec(wk.shape, const),
            pl.BlockSpec(wv.shape, const),
        ],
        out_specs=[
            q_spec,
            pl.BlockSpec((rows, N_HEADS * HEAD_W), row),
            v_spec,
            pl.BlockSpec((rows, KV_LORA), row),
            pl.BlockSpec((rows, HEAD_W), row),
        ],
        out_shape=[
            q_shape,
            jax.ShapeDtypeStruct((n, N_HEADS * HEAD_W), BF16),
            v_shape,
            jax.ShapeDtypeStruct((n, KV_LORA), F32),
            jax.ShapeDtypeStruct((n, HEAD_W), F32),
        ],
        compiler_params=_cparams(("arbitrary",)),
        name="attn_prep",
    )(z, z, z, cc, ss, gq, wq, gkv, wk, wv)


def _cache_expand_kernel(ckv_ref, kr_ref, wk_ref, wv_ref, place_ref, k_ref, v_ref):
    ckv_b = ckv_ref[...].astype(BF16)
    krp = _dot(kr_ref[...].astype(BF16), place_ref[...])
    v_ref[...] = _dot(ckv_b, wv_ref[...]).astype(BF16)
    for h in range(N_HEADS):
        sl = slice(h * HEAD_W, (h + 1) * HEAD_W)
        k_ref[:, sl] = (_dot(ckv_b, wk_ref[:, sl]) + krp).astype(BF16)


def _cache_expand(ckv, kr, wk, wv, place, rows):
    n = ckv.shape[0]
    row = lambda i: (i, 0)
    const = lambda i: (0, 0)
    return pl.pallas_call(
        _cache_expand_kernel,
        grid=(n // rows,),
        in_specs=[
            pl.BlockSpec((rows, KV_LORA), row),
            pl.BlockSpec((rows, QK_ROPE), row),
            pl.BlockSpec(wk.shape, const),
            pl.BlockSpec(wv.shape, const),
            pl.BlockSpec(place.shape, const),
        ],
        out_specs=[
            pl.BlockSpec((rows, N_HEADS * HEAD_W), row),
            pl.BlockSpec((rows, N_HEADS * V_HEAD), row),
        ],
        out_shape=[
            jax.ShapeDtypeStruct((n, N_HEADS * HEAD_W), BF16),
            jax.ShapeDtypeStruct((n, N_HEADS * V_HEAD), BF16),
        ],
        compiler_params=_cparams(("arbitrary",)),
        name="cache_expand",
    )(ckv, kr, wk, wv, place)


def _flash_step(qt_ref, k_ref, vt_ref, m_ref, acc_ref, masked, q0, k0):
    tk, tq = k_ref.shape[0], qt_ref.shape[1]
    if masked:
        kc = (k0 + lax.broadcasted_iota(jnp.int32, (tk, tq), 0)) // CHUNK
        qc = (q0 + lax.broadcasted_iota(jnp.int32, (tk, tq), 1)) // CHUNK
        keep = kc <= qc
    def scores(h):
        sl = slice(h * HEAD_W, (h + 1) * HEAD_W)
        s = _dot(k_ref[:, sl], qt_ref[sl, :])
        return jnp.where(keep, s, NEG_INF) if masked else s

    def stats(h, s):
        m_prev = m_ref[h]
        m_new = jnp.maximum(m_prev, jnp.max(s, axis=0, keepdims=True))
        m_ref[h] = m_new
        return m_new, jnp.exp2(m_prev - m_new)

    s = {0: scores(0)}
    if N_HEADS > 1:
        s[1] = scores(1)
    st = {0: stats(0, s[0])}
    for h in range(N_HEADS):
        vs = slice(h * V_ROWS, (h + 1) * V_ROWS)
        if h + 2 < N_HEADS:
            s[h + 2] = scores(h + 2)
        if h + 1 < N_HEADS:
            st[h + 1] = stats(h + 1, s[h + 1])
        m_new, alpha = st.pop(h)
        p = jnp.exp2(s.pop(h) - m_new).astype(BF16)
        acc_ref[h] = alpha * acc_ref[h] + _dot(vt_ref[0, vs, :], p)


def _flash_kernel(qt_ref, k_ref, vt_ref, o_ref, m_ref, acc_ref, *, ratio):
    qi = pl.program_id(1)
    ki = pl.program_id(2)
    tk, tq = k_ref.shape[0], qt_ref.shape[1]

    @pl.when(ki == 0)
    def _():
        m_ref[...] = jnp.full(m_ref.shape, NEG_INF, F32)
        acc_ref[...] = jnp.zeros(acc_ref.shape, F32)

    @pl.when(ki < qi * ratio)
    def _():
        _flash_step(qt_ref, k_ref, vt_ref, m_ref, acc_ref, False, 0, 0)

    @pl.when(jnp.logical_and(ki >= qi * ratio, ki < (qi + 1) * ratio))
    def _():
        _flash_step(qt_ref, k_ref, vt_ref, m_ref, acc_ref, True, qi * tq, ki * tk)

    @pl.when(ki == (qi + 1) * ratio - 1)
    def _():
        for h in range(N_HEADS):
            a = acc_ref[h]
            o = a[:V_HEAD, :] / a[V_HEAD:V_HEAD + 1, :]
            o_ref[:, h * V_HEAD:(h + 1) * V_HEAD] = o.T.astype(o_ref.dtype)


def _flash_prompt(qt, k, vt, b, t, tq, tk):
    nq, nk = t // tq, t // tk
    ratio = tq // tk
    last = lambda qi, ki: jnp.minimum(ki, (qi + 1) * ratio - 1)
    return pl.pallas_call(
        functools.partial(_flash_kernel, ratio=ratio),
        grid=(b, nq, nk),
        in_specs=[
            pl.BlockSpec((N_HEADS * HEAD_W, tq), lambda bi, qi, ki: (0, bi * nq + qi)),
            pl.BlockSpec((tk, N_HEADS * HEAD_W), lambda bi, qi, ki: (bi * nk + last(qi, ki), 0)),
            pl.BlockSpec((1, N_HEADS * V_ROWS, tk), lambda bi, qi, ki: (bi * nk + last(qi, ki), 0, 0)),
        ],
        out_specs=pl.BlockSpec((tq, N_HEADS * V_HEAD), lambda bi, qi, ki: (bi * nq + qi, 0)),
        out_shape=jax.ShapeDtypeStruct((b * t, N_HEADS * V_HEAD), BF16),
        scratch_shapes=[
            pltpu.VMEM((N_HEADS, 1, tq), F32),
            pltpu.VMEM((N_HEADS, V_ROWS, tq), F32),
        ],
        compiler_params=_cparams(("arbitrary", "arbitrary", "arbitrary")),
        name="flash_prompt",
    )(qt, k, vt)


def _attn_sample_kernel(q_ref, kc_ref, vc_ref, kn_ref, vn_ref, o_ref):
    for h in range(N_HEADS):
        sl = slice(h * HEAD_W, (h + 1) * HEAD_W)
        vs = slice(h * V_HEAD, (h + 1) * V_HEAD)
        qh = q_ref[:, sl]
        sc = _dot_nt(qh, kc_ref[:, sl])
        sn = _dot_nt(qh, kn_ref[:, sl])
        m = jnp.maximum(jnp.max(sc, axis=1, keepdims=True), jnp.max(sn, axis=1, keepdims=True))
        pc = jnp.exp2(sc - m)
        pn = jnp.exp2(sn - m)
        den = jnp.sum(pc, axis=1, keepdims=True) + jnp.sum(pn, axis=1, keepdims=True)
        o = _dot(pc.astype(BF16), vc_ref[:, vs]) + _dot(pn.astype(BF16), vn_ref[:, vs])
        o_ref[:, vs] = (o / den).astype(o_ref.dtype)


def _attn_sample(q, kc, vc, kn, vn, b, t, past):
    return pl.pallas_call(
        _attn_sample_kernel,
        grid=(b,),
        in_specs=[
            pl.BlockSpec((t, N_HEADS * HEAD_W), lambda i: (i, 0)),
            pl.BlockSpec((past, N_HEADS * HEAD_W), lambda i: (i, 0)),
            pl.BlockSpec((past, N_HEADS * V_HEAD), lambda i: (i, 0)),
            pl.BlockSpec((t, N_HEADS * HEAD_W), lambda i: (i, 0)),
            pl.BlockSpec((t, N_HEADS * V_HEAD), lambda i: (i, 0)),
        ],
        out_specs=pl.BlockSpec((t, N_HEADS * V_HEAD), lambda i: (i, 0)),
        out_shape=jax.ShapeDtypeStruct((b * t, N_HEADS * V_HEAD), BF16),
        compiler_params=_cparams(("arbitrary",)),
        name="attn_sample",
    )(q, kc, vc, kn, vn)


def _mix_kernel(x_ref, gt_ref, ubch_ref, g_ref, attn_ref, pu_ref, pch_ref, hp_ref, hc_ref,
                wpool_ref, pscale_ref, convw_ref, wbr_ref, wout_ref,
                xo_ref, ptail_ref, ctail_ref, extp_ref, extc_ref, *, n_hist):
    gg, tg, d = x_ref.shape
    j = pl.program_id(1)
    first = j == 0

    ubch = ubch_ref[...]
    u = ubch[:, COL_U:COL_U + POOL_W].reshape(gg, tg, POOL_W)
    bgate = ubch[:, COL_B:COL_B + CONV_W].reshape(gg, tg, CONV_W)
    cu = (ubch[:, COL_C:COL_C + CONV_W] * ubch[:, COL_H:COL_H + CONV_W]).reshape(gg, tg, CONV_W)

    hist_p = jnp.where(first, hp_ref[...], pu_ref[...].reshape(1, HIST_ROWS, POOL_W))
    pch = pch_ref[...]
    prev_cu = (pch[:, :CONV_W] * pch[:, CONV_W:])[HIST_ROWS - CONV_HIST_ROWS:]
    hist_c = jnp.where(first, hc_ref[...], prev_cu.reshape(1, CONV_HIST_ROWS, CONV_W))

    extp_ref[:, :HIST_ROWS, :] = hist_p
    extp_ref[:, HIST_ROWS:, :] = u
    extc_ref[:, :CONV_HIST_ROWS, :] = hist_c
    extc_ref[:, CONV_HIST_ROWS:, :] = cu
    ptail_ref[...] = u[:, tg - HIST_ROWS:, :]
    ctail_ref[...] = cu[:, tg - CONV_HIST_ROWS:, :]

    tpos = j * tg + lax.broadcasted_iota(jnp.int32, (1, tg, 1), 1)
    pooled = []
    for gi, w in enumerate(POOL_WINDOWS):
        cs = slice(gi * POOL_GROUP, (gi + 1) * POOL_GROUP)
        acc = u[:, :, cs]
        for kk in range(1, w):
            acc = acc + extp_ref[:, HIST_ROWS - kk:HIST_ROWS - kk + tg, cs]
        cnt = jnp.minimum(tpos + 1 + n_hist, w).astype(F32)
        dd = acc / cnt - u[:, :, cs]
        pooled.append(_dot(dd.reshape(gg * tg, POOL_GROUP).astype(BF16), wpool_ref[gi]))
    pool = jnp.concatenate(pooled, axis=1) * pscale_ref[...]

    cw = convw_ref[...]
    yc = (cw[0:1, :] * extc_ref[:, CONV_HIST_ROWS - 2:CONV_HIST_ROWS - 2 + tg, :]
          + cw[1:2, :] * extc_ref[:, CONV_HIST_ROWS - 1:CONV_HIST_ROWS - 1 + tg, :]
          + cw[2:3, :] * cu)
    conv = (bgate * yc).reshape(gg * tg, CONV_W)

    gates = jax.nn.sigmoid(g_ref[...])
    mixed = (gates[:, 0:d] * _dot(attn_ref[...], wbr_ref[0])
             + gates[:, d:2 * d] * _dot(pool.astype(BF16), wbr_ref[1])
             + gates[:, 2 * d:3 * d] * _dot(conv.astype(BF16), wbr_ref[2]))
    out = _dot(mixed.astype(BF16), wout_ref[...])
    xo_ref[...] = x_ref[...] + gt_ref[...] * out.reshape(gg, tg, d)


def _mix(x3, gt, z, attn, hist_p, hist_c, wpool, pscale, convw, wbr, wout, rows, n_hist):
    b, t, d = x3.shape
    gg, tg = _row_plan(b, t, rows)
    nj = t // tg
    rb = lambda i, j: i * nj + j
    hpb = tg // HIST_ROWS

    def prev_rows(i, j):
        return jnp.maximum(rb(i, j) * hpb - 1, 0)

    const2 = lambda i, j: (0, 0)
    const3 = lambda i, j: (0, 0, 0)
    return pl.pallas_call(
        functools.partial(_mix_kernel, n_hist=n_hist),
        grid=(b // gg, nj),
        in_specs=[
            pl.BlockSpec((gg, tg, d), lambda i, j: (i, j, 0)),
            pl.BlockSpec((gg, 1, d), lambda i, j: (i, 0, 0)),
            pl.BlockSpec((gg * tg, COL_CQ), lambda i, j: (rb(i, j), 0)),
            pl.BlockSpec((gg * tg, 3 * d), lambda i, j: (rb(i, j), COL_G // (3 * d))),
            pl.BlockSpec((gg * tg, N_HEADS * V_HEAD), lambda i, j: (rb(i, j), 0)),
            pl.BlockSpec((HIST_ROWS, POOL_W), lambda i, j: (prev_rows(i, j), 0)),
            pl.BlockSpec((HIST_ROWS, 2 * CONV_W), lambda i, j: (prev_rows(i, j), COL_C // (2 * CONV_W))),
            pl.BlockSpec((gg, HIST_ROWS, POOL_W), lambda i, j: (i, 0, 0)),
            pl.BlockSpec((gg, CONV_HIST_ROWS, CONV_W), lambda i, j: (i, 0, 0)),
            pl.BlockSpec(wpool.shape, const3),
            pl.BlockSpec(pscale.shape, const2),
            pl.BlockSpec(convw.shape, const2),
            pl.BlockSpec(wbr.shape, const3),
            pl.BlockSpec(wout.shape, const2),
        ],
        out_specs=[
            pl.BlockSpec((gg, tg, d), lambda i, j: (i, j, 0)),
            pl.BlockSpec((gg, HIST_ROWS, POOL_W), lambda i, j: (i, 0, 0)),
            pl.BlockSpec((gg, CONV_HIST_ROWS, CONV_W), lambda i, j: (i, 0, 0)),
        ],
        out_shape=[
            jax.ShapeDtypeStruct((b, t, d), F32),
            jax.ShapeDtypeStruct((b, HIST_ROWS, POOL_W), F32),
            jax.ShapeDtypeStruct((b, CONV_HIST_ROWS, CONV_W), F32),
        ],
        scratch_shapes=[
            pltpu.VMEM((gg, HIST_ROWS + tg, POOL_W), F32),
            pltpu.VMEM((gg, CONV_HIST_ROWS + tg, CONV_W), F32),
        ],
        compiler_params=_cparams(("arbitrary", "arbitrary")),
        name="mix_merge",
    )(x3, gt, z, z, attn, z, z, hist_p, hist_c, wpool, pscale, convw, wbr, wout)


def _peer_query_kernel(x_ref, g_ref, sh_ref, sc_ref, wq_ref, keys_ref, ht_ref, s1_ref, s2_ref):
    gg, tg, d = x_ref.shape
    h = _modnorm(x_ref[...], g_ref[...], sh_ref[...], sc_ref[...]).reshape(gg * tg, d)
    ht_ref[0] = h.T.astype(BF16)
    hb = h.astype(BF16)
    half = D_KEY // 2
    k1 = keys_ref[0]
    k2 = keys_ref[1]
    def query(hh):
        return _dot(hb, wq_ref[:, hh * D_KEY:(hh + 1) * D_KEY]).astype(BF16)

    q_next = query(0)
    for hh in range(PEER_HEADS):
        q = q_next
        if hh + 1 < PEER_HEADS:
            q_next = query(hh + 1)
        s1_ref[hh, 0] = _dot_nt(k1, q[:, :half])
        s2_ref[hh, 0] = _dot_nt(k2, q[:, half:])


def _peer_query(x3, g, sh, sc, wq, keys, rows):
    b, t, d = x3.shape
    gg, tg = _row_plan(b, t, rows)
    nj = t // tg
    nt = b * t // rows
    tok = lambda i, j: (0, i * nj + j, 0, 0)
    return pl.pallas_call(
        _peer_query_kernel,
        grid=(b // gg, nj),
        in_specs=[
            pl.BlockSpec((gg, tg, d), lambda i, j: (i, j, 0)),
            pl.BlockSpec((1, d), lambda i, j: (0, 0)),
            pl.BlockSpec((gg, 1, d), lambda i, j: (i, 0, 0)),
            pl.BlockSpec((gg, 1, d), lambda i, j: (i, 0, 0)),
            pl.BlockSpec(wq.shape, lambda i, j: (0, 0)),
            pl.BlockSpec(keys.shape, lambda i, j: (0, 0, 0)),
        ],
        out_specs=[
            pl.BlockSpec((1, d, rows), lambda i, j: (i * nj + j, 0, 0)),
            pl.BlockSpec((PEER_HEADS, 1, N_KEYS, rows), tok),
            pl.BlockSpec((PEER_HEADS, 1, N_KEYS, rows), tok),
        ],
        out_shape=[
            jax.ShapeDtypeStruct((nt, d, rows), BF16),
            jax.ShapeDtypeStruct((PEER_HEADS, nt, N_KEYS, rows), F32),
            jax.ShapeDtypeStruct((PEER_HEADS, nt, N_KEYS, rows), F32),
        ],
        compiler_params=_cparams(("arbitrary", "arbitrary")),
        name="peer_query",
    )(x3, g, sh, sc, wq, keys)


def _top_extract(s, with_rank):
    nk, r = s.shape
    iota = lax.broadcasted_iota(jnp.int32, (nk, r), 0).astype(F32)
    iota_k = lax.broadcasted_iota(jnp.int32, (PEER_TOPK, r), 0)
    rank = jnp.full((nk, r), NOT_TOP, F32) if with_rank else None
    vals = jnp.zeros((PEER_TOPK, r), F32)
    idxs = jnp.zeros((PEER_TOPK, r), F32)
    x = s
    for it in range(PEER_TOPK):
        m = jnp.max(x, axis=0, keepdims=True)
        idx = jnp.min(jnp.where(x == m, iota, float(nk)), axis=0, keepdims=True)
        hit = iota == idx
        if with_rank:
            rank = jnp.where(hit, float(it), rank)
        x = jnp.where(hit, -jnp.inf, x)
        vals = jnp.where(iota_k == it, m, vals)
        idxs = jnp.where(iota_k == it, idx, idxs)
    return rank, vals, idxs


def _peer_select_kernel(s1_ref, s2_ref, c_ref, lr_ref, rk2_ref, d_ref):
    for h in range(s1_ref.shape[0]):
        _peer_select_head(h, s1_ref, s2_ref, c_ref, lr_ref, rk2_ref, d_ref)


def _peer_select_head(h, s1_ref, s2_ref, c_ref, lr_ref, rk2_ref, d_ref):
    s1 = s1_ref[h, 0]
    s2 = s2_ref[h, 0]
    nk, r = s1.shape
    _, v1, idx1 = _top_extract(s1, False)
    rank2, v2, _ = _top_extract(s2, True)
    iota_k = lax.broadcasted_iota(jnp.int32, (PEER_TOPK, r), 0).astype(F32)
    cnt = jnp.zeros((PEER_TOPK, r), F32)
    front = v1 + v2[0:1, :]
    top = front[0:1, :]
    zsum = jnp.zeros((1, r), F32)
    for _ in range(PEER_TOPK):
        m = jnp.max(front, axis=0, keepdims=True)
        a = jnp.min(jnp.where(front == m, iota_k, float(PEER_TOPK)), axis=0, keepdims=True)
        hit = iota_k == a
        zsum = zsum + jnp.exp(m - top)
        cnt = jnp.where(hit, cnt + 1.0, cnt)
        c_hit = jnp.max(jnp.where(hit, cnt, 0.0), axis=0, keepdims=True)
        v1_hit = jnp.max(jnp.where(hit, v1, -jnp.inf), axis=0, keepdims=True)
        nxt = jnp.full((1, r), -jnp.inf, F32)
        for bcol in range(1, PEER_TOPK):
            nxt = jnp.where(c_hit == float(bcol), v2[bcol:bcol + 1, :], nxt)
        front = jnp.where(hit, v1_hit + nxt, front)
    iota = lax.broadcasted_iota(jnp.int32, (nk, r), 0).astype(F32)
    lr = jnp.zeros((nk, r), F32)
    for a in range(PEER_TOPK):
        lr = jnp.where(iota == idx1[a:a + 1, :], cnt[a:a + 1, :], lr)
    c_ref[h, 0] = jnp.exp(s1 - v1[0:1, :]) / zsum
    lr_ref[h, 0] = lr
    rk2_ref[h, 0] = rank2.astype(BF16)
    d_ref[h, 0] = jnp.exp(s2 - v2[0:1, :]).astype(BF16)


def _peer_select(s1, s2):
    hh, nt, nk, lanes = s1.shape
    heads_per_step = 4
    spec = pl.BlockSpec((heads_per_step, 1, nk, lanes), lambda i, h: (h, i, 0, 0))
    f32 = jax.ShapeDtypeStruct(s1.shape, F32)
    b16 = jax.ShapeDtypeStruct(s1.shape, BF16)
    return pl.pallas_call(
        _peer_select_kernel,
        grid=(nt, hh // heads_per_step),
        in_specs=[spec, spec],
        out_specs=[spec, spec, spec, spec],
        out_shape=[f32, f32, b16, b16],
        compiler_params=_cparams(("arbitrary", "arbitrary")),
        name="peer_select",
    )(s1, s2)


def _gelu(x):
    return 0.5 * x * (1.0 + lax.erf(x * (2.0 ** -0.5)))


def _peer_dense_kernel(x_ref, gt_ref, ht_ref, u_ref, vt_ref, c_ref, lr_ref, rk2_ref, d_ref,
                       xo_ref, a_ref, w_ref, acc_ref, *, rows_per_chunk):
    gg, tg, d = x_ref.shape
    nt, _, lanes = ht_ref.shape
    e = pl.program_id(1)
    slab = 16
    n_slab = N_KEYS // slab

    gsz = acc_ref.shape[2] // lanes
    ng = nt // gsz

    @pl.when(e == 0)
    def _():
        acc_ref[...] = jnp.zeros(acc_ref.shape, F32)

    w_ref[(ng - 1) % 2] = jnp.zeros(w_ref.shape[1:], BF16)

    def group(g):
        prev = (g + ng - 1) % ng
        rhs = jnp.concatenate([ht_ref[g * gsz + k] for k in range(gsz)], axis=1)
        w_prev = w_ref.at[prev % 2]
        w_cur = w_ref.at[g % 2]
        ec = a_ref.shape[0]
        n_part = 1
        pe, pd = ec // n_part, d // n_part
        ii_per_part = rows_per_chunk // n_part

        def key_part(q):
            a_ref[q * pe:(q + 1) * pe, :] = _dot(u_ref[q * pe:(q + 1) * pe, :], rhs)

        def value_part(q):
            acc_ref[prev, q * pd:(q + 1) * pd, :] += _dot(vt_ref[q * pd:(q + 1) * pd, :], w_prev[...])

        key_part(0)
        for q in range(n_part):
            if q + 1 < n_part:
                key_part(q + 1)
            value_part(q)
            for k in range(gsz):
                lt = g * gsz + k
                ls = slice(k * lanes, (k + 1) * lanes)
                for ii in range(q * ii_per_part, (q + 1) * ii_per_part):
                    cl = []
                    for hh in range(PEER_HEADS):
                        cl.append((
                            jnp.broadcast_to(c_ref[hh, lt, ii:ii + 1, :], (slab, lanes)).astype(BF16),
                            jnp.broadcast_to(lr_ref[hh, lt, ii:ii + 1, :], (slab, lanes)).astype(BF16)))
                    for jv in range(n_slab):
                        rs = slice(jv * slab, (jv + 1) * slab)
                        gate = None
                        for hh in range(PEER_HEADS):
                            dd = d_ref[hh, lt, rs, :]
                            term = cl[hh][0] * jnp.where(rk2_ref[hh, lt, rs, :] < cl[hh][1], dd,
                                                         jnp.zeros_like(dd))
                            gate = term if gate is None else gate + term
                        rows = slice(ii * N_KEYS + jv * slab, ii * N_KEYS + (jv + 1) * slab)
                        w_cur[rows, ls] = _gelu(a_ref[rows, ls]).astype(BF16) * gate

    for g in range(ng):
        group(g)
    acc_ref[ng - 1] += _dot(vt_ref[...], w_ref[(ng - 1) % 2])

    @pl.when(e == pl.num_programs(1) - 1)
    def _():
        for g in range(ng):
            for k in range(gsz):
                lt = g * gsz + k
                upd = acc_ref[g, :, k * lanes:(k + 1) * lanes].T
                if gg == 1:
                    rows = slice(lt * lanes, (lt + 1) * lanes)
                    xo_ref[0, rows, :] = x_ref[0, rows, :] + gt_ref[0] * upd
                else:
                    per = lanes // tg
                    rows = slice(lt * per, (lt + 1) * per)
                    xo_ref[rows] = x_ref[rows] + gt_ref[rows] * upd.reshape(per, tg, d)


def _peer_dense(x3, gt, ht, u, vt, c, lr, rk2, dd, tiles_per_block, rows_per_chunk):
    b, t, d = x3.shape
    nt_all, _, lanes = ht.shape
    gg, tg = _row_plan(b, t, tiles_per_block * lanes)
    nj = t // tg
    ne = u.shape[0]
    ec = rows_per_chunk * N_KEYS
    gsz = 1
    full = pl.BlockSpec((PEER_HEADS, tiles_per_block, N_KEYS, lanes), lambda i, e: (0, i, 0, 0))
    part = pl.BlockSpec((PEER_HEADS, tiles_per_block, rows_per_chunk, lanes),
                        lambda i, e: (0, i, e, 0))
    return pl.pallas_call(
        functools.partial(_peer_dense_kernel, rows_per_chunk=rows_per_chunk),
        grid=(nt_all // tiles_per_block, ne // ec),
        in_specs=[
            pl.BlockSpec((gg, tg, d), lambda i, e: (i // nj, i % nj, 0)),
            pl.BlockSpec((gg, 1, d), lambda i, e: (i // nj, 0, 0)),
            pl.BlockSpec((tiles_per_block, d, lanes), lambda i, e: (i, 0, 0)),
            pl.BlockSpec((ec, d), lambda i, e: (e, 0)),
            pl.BlockSpec((d, ec), lambda i, e: (0, e)),
            part, part, full, full,
        ],
        out_specs=pl.BlockSpec((gg, tg, d), lambda i, e: (i // nj, i % nj, 0)),
        out_shape=jax.ShapeDtypeStruct((b, t, d), F32),
        scratch_shapes=[
            pltpu.VMEM((ec, gsz * lanes), F32),
            pltpu.VMEM((2, ec, gsz * lanes), BF16),
            pltpu.VMEM((tiles_per_block // gsz, d, gsz * lanes), F32),
        ],
        compiler_params=_cparams(("arbitrary", "arbitrary")),
        name="peer_dense",
    )(x3, gt, ht, u, vt, c, lr, rk2, dd)


def _final_kernel(x_ref, g_ref, o_ref):
    o_ref[...] = _rmsnorm2(x_ref[...], g_ref[...])


def _final_norm(x3, g, rows):
    b, t, d = x3.shape
    x2 = x3.reshape(b * t, d)
    y = pl.pallas_call(
        _final_kernel,
        grid=(b * t // rows,),
        in_specs=[pl.BlockSpec((rows, d), lambda i: (i, 0)), pl.BlockSpec((1, d), lambda i: (0, 0))],
        out_specs=pl.BlockSpec((rows, d), lambda i: (i, 0)),
        out_shape=jax.ShapeDtypeStruct((b * t, d), F32),
        compiler_params=_cparams(("arbitrary",)),
        name="final_norm",
    )(x2, g)
    return y.reshape(b, t, d)


def _swap_halves(w):
    half = w.shape[-1] // 2
    return jnp.concatenate([w[..., half:], w[..., :half]], axis=-1)


def _prep_layer(p, l):
    d = p['w_in'].shape[1]
    w_in = p['w_in'][l]
    offs = [0]
    for nsz in (Q_LORA, KV_LORA, QK_ROPE, POOL_W, CONV_W, CONV_W, CONV_W, 3 * d):
        offs.append(offs[-1] + nsz)
    w_cq, w_ckv, w_kr, w_u, w_b, w_c, w_h, w_g = [w_in[:, offs[i]:offs[i + 1]] for i in range(8)]
    zeros = lambda n: jnp.zeros((d, n), w_in.dtype)
    w_krg = jnp.concatenate([zeros(QK_NOPE), w_kr, _swap_halves(w_kr)], axis=1)
    w_in_r = jnp.concatenate(
        [w_u, w_b, w_c, w_h, w_cq, w_ckv, w_krg, zeros(COL_G - COL_KR - HEAD_W), w_g], axis=1)
    assert w_in_r.shape[1] == D_IN_PAD

    w_uq = p['w_uq'][l]
    wq_r = jnp.concatenate(
        [w_uq[..., :QK_NOPE], w_uq[..., QK_NOPE:], _swap_halves(w_uq[..., QK_NOPE:])], axis=-1)
    wq_r = wq_r.reshape(Q_LORA, N_HEADS * HEAD_W)
    w_ukv = p['w_ukv'][l]
    wk_r = jnp.concatenate(
        [w_ukv[..., :QK_NOPE], jnp.zeros((KV_LORA, N_HEADS, HEAD_W - QK_NOPE), w_ukv.dtype)],
        axis=-1).reshape(KV_LORA, N_HEADS * HEAD_W)
    wv_r = w_ukv[..., QK_NOPE:].reshape(KV_LORA, N_HEADS * V_HEAD)
    return dict(
        w_in=w_in_r.astype(BF16), wq=wq_r.astype(BF16), wk=wk_r.astype(BF16), wv=wv_r.astype(BF16),
        g_mix=p['g_mix'][l][None, :], g_q=p['g_q'][l][None, :], g_kv=p['g_kv'][l][None, :],
        w_pool=p['w_pool'][l].astype(BF16), pool_scale=p['pool_scale'][l][None, :],
        conv_w=jnp.pad(p['conv_w'][l], ((0, 8 - CONV_K), (0, 0))),
        w_branch=p['w_branch'][l].astype(BF16), w_out=p['w_out'][l].astype(BF16),
        g_ffn=p['g_ffn'][l][None, :],
        peer_wq=p['peer_wq'][l].reshape(d, PEER_HEADS * D_KEY).astype(BF16),
        peer_keys=p['peer_keys'][l].astype(BF16),
        peer_u=p['peer_u'][l].astype(BF16),
        peer_vt=p['peer_v'][l].T.astype(BF16),
    )


def _rope_tables(pos):
    half = QK_ROPE // 2
    inv = ROPE_THETA ** (-jnp.arange(half, dtype=F32) / half)
    ang = pos.astype(F32)[:, None] * inv[None, :]
    cos, sin = jnp.cos(ang), jnp.sin(ang)
    z = lambda n: jnp.zeros((pos.shape[0], n), F32)
    cc = jnp.concatenate([z(QK_NOPE), cos, cos, z(QK_ROPE)], axis=1)
    ss = jnp.concatenate([z(QK_NOPE), -sin, sin, z(QK_ROPE)], axis=1)
    return cc, ss


def _trunk(x, mods, pos, n_hist, cache, hist_pool, hist_conv, layers, g_final, cfg):
    b, t, d = x.shape
    rows = cfg['rows']
    n = b * t
    cc, ss = _rope_tables(pos)
    if t < rows:
        cc = jnp.tile(cc, (rows // t, 1))
        ss = jnp.tile(ss, (rows // t, 1))
    place = jnp.concatenate(
        [jnp.zeros((QK_ROPE, QK_NOPE), F32), jnp.eye(QK_ROPE, dtype=F32),
         jnp.zeros((QK_ROPE, HEAD_W - QK_NOPE - QK_ROPE), F32)], axis=1).astype(BF16)
    new_kv, new_kr, new_pool, new_conv = [], [], [], []
    for l, lw in enumerate(layers):
        mod = mods[l].reshape(b, 1, 6 * d)
        sh1, sc1, gt1, sh2, sc2, gt2 = [mod[:, :, i * d:(i + 1) * d] for i in range(6)]
        z = _inproj(x, lw['g_mix'], sh1, sc1, lw['w_in'], rows)
        q, k, v, ckv, krg = _attn_prep(z, cc, ss, lw['g_q'], lw['wq'], lw['g_kv'], lw['wk'],
                                       lw['wv'], rows, cache is None)
        new_kv.append(ckv.reshape(b, t, KV_LORA))
        new_kr.append(krg[:, QK_NOPE:QK_NOPE + QK_ROPE].reshape(b, t, QK_ROPE))
        if cache is None:
            attn = _flash_prompt(q, k, v, b, t, cfg['attn_tq'], cfg['attn_tk'])
        else:
            ckv_c, kr_c = cache
            past = ckv_c.shape[2]
            kc, vc = _cache_expand(ckv_c[l].reshape(b * past, KV_LORA),
                                   kr_c[l].reshape(b * past, QK_ROPE),
                                   lw['wk'], lw['wv'], place, cfg['cache_rows'])
            attn = _attn_sample(q, kc, vc, k, v, b, t, past)
        hp = jnp.pad(hist_pool[l], ((0, 0), (HIST_ROWS - POOL_HIST, 0), (0, 0)))
        hc = jnp.pad(hist_conv[l], ((0, 0), (CONV_HIST_ROWS - (CONV_K - 1), 0), (0, 0)))
        x, ptail, ctail = _mix(x, gt1, z, attn, hp, hc, lw['w_pool'], lw['pool_scale'],
                               lw['conv_w'], lw['w_branch'], lw['w_out'], rows, n_hist)
        new_pool.append(ptail[:, HIST_ROWS - POOL_HIST:, :])
        new_conv.append(ctail[:, CONV_HIST_ROWS - (CONV_K - 1):, :])
        ht, s1, s2 = _peer_query(x, lw['g_ffn'], sh2, sc2, lw['peer_wq'], lw['peer_keys'], rows)
        c, lr, rk2, dd = _peer_select(s1, s2)
        x = _peer_dense(x, gt2, ht, lw['peer_u'], lw['peer_vt'], c, lr, rk2, dd,
                        min(cfg['peer_tiles'], n // rows), cfg['rows_per_chunk'])
    y = _final_norm(x, g_final[None, :], rows)
    return y, jnp.stack(new_kv), jnp.stack(new_kr), jnp.stack(new_pool), jnp.stack(new_conv)


def _config(t_prompt):
    rows = min(256, t_prompt)
    return dict(rows=rows, attn_tq=min(512, t_prompt), attn_tk=min(256, t_prompt), cache_rows=256, peer_tiles=4,
                rows_per_chunk=8)


def kernel(x_prompt, x_sample, cache_kv_latent, cache_k_rope, state_pool, state_conv,
           c_prompt, c_sample, w_ada, b_ada, g_mix, w_in, g_q, w_uq, g_kv, w_ukv,
           w_pool, pool_scale, conv_w, w_branch, w_out, g_ffn, peer_wq, peer_keys,
           peer_u, peer_v, g_final):
    p = {'w_in': w_in, 'g_mix': g_mix, 'g_q': g_q, 'w_uq': w_uq, 'g_kv': g_kv, 'w_ukv': w_ukv,
         'w_pool': w_pool, 'pool_scale': pool_scale, 'conv_w': conv_w, 'w_branch': w_branch,
         'w_out': w_out, 'g_ffn': g_ffn, 'peer_wq': peer_wq, 'peer_keys': peer_keys,
         'peer_u': peer_u, 'peer_v': peer_v}
    depth = w_ada.shape[0]
    bp, tp, d = x_prompt.shape
    bs, ts, _ = x_sample.shape
    past = cache_kv_latent.shape[2]
    layers = [_prep_layer(p, l) for l in range(depth)]

    c_all = jnp.concatenate([c_prompt, c_sample], axis=0)
    pad = (-c_all.shape[0]) % 8
    c_all = jnp.pad(c_all, ((0, pad), (0, 0)))
    mods = _ada(c_all, w_ada, b_ada)
    mods_p, mods_s = mods[:, :bp], mods[:, bp:bp + bs]

    cfg = _config(tp)
    zp = jnp.zeros((depth, bp, POOL_HIST, POOL_W), x_prompt.dtype)
    zc = jnp.zeros((depth, bp, CONV_K - 1, CONV_W), x_prompt.dtype)
    y_p, p_kv, p_kr, p_pool, p_conv = _trunk(
        x_prompt, mods_p, jnp.arange(tp), 0, None, zp, zc, layers, g_final, cfg)
    y_s, s_kv, s_kr, s_pool, s_conv = _trunk(
        x_sample, mods_s, past + jnp.arange(ts), min(past, POOL_HIST),
        (cache_kv_latent, cache_k_rope), state_pool, state_conv, layers, g_final, cfg)
    return (y_p, y_s, p_kv, p_kr, p_pool, p_conv, s_kv, s_kr, s_pool, s_conv)
```

```python
import functools
import math

import jax
import jax.numpy as jnp
from jax import lax
from jax.experimental import pallas as pl
from jax.experimental.pallas import tpu as pltpu

F32 = jnp.float32
BF16 = jnp.bfloat16

EPS = 1e-6
N_HEADS = 8
QK_NOPE = 64
QK_ROPE = 32
V_HEAD = 64
Q_LORA = 512
KV_LORA = 256
ROPE_THETA = 10000.0
CHUNK = 64
ATTN_SCALE = (QK_NOPE + QK_ROPE) ** -0.5
NEG_INF = -1e30
POOL_WINDOWS = (2, 4, 8, 16)
POOL_GROUP = 128
POOL_W = 512
POOL_HIST = 15
CONV_W = 512
CONV_K = 3
PEER_HEADS = 8
N_KEYS = 128
D_KEY = 256
PEER_TOPK = 16
NOT_TOP = 99.0

LANES = 128
VMEM_LIMIT_BYTES = 56 * 2**20

HEAD_W = LANES
V_ROWS = V_HEAD + 16
Q_SCALE = ATTN_SCALE * math.log2(math.e)
HIST_ROWS = 16
CONV_HIST_ROWS = 8

COL_U, COL_B, COL_C, COL_H = 0, 512, 1024, 1536
COL_CQ = 2048
COL_CKV = 2560
COL_KR = 2816
COL_G = 3072
D_IN_PAD = 6144


def _cparams(sem):
    return pltpu.CompilerParams(dimension_semantics=sem, vmem_limit_bytes=VMEM_LIMIT_BYTES)


def _dot(a, b):
    return jnp.dot(a, b, preferred_element_type=F32)


def _dot_nt(a, b):
    return lax.dot_general(a, b, (((1,), (1,)), ((), ())), preferred_element_type=F32)


def _ada_kernel(c_ref, w_ref, b_ref, o_ref):
    c = c_ref[...]
    act = c * jax.nn.sigmoid(c)
    o_ref[0] = _dot(act.astype(BF16), w_ref[0].astype(BF16)) + b_ref[0]


def _ada(c_all, w_ada, b_ada):
    depth, d, n6 = w_ada.shape
    bp = c_all.shape[0]
    tn = 1536
    return pl.pallas_call(
        _ada_kernel,
        grid=(depth, n6 // tn),
        in_specs=[
            pl.BlockSpec((bp, d), lambda l, j: (0, 0)),
            pl.BlockSpec((1, d, tn), lambda l, j: (l, 0, j)),
            pl.BlockSpec((1, 1, tn), lambda l, j: (l, 0, j)),
        ],
        out_specs=pl.BlockSpec((1, bp, tn), lambda l, j: (l, 0, j)),
        out_shape=jax.ShapeDtypeStruct((depth, bp, n6), F32),
        compiler_params=_cparams(("arbitrary", "arbitrary")),
        name="ada",
    )(c_all, w_ada, b_ada.reshape(depth, 1, n6))


def _modnorm(x, g, sh, sc):
    ms = jnp.mean(x * x, axis=-1, keepdims=True)
    y = x * lax.rsqrt(ms + EPS) * g
    return y * (1.0 + sc) + sh


def _rmsnorm2(x, g):
    ms = jnp.mean(x * x, axis=-1, keepdims=True)
    return x * lax.rsqrt(ms + EPS) * g


def _row_plan(b, t, rows):
    if t >= rows:
        assert t % rows == 0
        return 1, rows
    assert rows % t == 0 and b % (rows // t) == 0
    return rows // t, t


def _inproj_kernel(x_ref, g_ref, sh_ref, sc_ref, w_ref, o_ref, *, col_chunk):
    gg, tg, d = x_ref.shape
    h = _modnorm(x_ref[...], g_ref[...], sh_ref[...], sc_ref[...])
    hb = h.reshape(gg * tg, d).astype(BF16)
    n = w_ref.shape[1]
    for c in range(0, n, col_chunk):
        o_ref[:, c:c + col_chunk] = _dot(hb, w_ref[:, c:c + col_chunk])


def _inproj(x3, g, sh, sc, w, rows):
    b, t, d = x3.shape
    gg, tg = _row_plan(b, t, rows)
    nj = t // tg
    n = w.shape[1]
    return pl.pallas_call(
        functools.partial(_inproj_kernel, col_chunk=512),
        grid=(b // gg, nj),
        in_specs=[
            pl.BlockSpec((gg, tg, d), lambda i, j: (i, j, 0)),
            pl.BlockSpec((1, d), lambda i, j: (0, 0)),
            pl.BlockSpec((gg, 1, d), lambda i, j: (i, 0, 0)),
            pl.BlockSpec((gg, 1, d), lambda i, j: (i, 0, 0)),
            pl.BlockSpec((d, n), lambda i, j: (0, 0)),
        ],
        out_specs=pl.BlockSpec((gg * tg, n), lambda i, j: (i * nj + j, 0)),
        out_shape=jax.ShapeDtypeStruct((b * t, n), F32),
        compiler_params=_cparams(("arbitrary", "arbitrary")),
        name="inproj",
    )(x3, g, sh, sc, w)


def _rope_group(z, cc, ss):
    return z * cc + pltpu.roll(z, HEAD_W - QK_ROPE, axis=1) * ss


def _attn_prep_kernel(cq_ref, ckv_ref, kr_ref, cc_ref, ss_ref, gq_ref, wq_ref, gkv_ref,
                      wk_ref, wv_ref, q_ref, k_ref, v_ref, kv_ref, kro_ref, *, transposed):
    cck = cc_ref[...]
    ss = ss_ref[...]
    lane = lax.broadcasted_iota(jnp.int32, cck.shape, 1)
    ccq = jnp.where(lane < QK_NOPE, 1.0, cck)
    qn = _rmsnorm2(cq_ref[...], gq_ref[...]).astype(BF16)
    ckv = _rmsnorm2(ckv_ref[...], gkv_ref[...])
    kv_ref[...] = ckv
    ckv_b = ckv.astype(BF16)
    kr = _rope_group(kr_ref[...], cck, ss)
    kro_ref[...] = kr
    v = _dot(ckv_b, wv_ref[...])
    rows = v.shape[0]
    if transposed:
        ones = jnp.ones((V_ROWS - V_HEAD, rows), F32)
        vt = v.T
        for h in range(N_HEADS):
            v_ref[0, h * V_ROWS:(h + 1) * V_ROWS, :] = jnp.concatenate(
                [vt[h * V_HEAD:(h + 1) * V_HEAD, :], ones], axis=0).astype(BF16)
    else:
        v_ref[...] = v.astype(BF16)
    for h in range(N_HEADS):
        sl = slice(h * HEAD_W, (h + 1) * HEAD_W)
        zq = _dot(qn, wq_ref[:, sl])
        qh = _rope_group(zq, ccq, ss) * Q_SCALE
        if transposed:
            q_ref[sl, :] = qh.T.astype(BF16)
        else:
            q_ref[:, sl] = qh.astype(BF16)
        k_ref[:, sl] = (_dot(ckv_b, wk_ref[:, sl]) + kr).astype(BF16)


def _attn_prep(z, cc, ss, gq, wq, gkv, wk, wv, rows, transposed):
    n = z.shape[0]
    npos = cc.shape[0] // rows
    row = lambda i: (i, 0)
    col = lambda i: (0, i)
    const = lambda i: (0, 0)
    if transposed:
        q_spec = pl.BlockSpec((N_HEADS * HEAD_W, rows), col)
        q_shape = jax.ShapeDtypeStruct((N_HEADS * HEAD_W, n), BF16)
        v_spec = pl.BlockSpec((1, N_HEADS * V_ROWS, rows), lambda i: (i, 0, 0))
        v_shape = jax.ShapeDtypeStruct((n // rows, N_HEADS * V_ROWS, rows), BF16)
    else:
        q_spec = pl.BlockSpec((rows, N_HEADS * HEAD_W), row)
        q_shape = jax.ShapeDtypeStruct((n, N_HEADS * HEAD_W), BF16)
        v_spec = pl.BlockSpec((rows, N_HEADS * V_HEAD), row)
        v_shape = jax.ShapeDtypeStruct((n, N_HEADS * V_HEAD), BF16)
    return pl.pallas_call(
        functools.partial(_attn_prep_kernel, transposed=transposed),
        grid=(n // rows,),
        in_specs=[
            pl.BlockSpec((rows, Q_LORA), lambda i: (i, COL_CQ // Q_LORA)),
            pl.BlockSpec((rows, KV_LORA), lambda i: (i, COL_CKV // KV_LORA)),
            pl.BlockSpec((rows, HEAD_W), lambda i: (i, COL_KR // HEAD_W)),
            pl.BlockSpec((rows, HEAD_W), lambda i: (i % npos, 0)),
            pl.BlockSpec((rows, HEAD_W), lambda i: (i % npos, 0)),
            pl.BlockSpec((1, Q_LORA), const),
            pl.BlockSpec(wq.shape, const),
            pl.BlockSpec((1, KV_LORA), const),
            pl.BlockSpec(wk.shape, const),
            pl.BlockSpec(wv.shape, const),
        ],
        out_specs=[
            q_spec,
            pl.BlockSpec((rows, N_HEADS * HEAD_W), row),
            v_spec,
            pl.BlockSpec((rows, KV_LORA), row),
            pl.BlockSpec((rows, HEAD_W), row),
        ],
        out_shape=[
            q_shape,
            jax.ShapeDtypeStruct((n, N_HEADS * HEAD_W), BF16),
            v_shape,
            jax.ShapeDtypeStruct((n, KV_LORA), F32),
            jax.ShapeDtypeStruct((n, HEAD_W), F32),
        ],
        compiler_params=_cparams(("arbitrary",)),
        name="attn_prep",
    )(z, z, z, cc, ss, gq, wq, gkv, wk, wv)


def _cache_expand_kernel(ckv_ref, kr_ref, wk_ref, wv_ref, place_ref, k_ref, v_ref):
    ckv_b = ckv_ref[...].astype(BF16)
    krp = _dot(kr_ref[...].astype(BF16), place_ref[...])
    v_ref[...] = _dot(ckv_b, wv_ref[...]).astype(BF16)
    for h in range(N_HEADS):
        sl = slice(h * HEAD_W, (h + 1) * HEAD_W)
        k_ref[:, sl] = (_dot(ckv_b, wk_ref[:, sl]) + krp).astype(BF16)


def _cache_expand(ckv, kr, wk, wv, place, rows):
    n = ckv.shape[0]
    row = lambda i: (i, 0)
    const = lambda i: (0, 0)
    return pl.pallas_call(
        _cache_expand_kernel,
        grid=(n // rows,),
        in_specs=[
            pl.BlockSpec((rows, KV_LORA), row),
            pl.BlockSpec((rows, QK_ROPE), row),
            pl.BlockSpec(wk.shape, const),
            pl.BlockSpec(wv.shape, const),
            pl.BlockSpec(place.shape, const),
        ],
        out_specs=[
            pl.BlockSpec((rows, N_HEADS * HEAD_W), row),
            pl.BlockSpec((rows, N_HEADS * V_HEAD), row),
        ],
        out_shape=[
            jax.ShapeDtypeStruct((n, N_HEADS * HEAD_W), BF16),
            jax.ShapeDtypeStruct((n, N_HEADS * V_HEAD), BF16),
        ],
        compiler_params=_cparams(("arbitrary",)),
        name="cache_expand",
    )(ckv, kr, wk, wv, place)


def _flash_step(qt_ref, k_ref, vt_ref, m_ref, acc_ref, masked, q0, k0):
    tk, tq = k_ref.shape[0], qt_ref.shape[1]
    if masked:
        kc = (k0 + lax.broadcasted_iota(jnp.int32, (tk, tq), 0)) // CHUNK
        qc = (q0 + lax.broadcasted_iota(jnp.int32, (tk, tq), 1)) // CHUNK
        keep = kc <= qc
    def scores(h):
        sl = slice(h * HEAD_W, (h + 1) * HEAD_W)
        s = _dot(k_ref[:, sl], qt_ref[sl, :])
        return jnp.where(keep, s, NEG_INF) if masked else s

    def stats(h, s):
        m_prev = m_ref[h]
        m_new = jnp.maximum(m_prev, jnp.max(s, axis=0, keepdims=True))
        m_ref[h] = m_new
        return m_new, jnp.exp2(m_prev - m_new)

    s = {0: scores(0)}
    if N_HEADS > 1:
        s[1] = scores(1)
    st = {0: stats(0, s[0])}
    for h in range(N_HEADS):
        vs = slice(h * V_ROWS, (h + 1) * V_ROWS)
        if h + 2 < N_HEADS:
            s[h + 2] = scores(h + 2)
        if h + 1 < N_HEADS:
            st[h + 1] = stats(h + 1, s[h + 1])
        m_new, alpha = st.pop(h)
        p = jnp.exp2(s.pop(h) - m_new).astype(BF16)
        vt_h = jnp.concatenate([vt_ref[j, vs, :] for j in range(vt_ref.shape[0])], axis=1)
        acc_ref[h] = alpha * acc_ref[h] + _dot(vt_h, p)


def _flash_kernel(qi_tab, ki_tab, qt_ref, k_ref, vt_ref, o_ref, m_ref, acc_ref, *, ratio):
    step = pl.program_id(1)
    qi = qi_tab[step]
    ki = ki_tab[step]
    tk, tq = k_ref.shape[0], qt_ref.shape[1]

    @pl.when(ki == 0)
    def _():
        m_ref[...] = jnp.full(m_ref.shape, NEG_INF, F32)
        acc_ref[...] = jnp.zeros(acc_ref.shape, F32)

    @pl.when(ki < qi * ratio)
    def _():
        _flash_step(qt_ref, k_ref, vt_ref, m_ref, acc_ref, False, 0, 0)

    @pl.when(ki >= qi * ratio)
    def _():
        _flash_step(qt_ref, k_ref, vt_ref, m_ref, acc_ref, True, qi * tq, ki * tk)

    @pl.when(ki == (qi + 1) * ratio - 1)
    def _():
        for h in range(N_HEADS):
            a = acc_ref[h]
            o = a[:V_HEAD, :] / a[V_HEAD:V_HEAD + 1, :]
            o_ref[:, h * V_HEAD:(h + 1) * V_HEAD] = o.T.astype(o_ref.dtype)


def _flash_prompt(qt, k, vt, b, t, tq, tk):
    nq, nk = t // tq, t // tk
    ratio = tq // tk
    vt_tiles = tk // vt.shape[2]
    pairs = [(qi, ki) for qi in range(nq) for ki in range((qi + 1) * ratio)]
    qi_tab = jnp.asarray([p[0] for p in pairs], jnp.int32)
    ki_tab = jnp.asarray([p[1] for p in pairs], jnp.int32)
    grid_spec = pltpu.PrefetchScalarGridSpec(
        num_scalar_prefetch=2,
        grid=(b, len(pairs)),
        in_specs=[
            pl.BlockSpec((N_HEADS * HEAD_W, tq), lambda bi, s, qtab, ktab: (0, bi * nq + qtab[s])),
            pl.BlockSpec((tk, N_HEADS * HEAD_W), lambda bi, s, qtab, ktab: (bi * nk + ktab[s], 0)),
            pl.BlockSpec((vt_tiles, N_HEADS * V_ROWS, vt.shape[2]),
                         lambda bi, s, qtab, ktab: (bi * nk + ktab[s], 0, 0)),
        ],
        out_specs=pl.BlockSpec((tq, N_HEADS * V_HEAD), lambda bi, s, qtab, ktab: (bi * nq + qtab[s], 0)),
        scratch_shapes=[
            pltpu.VMEM((N_HEADS, 1, tq), F32),
            pltpu.VMEM((N_HEADS, V_ROWS, tq), F32),
        ],
    )
    return pl.pallas_call(
        functools.partial(_flash_kernel, ratio=ratio),
        grid_spec=grid_spec,
        out_shape=jax.ShapeDtypeStruct((b * t, N_HEADS * V_HEAD), BF16),
        compiler_params=_cparams(("arbitrary", "arbitrary")),
        name="flash_prompt",
    )(qi_tab, ki_tab, qt, k, vt)


def _attn_sample_kernel(q_ref, kc_ref, vc_ref, kn_ref, vn_ref, o_ref):
    for h in range(N_HEADS):
        sl = slice(h * HEAD_W, (h + 1) * HEAD_W)
        vs = slice(h * V_HEAD, (h + 1) * V_HEAD)
        qh = q_ref[:, sl]
        sc = _dot_nt(qh, kc_ref[:, sl])
        sn = _dot_nt(qh, kn_ref[:, sl])
        m = jnp.maximum(jnp.max(sc, axis=1, keepdims=True), jnp.max(sn, axis=1, keepdims=True))
        pc = jnp.exp2(sc - m)
        pn = jnp.exp2(sn - m)
        den = jnp.sum(pc, axis=1, keepdims=True) + jnp.sum(pn, axis=1, keepdims=True)
        o = _dot(pc.astype(BF16), vc_ref[:, vs]) + _dot(pn.astype(BF16), vn_ref[:, vs])
        o_ref[:, vs] = (o / den).astype(o_ref.dtype)


def _attn_sample(q, kc, vc, kn, vn, b, t, past):
    return pl.pallas_call(
        _attn_sample_kernel,
        grid=(b,),
        in_specs=[
            pl.BlockSpec((t, N_HEADS * HEAD_W), lambda i: (i, 0)),
            pl.BlockSpec((past, N_HEADS * HEAD_W), lambda i: (i, 0)),
            pl.BlockSpec((past, N_HEADS * V_HEAD), lambda i: (i, 0)),
            pl.BlockSpec((t, N_HEADS * HEAD_W), lambda i: (i, 0)),
            pl.BlockSpec((t, N_HEADS * V_HEAD), lambda i: (i, 0)),
        ],
        out_specs=pl.BlockSpec((t, N_HEADS * V_HEAD), lambda i: (i, 0)),
        out_shape=jax.ShapeDtypeStruct((b * t, N_HEADS * V_HEAD), BF16),
        compiler_params=_cparams(("arbitrary",)),
        name="attn_sample",
    )(q, kc, vc, kn, vn)


def _mix_kernel(x_ref, gt_ref, ubch_ref, g_ref, attn_ref, pu_ref, pch_ref, hp_ref, hc_ref,
                wpool_ref, pscale_ref, convw_ref, wbr_ref, wout_ref,
                xo_ref, ptail_ref, ctail_ref, extp_ref, extc_ref, *, n_hist):
    gg, tg, d = x_ref.shape
    j = pl.program_id(1)
    first = j == 0

    ubch = ubch_ref[...]
    u = ubch[:, COL_U:COL_U + POOL_W].reshape(gg, tg, POOL_W)
    bgate = ubch[:, COL_B:COL_B + CONV_W].reshape(gg, tg, CONV_W)
    cu = (ubch[:, COL_C:COL_C + CONV_W] * ubch[:, COL_H:COL_H + CONV_W]).reshape(gg, tg, CONV_W)

    hist_p = jnp.where(first, hp_ref[...], pu_ref[...].reshape(1, HIST_ROWS, POOL_W))
    pch = pch_ref[...]
    prev_cu = (pch[:, :CONV_W] * pch[:, CONV_W:])[HIST_ROWS - CONV_HIST_ROWS:]
    hist_c = jnp.where(first, hc_ref[...], prev_cu.reshape(1, CONV_HIST_ROWS, CONV_W))

    extp_ref[:, :HIST_ROWS, :] = hist_p
    extp_ref[:, HIST_ROWS:, :] = u
    extc_ref[:, :CONV_HIST_ROWS, :] = hist_c
    extc_ref[:, CONV_HIST_ROWS:, :] = cu
    ptail_ref[...] = u[:, tg - HIST_ROWS:, :]
    ctail_ref[...] = cu[:, tg - CONV_HIST_ROWS:, :]

    tpos = j * tg + lax.broadcasted_iota(jnp.int32, (1, tg, 1), 1)
    pooled = []
    for gi, w in enumerate(POOL_WINDOWS):
        cs = slice(gi * POOL_GROUP, (gi + 1) * POOL_GROUP)
        acc = u[:, :, cs]
        for kk in range(1, w):
            acc = acc + extp_ref[:, HIST_ROWS - kk:HIST_ROWS - kk + tg, cs]
        cnt = jnp.minimum(tpos + 1 + n_hist, w).astype(F32)
        dd = acc / cnt - u[:, :, cs]
        pooled.append(_dot(dd.reshape(gg * tg, POOL_GROUP).astype(BF16), wpool_ref[gi]))
    pool = jnp.concatenate(pooled, axis=1) * pscale_ref[...]

    cw = convw_ref[...]
    yc = (cw[0:1, :] * extc_ref[:, CONV_HIST_ROWS - 2:CONV_HIST_ROWS - 2 + tg, :]
          + cw[1:2, :] * extc_ref[:, CONV_HIST_ROWS - 1:CONV_HIST_ROWS - 1 + tg, :]
          + cw[2:3, :] * cu)
    conv = (bgate * yc).reshape(gg * tg, CONV_W)

    gates = jax.nn.sigmoid(g_ref[...])
    mixed = (gates[:, 0:d] * _dot(attn_ref[...], wbr_ref[0])
             + gates[:, d:2 * d] * _dot(pool.astype(BF16), wbr_ref[1])
             + gates[:, 2 * d:3 * d] * _dot(conv.astype(BF16), wbr_ref[2]))
    out = _dot(mixed.astype(BF16), wout_ref[...])
    xo_ref[...] = x_ref[...] + gt_ref[...] * out.reshape(gg, tg, d)


def _mix(x3, gt, z, attn, hist_p, hist_c, wpool, pscale, convw, wbr, wout, rows, n_hist):
    b, t, d = x3.shape
    gg, tg = _row_plan(b, t, rows)
    nj = t // tg
    rb = lambda i, j: i * nj + j
    hpb = tg // HIST_ROWS

    def prev_rows(i, j):
        return jnp.maximum(rb(i, j) * hpb - 1, 0)

    const2 = lambda i, j: (0, 0)
    const3 = lambda i, j: (0, 0, 0)
    return pl.pallas_call(
        functools.partial(_mix_kernel, n_hist=n_hist),
        grid=(b // gg, nj),
        in_specs=[
            pl.BlockSpec((gg, tg, d), lambda i, j: (i, j, 0)),
            pl.BlockSpec((gg, 1, d), lambda i, j: (i, 0, 0)),
            pl.BlockSpec((gg * tg, COL_CQ), lambda i, j: (rb(i, j), 0)),
            pl.BlockSpec((gg * tg, 3 * d), lambda i, j: (rb(i, j), COL_G // (3 * d))),
            pl.BlockSpec((gg * tg, N_HEADS * V_HEAD), lambda i, j: (rb(i, j), 0)),
            pl.BlockSpec((HIST_ROWS, POOL_W), lambda i, j: (prev_rows(i, j), 0)),
            pl.BlockSpec((HIST_ROWS, 2 * CONV_W), lambda i, j: (prev_rows(i, j), COL_C // (2 * CONV_W))),
            pl.BlockSpec((gg, HIST_ROWS, POOL_W), lambda i, j: (i, 0, 0)),
            pl.BlockSpec((gg, CONV_HIST_ROWS, CONV_W), lambda i, j: (i, 0, 0)),
            pl.BlockSpec(wpool.shape, const3),
            pl.BlockSpec(pscale.shape, const2),
            pl.BlockSpec(convw.shape, const2),
            pl.BlockSpec(wbr.shape, const3),
            pl.BlockSpec(wout.shape, const2),
        ],
        out_specs=[
            pl.BlockSpec((gg, tg, d), lambda i, j: (i, j, 0)),
            pl.BlockSpec((gg, HIST_ROWS, POOL_W), lambda i, j: (i, 0, 0)),
            pl.BlockSpec((gg, CONV_HIST_ROWS, CONV_W), lambda i, j: (i, 0, 0)),
        ],
        out_shape=[
            jax.ShapeDtypeStruct((b, t, d), F32),
            jax.ShapeDtypeStruct((b, HIST_ROWS, POOL_W), F32),
            jax.ShapeDtypeStruct((b, CONV_HIST_ROWS, CONV_W), F32),
        ],
        scratch_shapes=[
            pltpu.VMEM((gg, HIST_ROWS + tg, POOL_W), F32),
            pltpu.VMEM((gg, CONV_HIST_ROWS + tg, CONV_W), F32),
        ],
        compiler_params=_cparams(("arbitrary", "arbitrary")),
        name="mix_merge",
    )(x3, gt, z, z, attn, z, z, hist_p, hist_c, wpool, pscale, convw, wbr, wout)


def _peer_query_kernel(x_ref, g_ref, sh_ref, sc_ref, wq_ref, keys_ref, ht_ref, s1_ref, s2_ref):
    gg, tg, d = x_ref.shape
    h = _modnorm(x_ref[...], g_ref[...], sh_ref[...], sc_ref[...]).reshape(gg * tg, d)
    ht_ref[0] = h.T.astype(BF16)
    hb = h.astype(BF16)
    half = D_KEY // 2
    k1 = keys_ref[0]
    k2 = keys_ref[1]
    def query(hh):
        return _dot(hb, wq_ref[:, hh * D_KEY:(hh + 1) * D_KEY]).astype(BF16)

    q_next = query(0)
    for hh in range(PEER_HEADS):
        q = q_next
        if hh + 1 < PEER_HEADS:
            q_next = query(hh + 1)
        s1_ref[hh, 0] = _dot_nt(k1, q[:, :half])
        s2_ref[hh, 0] = _dot_nt(k2, q[:, half:])


def _peer_query(x3, g, sh, sc, wq, keys, rows):
    b, t, d = x3.shape
    gg, tg = _row_plan(b, t, rows)
    nj = t // tg
    nt = b * t // rows
    tok = lambda i, j: (0, i * nj + j, 0, 0)
    return pl.pallas_call(
        _peer_query_kernel,
        grid=(b // gg, nj),
        in_specs=[
            pl.BlockSpec((gg, tg, d), lambda i, j: (i, j, 0)),
            pl.BlockSpec((1, d), lambda i, j: (0, 0)),
            pl.BlockSpec((gg, 1, d), lambda i, j: (i, 0, 0)),
            pl.BlockSpec((gg, 1, d), lambda i, j: (i, 0, 0)),
            pl.BlockSpec(wq.shape, lambda i, j: (0, 0)),
            pl.BlockSpec(keys.shape, lambda i, j: (0, 0, 0)),
        ],
        out_specs=[
            pl.BlockSpec((1, d, rows), lambda i, j: (i * nj + j, 0, 0)),
            pl.BlockSpec((PEER_HEADS, 1, N_KEYS, rows), tok),
            pl.BlockSpec((PEER_HEADS, 1, N_KEYS, rows), tok),
        ],
        out_shape=[
            jax.ShapeDtypeStruct((nt, d, rows), BF16),
            jax.ShapeDtypeStruct((PEER_HEADS, nt, N_KEYS, rows), F32),
            jax.ShapeDtypeStruct((PEER_HEADS, nt, N_KEYS, rows), F32),
        ],
        compiler_params=_cparams(("arbitrary", "arbitrary")),
        name="peer_query",
    )(x3, g, sh, sc, wq, keys)


def _top_extract(s, with_rank):
    nk, r = s.shape
    iota = lax.broadcasted_iota(jnp.int32, (nk, r), 0).astype(F32)
    iota_k = lax.broadcasted_iota(jnp.int32, (PEER_TOPK, r), 0)
    rank = jnp.full((nk, r), NOT_TOP, F32) if with_rank else None
    vals = jnp.zeros((PEER_TOPK, r), F32)
    idxs = jnp.zeros((PEER_TOPK, r), F32)
    x = s
    for it in range(PEER_TOPK):
        m = jnp.max(x, axis=0, keepdims=True)
        idx = jnp.min(jnp.where(x == m, iota, float(nk)), axis=0, keepdims=True)
        hit = iota == idx
        if with_rank:
            rank = jnp.where(hit, float(it), rank)
        x = jnp.where(hit, -jnp.inf, x)
        vals = jnp.where(iota_k == it, m, vals)
        idxs = jnp.where(iota_k == it, idx, idxs)
    return rank, vals, idxs


def _oddeven_mergesort_pairs(n):
    pairs = []

    def merge(lo, hi, step):
        nxt = step * 2
        if nxt < hi - lo:
            merge(lo, hi, nxt)
            merge(lo + step, hi, nxt)
            for i in range(lo + step, hi - step, nxt):
                pairs.append((i, i + step))
        else:
            pairs.append((lo, lo + step))

    def sort(lo, hi):
        if hi - lo >= 1:
            mid = lo + (hi - lo) // 2
            sort(lo, mid)
            sort(mid + 1, hi)
            merge(lo, hi, 1)

    sort(0, n - 1)
    return pairs


SUBLANES = 8
SORT_PAIRS = _oddeven_mergesort_pairs(N_KEYS // SUBLANES)


def _sorted_top(s):
    n = N_KEYS // SUBLANES
    cols = [s[SUBLANES * v:SUBLANES * (v + 1), :] for v in range(n)]

    def exchange(i, j):
        hi, lo = jnp.maximum(cols[i], cols[j]), jnp.minimum(cols[i], cols[j])
        cols[i], cols[j] = hi, lo

    for i, j in SORT_PAIRS:
        exchange(i, j)
    shift = SUBLANES // 2
    while shift >= 1:
        other = [pltpu.roll(c, shift, axis=0) for c in cols]
        cols = [jnp.maximum(cols[p], other[n - 1 - p]) for p in range(n)]
        dist = n // 2
        while dist >= 1:
            for p in range(n):
                if p & dist == 0:
                    exchange(p, p + dist)
            dist //= 2
        shift //= 2
    return cols


def _ranks_from_sorted(s, top):
    n = len(top)
    ranks = []
    n_sel = None
    for v in range(N_KEYS // SUBLANES):
        x = s[SUBLANES * v:SUBLANES * (v + 1), :]
        acc = jnp.where(top[0] > x, 1.0, 0.0)
        for a in range(1, n - 1):
            acc = acc + jnp.where(top[a] > x, 1.0, 0.0)
        sel = x >= top[n - 1]
        ranks.append(jnp.where(sel, acc, NOT_TOP))
        one = jnp.where(sel, 1.0, 0.0)
        n_sel = one if n_sel is None else n_sel + one
    n_sel = jnp.sum(n_sel, axis=0, keepdims=True)
    dup = jnp.where(top[0] == top[1], 1.0, 0.0)
    for a in range(1, n - 1):
        dup = jnp.maximum(dup, jnp.where(top[a] == top[a + 1], 1.0, 0.0))
    flag = jnp.maximum(jnp.max(dup, axis=0, keepdims=True), jnp.where(n_sel != float(n), 1.0, 0.0))
    return jnp.concatenate(ranks, axis=0), flag


def _stack_rows(top):
    r = top[0].shape[1]
    iota_k = lax.broadcasted_iota(jnp.int32, (PEER_TOPK, r), 0)
    out = jnp.zeros((PEER_TOPK, r), F32)
    for a, t in enumerate(top):
        out = jnp.where(iota_k == a, jnp.concatenate([t, t], axis=0), out)
    return out


def _peer_select_kernel(s1_ref, s2_ref, c_ref, lr_ref, rk2_ref, d_ref):
    refs = (s1_ref, s2_ref, c_ref, lr_ref, rk2_ref, d_ref)
    flag = None
    for h in range(s1_ref.shape[0]):
        f = _peer_select_sorted(h, *refs)
        flag = f if flag is None else jnp.maximum(flag, f)

    @pl.when(jnp.max(flag) > 0.0)
    def _():
        for h in range(s1_ref.shape[0]):
            _peer_select_head(h, *refs)


def _peer_select_sorted(h, s1_ref, s2_ref, c_ref, lr_ref, rk2_ref, d_ref):
    s1 = s1_ref[h, 0]
    s2 = s2_ref[h, 0]
    top1 = _sorted_top(s1)
    top2 = _sorted_top(s2)
    rank1, flag1 = _ranks_from_sorted(s1, top1)
    rank2, flag2 = _ranks_from_sorted(s2, top2)
    v1 = _stack_rows(top1)
    v2 = _stack_rows(top2)
    cnt, zsum = _merge_counts(v1, v2)
    lr = jnp.zeros(rank1.shape, F32)
    for a in range(PEER_TOPK):
        lr = jnp.where(rank1 == float(a), cnt[a:a + 1, :], lr)
    c_ref[h, 0] = jnp.exp(s1 - v1[0:1, :]) / zsum
    lr_ref[h, 0] = lr
    rk2_ref[h, 0] = rank2.astype(BF16)
    d_ref[h, 0] = jnp.exp(s2 - v2[0:1, :]).astype(BF16)
    return jnp.maximum(flag1, flag2)


def _merge_counts(v1, v2):
    r = v1.shape[1]
    iota_k = lax.broadcasted_iota(jnp.int32, (PEER_TOPK, r), 0).astype(F32)
    cnt = jnp.zeros((PEER_TOPK, r), F32)
    front = v1 + v2[0:1, :]
    top = front[0:1, :]
    zsum = jnp.zeros((1, r), F32)
    for _ in range(PEER_TOPK):
        m = jnp.max(front, axis=0, keepdims=True)
        a = jnp.min(jnp.where(front == m, iota_k, float(PEER_TOPK)), axis=0, keepdims=True)
        hit = iota_k == a
        zsum = zsum + jnp.exp(m - top)
        cnt = jnp.where(hit, cnt + 1.0, cnt)
        c_hit = jnp.max(jnp.where(hit, cnt, 0.0), axis=0, keepdims=True)
        v1_hit = jnp.max(jnp.where(hit, v1, -jnp.inf), axis=0, keepdims=True)
        nxt = jnp.full((1, r), -jnp.inf, F32)
        for bcol in range(1, PEER_TOPK):
            nxt = jnp.where(c_hit == float(bcol), v2[bcol:bcol + 1, :], nxt)
        front = jnp.where(hit, v1_hit + nxt, front)
    return cnt, zsum


def _peer_select_head(h, s1_ref, s2_ref, c_ref, lr_ref, rk2_ref, d_ref):
    s1 = s1_ref[h, 0]
    s2 = s2_ref[h, 0]
    nk, r = s1.shape
    _, v1, idx1 = _top_extract(s1, False)
    rank2, v2, _ = _top_extract(s2, True)
    cnt, zsum = _merge_counts(v1, v2)
    iota = lax.broadcasted_iota(jnp.int32, (nk, r), 0).astype(F32)
    lr = jnp.zeros((nk, r), F32)
    for a in range(PEER_TOPK):
        lr = jnp.where(iota == idx1[a:a + 1, :], cnt[a:a + 1, :], lr)
    c_ref[h, 0] = jnp.exp(s1 - v1[0:1, :]) / zsum
    lr_ref[h, 0] = lr
    rk2_ref[h, 0] = rank2.astype(BF16)
    d_ref[h, 0] = jnp.exp(s2 - v2[0:1, :]).astype(BF16)


def _peer_select(s1, s2):
    hh, nt, nk, lanes = s1.shape
    heads_per_step = 2
    spec = pl.BlockSpec((heads_per_step, 1, nk, lanes), lambda i, h: (h, i, 0, 0))
    f32 = jax.ShapeDtypeStruct(s1.shape, F32)
    b16 = jax.ShapeDtypeStruct(s1.shape, BF16)
    return pl.pallas_call(
        _peer_select_kernel,
        grid=(nt, hh // heads_per_step),
        in_specs=[spec, spec],
        out_specs=[spec, spec, spec, spec],
        out_shape=[f32, f32, b16, b16],
        compiler_params=_cparams(("arbitrary", "arbitrary")),
        name="peer_select",
    )(s1, s2)


def _gelu(x):
    return 0.5 * x * (1.0 + lax.erf(x * (2.0 ** -0.5)))


def _peer_dense_kernel(x_ref, gt_ref, ht_ref, u_ref, vt_ref, c_ref, lr_ref, rk2_ref, d_ref,
                       xo_ref, a_ref, w_ref, acc_ref, *, rows_per_chunk):
    gg, tg, d = x_ref.shape
    nt, _, lanes = ht_ref.shape
    e = pl.program_id(1)
    slab = 16
    n_slab = N_KEYS // slab

    gsz = acc_ref.shape[2] // lanes
    ng = nt // gsz

    @pl.when(e == 0)
    def _():
        acc_ref[...] = jnp.zeros(acc_ref.shape, F32)

    w_ref[(ng - 1) % 2] = jnp.zeros(w_ref.shape[1:], BF16)

    def group(g):
        prev = (g + ng - 1) % ng
        rhs = jnp.concatenate([ht_ref[g * gsz + k] for k in range(gsz)], axis=1)
        w_prev = w_ref.at[prev % 2]
        w_cur = w_ref.at[g % 2]
        ec = a_ref.shape[0]
        n_part = 1
        pe, pd = ec // n_part, d // n_part
        ii_per_part = rows_per_chunk // n_part

        def key_part(q):
            a_ref[q * pe:(q + 1) * pe, :] = _dot(u_ref[q * pe:(q + 1) * pe, :], rhs)

        def value_part(q):
            acc_ref[prev, q * pd:(q + 1) * pd, :] += _dot(vt_ref[q * pd:(q + 1) * pd, :], w_prev[...])

        key_part(0)
        for q in range(n_part):
            if q + 1 < n_part:
                key_part(q + 1)
            value_part(q)
            for k in range(gsz):
                lt = g * gsz + k
                ls = slice(k * lanes, (k + 1) * lanes)
                for ii in range(q * ii_per_part, (q + 1) * ii_per_part):
                    cl = []
                    for hh in range(PEER_HEADS):
                        cl.append((
                            jnp.broadcast_to(c_ref[hh, lt, ii:ii + 1, :], (slab, lanes)).astype(BF16),
                            jnp.broadcast_to(lr_ref[hh, lt, ii:ii + 1, :], (slab, lanes)).astype(BF16)))
                    for jv in range(n_slab):
                        rs = slice(jv * slab, (jv + 1) * slab)
                        gate = None
                        for hh in range(PEER_HEADS):
                            dd = d_ref[hh, lt, rs, :]
                            term = cl[hh][0] * jnp.where(rk2_ref[hh, lt, rs, :] < cl[hh][1], dd,
                                                         jnp.zeros_like(dd))
                            gate = term if gate is None else gate + term
                        rows = slice(ii * N_KEYS + jv * slab, ii * N_KEYS + (jv + 1) * slab)
                        w_cur[rows, ls] = _gelu(a_ref[rows, ls]).astype(BF16) * gate

    for g in range(ng):
        group(g)
    acc_ref[ng - 1] += _dot(vt_ref[...], w_ref[(ng - 1) % 2])

    @pl.when(e == pl.num_programs(1) - 1)
    def _():
        for g in range(ng):
            for k in range(gsz):
                lt = g * gsz + k
                upd = acc_ref[g, :, k * lanes:(k + 1) * lanes].T
                if gg == 1:
                    rows = slice(lt * lanes, (lt + 1) * lanes)
                    xo_ref[0, rows, :] = x_ref[0, rows, :] + gt_ref[0] * upd
                else:
                    per = lanes // tg
                    rows = slice(lt * per, (lt + 1) * per)
                    xo_ref[rows] = x_ref[rows] + gt_ref[rows] * upd.reshape(per, tg, d)


def _peer_dense(x3, gt, ht, u, vt, c, lr, rk2, dd, tiles_per_block, rows_per_chunk):
    b, t, d = x3.shape
    nt_all, _, lanes = ht.shape
    gg, tg = _row_plan(b, t, tiles_per_block * lanes)
    nj = t // tg
    ne = u.shape[0]
    ec = rows_per_chunk * N_KEYS
    gsz = 1
    full = pl.BlockSpec((PEER_HEADS, tiles_per_block, N_KEYS, lanes), lambda i, e: (0, i, 0, 0))
    part = pl.BlockSpec((PEER_HEADS, tiles_per_block, rows_per_chunk, lanes),
                        lambda i, e: (0, i, e, 0))
    return pl.pallas_call(
        functools.partial(_peer_dense_kernel, rows_per_chunk=rows_per_chunk),
        grid=(nt_all // tiles_per_block, ne // ec),
        in_specs=[
            pl.BlockSpec((gg, tg, d), lambda i, e: (i // nj, i % nj, 0)),
            pl.BlockSpec((gg, 1, d), lambda i, e: (i // nj, 0, 0)),
            pl.BlockSpec((tiles_per_block, d, lanes), lambda i, e: (i, 0, 0)),
            pl.BlockSpec((ec, d), lambda i, e: (e, 0)),
            pl.BlockSpec((d, ec), lambda i, e: (0, e)),
            part, part, full, full,
        ],
        out_specs=pl.BlockSpec((gg, tg, d), lambda i, e: (i // nj, i % nj, 0)),
        out_shape=jax.ShapeDtypeStruct((b, t, d), F32),
        scratch_shapes=[
            pltpu.VMEM((ec, gsz * lanes), F32),
            pltpu.VMEM((2, ec, gsz * lanes), BF16),
            pltpu.VMEM((tiles_per_block // gsz, d, gsz * lanes), F32),
        ],
        compiler_params=_cparams(("arbitrary", "arbitrary")),
        name="peer_dense",
    )(x3, gt, ht, u, vt, c, lr, rk2, dd)


def _final_kernel(x_ref, g_ref, o_ref):
    o_ref[...] = _rmsnorm2(x_ref[...], g_ref[...])


def _final_norm(x3, g, rows):
    b, t, d = x3.shape
    x2 = x3.reshape(b * t, d)
    y = pl.pallas_call(
        _final_kernel,
        grid=(b * t // rows,),
        in_specs=[pl.BlockSpec((rows, d), lambda i: (i, 0)), pl.BlockSpec((1, d), lambda i: (0, 0))],
        out_specs=pl.BlockSpec((rows, d), lambda i: (i, 0)),
        out_shape=jax.ShapeDtypeStruct((b * t, d), F32),
        compiler_params=_cparams(("arbitrary",)),
        name="final_norm",
    )(x2, g)
    return y.reshape(b, t, d)


def _swap_halves(w):
    half = w.shape[-1] // 2
    return jnp.concatenate([w[..., half:], w[..., :half]], axis=-1)


def _prep_layer(p, l):
    d = p['w_in'].shape[1]
    w_in = p['w_in'][l]
    offs = [0]
    for nsz in (Q_LORA, KV_LORA, QK_ROPE, POOL_W, CONV_W, CONV_W, CONV_W, 3 * d):
        offs.append(offs[-1] + nsz)
    w_cq, w_ckv, w_kr, w_u, w_b, w_c, w_h, w_g = [w_in[:, offs[i]:offs[i + 1]] for i in range(8)]
    zeros = lambda n: jnp.zeros((d, n), w_in.dtype)
    w_krg = jnp.concatenate([zeros(QK_NOPE), w_kr, _swap_halves(w_kr)], axis=1)
    w_in_r = jnp.concatenate(
        [w_u, w_b, w_c, w_h, w_cq, w_ckv, w_krg, zeros(COL_G - COL_KR - HEAD_W), w_g], axis=1)
    assert w_in_r.shape[1] == D_IN_PAD

    w_uq = p['w_uq'][l]
    wq_r = jnp.concatenate(
        [w_uq[..., :QK_NOPE], w_uq[..., QK_NOPE:], _swap_halves(w_uq[..., QK_NOPE:])], axis=-1)
    wq_r = wq_r.reshape(Q_LORA, N_HEADS * HEAD_W)
    w_ukv = p['w_ukv'][l]
    wk_r = jnp.concatenate(
        [w_ukv[..., :QK_NOPE], jnp.zeros((KV_LORA, N_HEADS, HEAD_W - QK_NOPE), w_ukv.dtype)],
        axis=-1).reshape(KV_LORA, N_HEADS * HEAD_W)
    wv_r = w_ukv[..., QK_NOPE:].reshape(KV_LORA, N_HEADS * V_HEAD)
    return dict(
        w_in=w_in_r.astype(BF16), wq=wq_r.astype(BF16), wk=wk_r.astype(BF16), wv=wv_r.astype(BF16),
        g_mix=p['g_mix'][l][None, :], g_q=p['g_q'][l][None, :], g_kv=p['g_kv'][l][None, :],
        w_pool=p['w_pool'][l].astype(BF16), pool_scale=p['pool_scale'][l][None, :],
        conv_w=jnp.pad(p['conv_w'][l], ((0, 8 - CONV_K), (0, 0))),
        w_branch=p['w_branch'][l].astype(BF16), w_out=p['w_out'][l].astype(BF16),
        g_ffn=p['g_ffn'][l][None, :],
        peer_wq=p['peer_wq'][l].reshape(d, PEER_HEADS * D_KEY).astype(BF16),
        peer_keys=p['peer_keys'][l].astype(BF16),
        peer_u=p['peer_u'][l].astype(BF16),
        peer_vt=p['peer_v'][l].T.astype(BF16),
    )


def _rope_tables(pos):
    half = QK_ROPE // 2
    inv = ROPE_THETA ** (-jnp.arange(half, dtype=F32) / half)
    ang = pos.astype(F32)[:, None] * inv[None, :]
    cos, sin = jnp.cos(ang), jnp.sin(ang)
    z = lambda n: jnp.zeros((pos.shape[0], n), F32)
    cc = jnp.concatenate([z(QK_NOPE), cos, cos, z(QK_ROPE)], axis=1)
    ss = jnp.concatenate([z(QK_NOPE), -sin, sin, z(QK_ROPE)], axis=1)
    return cc, ss


def _trunk(x, mods, pos, n_hist, cache, hist_pool, hist_conv, layers, g_final, cfg):
    b, t, d = x.shape
    rows = cfg['rows']
    n = b * t
    cc, ss = _rope_tables(pos)
    if t < rows:
        cc = jnp.tile(cc, (rows // t, 1))
        ss = jnp.tile(ss, (rows // t, 1))
    place = jnp.concatenate(
        [jnp.zeros((QK_ROPE, QK_NOPE), F32), jnp.eye(QK_ROPE, dtype=F32),
         jnp.zeros((QK_ROPE, HEAD_W - QK_NOPE - QK_ROPE), F32)], axis=1).astype(BF16)
    new_kv, new_kr, new_pool, new_conv = [], [], [], []
    for l, lw in enumerate(layers):
        mod = mods[l].reshape(b, 1, 6 * d)
        sh1, sc1, gt1, sh2, sc2, gt2 = [mod[:, :, i * d:(i + 1) * d] for i in range(6)]
        z = _inproj(x, lw['g_mix'], sh1, sc1, lw['w_in'], rows)
        q, k, v, ckv, krg = _attn_prep(z, cc, ss, lw['g_q'], lw['wq'], lw['g_kv'], lw['wk'],
                                       lw['wv'], rows, cache is None)
        new_kv.append(ckv.reshape(b, t, KV_LORA))
        new_kr.append(krg[:, QK_NOPE:QK_NOPE + QK_ROPE].reshape(b, t, QK_ROPE))
        if cache is None:
            attn = _flash_prompt(q, k, v, b, t, cfg['attn_tq'], cfg['attn_tk'])
        else:
            ckv_c, kr_c = cache
            past = ckv_c.shape[2]
            kc, vc = _cache_expand(ckv_c[l].reshape(b * past, KV_LORA),
                                   kr_c[l].reshape(b * past, QK_ROPE),
                                   lw['wk'], lw['wv'], place, cfg['cache_rows'])
            attn = _attn_sample(q, kc, vc, k, v, b, t, past)
        hp = jnp.pad(hist_pool[l], ((0, 0), (HIST_ROWS - POOL_HIST, 0), (0, 0)))
        hc = jnp.pad(hist_conv[l], ((0, 0), (CONV_HIST_ROWS - (CONV_K - 1), 0), (0, 0)))
        x, ptail, ctail = _mix(x, gt1, z, attn, hp, hc, lw['w_pool'], lw['pool_scale'],
                               lw['conv_w'], lw['w_branch'], lw['w_out'], rows, n_hist)
        new_pool.append(ptail[:, HIST_ROWS - POOL_HIST:, :])
        new_conv.append(ctail[:, CONV_HIST_ROWS - (CONV_K - 1):, :])
        ht, s1, s2 = _peer_query(x, lw['g_ffn'], sh2, sc2, lw['peer_wq'], lw['peer_keys'], rows)
        c, lr, rk2, dd = _peer_select(s1, s2)
        x = _peer_dense(x, gt2, ht, lw['peer_u'], lw['peer_vt'], c, lr, rk2, dd,
                        min(cfg['peer_tiles'], n // rows), cfg['rows_per_chunk'])
    y = _final_norm(x, g_final[None, :], rows)
    return y, jnp.stack(new_kv), jnp.stack(new_kr), jnp.stack(new_pool), jnp.stack(new_conv)


def _config(t_prompt):
    rows = min(256, t_prompt)
    return dict(rows=rows, attn_tq=min(512, t_prompt), attn_tk=min(512, t_prompt), cache_rows=256, peer_tiles=4,
                rows_per_chunk=8)


def kernel(x_prompt, x_sample, cache_kv_latent, cache_k_rope, state_pool, state_conv,
           c_prompt, c_sample, w_ada, b_ada, g_mix, w_in, g_q, w_uq, g_kv, w_ukv,
           w_pool, pool_scale, conv_w, w_branch, w_out, g_ffn, peer_wq, peer_keys,
           peer_u, peer_v, g_final):
    p = {'w_in': w_in, 'g_mix': g_mix, 'g_q': g_q, 'w_uq': w_uq, 'g_kv': g_kv, 'w_ukv': w_ukv,
         'w_pool': w_pool, 'pool_scale': pool_scale, 'conv_w': conv_w, 'w_branch': w_branch,
         'w_out': w_out, 'g_ffn': g_ffn, 'peer_wq': peer_wq, 'peer_keys': peer_keys,
         'peer_u': peer_u, 'peer_v': peer_v}
    depth = w_ada.shape[0]
    bp, tp, d = x_prompt.shape
    bs, ts, _ = x_sample.shape
    past = cache_kv_latent.shape[2]
    layers = [_prep_layer(p, l) for l in range(depth)]

    c_all = jnp.concatenate([c_prompt, c_sample], axis=0)
    pad = (-c_all.shape[0]) % 8
    c_all = jnp.pad(c_all, ((0, pad), (0, 0)))
    mods = _ada(c_all, w_ada, b_ada)
    mods_p, mods_s = mods[:, :bp], mods[:, bp:bp + bs]

    cfg = _config(tp)
    zp = jnp.zeros((depth, bp, POOL_HIST, POOL_W), x_prompt.dtype)
    zc = jnp.zeros((depth, bp, CONV_K - 1, CONV_W), x_prompt.dtype)
    y_p, p_kv, p_kr, p_pool, p_conv = _trunk(
        x_prompt, mods_p, jnp.arange(tp), 0, None, zp, zc, layers, g_final, cfg)
    y_s, s_kv, s_kr, s_pool, s_conv = _trunk(
        x_sample, mods_s, past + jnp.arange(ts), min(past, POOL_HIST),
        (cache_kv_latent, cache_k_rope), state_pool, state_conv, layers, g_final, cfg)
    return (y_p, y_s, p_kv, p_kr, p_pool, p_conv, s_kv, s_kr, s_pool, s_conv)
```

```python
import functools
import math

import jax
import jax.numpy as jnp
from jax import lax
from jax.experimental import pallas as pl
from jax.experimental.pallas import tpu as pltpu

F32 = jnp.float32
BF16 = jnp.bfloat16

EPS = 1e-6
N_HEADS = 8
QK_NOPE = 64
QK_ROPE = 32
V_HEAD = 64
Q_LORA = 512
KV_LORA = 256
ROPE_THETA = 10000.0
CHUNK = 64
ATTN_SCALE = (QK_NOPE + QK_ROPE) ** -0.5
NEG_INF = -1e30
POOL_WINDOWS = (2, 4, 8, 16)
POOL_GROUP = 128
POOL_W = 512
POOL_HIST = 15
CONV_W = 512
CONV_K = 3
PEER_HEADS = 8
N_KEYS = 128
D_KEY = 256
PEER_TOPK = 16
NOT_TOP = 99.0

LANES = 128
VMEM_LIMIT_BYTES = 56 * 2**20

HEAD_W = LANES
V_ROWS = V_HEAD + 16
Q_SCALE = ATTN_SCALE * math.log2(math.e)
HIST_ROWS = 16
CONV_HIST_ROWS = 8

COL_U, COL_B, COL_C, COL_H = 0, 512, 1024, 1536
COL_CQ = 2048
COL_CKV = 2560
COL_KR = 2816
COL_G = 3072
D_IN_PAD = 6144


def _cparams(sem):
    return pltpu.CompilerParams(dimension_semantics=sem, vmem_limit_bytes=VMEM_LIMIT_BYTES)


def _dot(a, b):
    return jnp.dot(a, b, preferred_element_type=F32)


def _dot_nt(a, b):
    return lax.dot_general(a, b, (((1,), (1,)), ((), ())), preferred_element_type=F32)


def _ada_kernel(c_ref, w_ref, b_ref, o_ref):
    c = c_ref[...]
    act = c * jax.nn.sigmoid(c)
    o_ref[0] = _dot(act.astype(BF16), w_ref[0].astype(BF16)) + b_ref[0]


def _ada(c_all, w_ada, b_ada):
    depth, d, n6 = w_ada.shape
    bp = c_all.shape[0]
    tn = 1536
    return pl.pallas_call(
        _ada_kernel,
        grid=(depth, n6 // tn),
        in_specs=[
            pl.BlockSpec((bp, d), lambda l, j: (0, 0)),
            pl.BlockSpec((1, d, tn), lambda l, j: (l, 0, j)),
            pl.BlockSpec((1, 1, tn), lambda l, j: (l, 0, j)),
        ],
        out_specs=pl.BlockSpec((1, bp, tn), lambda l, j: (l, 0, j)),
        out_shape=jax.ShapeDtypeStruct((depth, bp, n6), F32),
        compiler_params=_cparams(("arbitrary", "arbitrary")),
        name="ada",
    )(c_all, w_ada, b_ada.reshape(depth, 1, n6))


def _modnorm(x, g, sh, sc):
    ms = jnp.mean(x * x, axis=-1, keepdims=True)
    y = x * lax.rsqrt(ms + EPS) * g
    return y * (1.0 + sc) + sh


def _rmsnorm2(x, g):
    ms = jnp.mean(x * x, axis=-1, keepdims=True)
    return x * lax.rsqrt(ms + EPS) * g


def _row_plan(b, t, rows):
    if t >= rows:
        assert t % rows == 0
        return 1, rows
    assert rows % t == 0 and b % (rows // t) == 0
    return rows // t, t


def _inproj_kernel(x_ref, g_ref, sh_ref, sc_ref, w_ref, o_ref, *, col_chunk):
    gg, tg, d = x_ref.shape
    h = _modnorm(x_ref[...], g_ref[...], sh_ref[...], sc_ref[...])
    hb = h.reshape(gg * tg, d).astype(BF16)
    n = w_ref.shape[1]
    for c in range(0, n, col_chunk):
        o_ref[:, c:c + col_chunk] = _dot(hb, w_ref[:, c:c + col_chunk]).astype(o_ref.dtype)


def _inproj(x3, g, sh, sc, w, rows):
    b, t, d = x3.shape
    gg, tg = _row_plan(b, t, rows)
    nj = t // tg
    n = w.shape[1]
    return pl.pallas_call(
        functools.partial(_inproj_kernel, col_chunk=512),
        grid=(b // gg, nj),
        in_specs=[
            pl.BlockSpec((gg, tg, d), lambda i, j: (i, j, 0)),
            pl.BlockSpec((1, d), lambda i, j: (0, 0)),
            pl.BlockSpec((gg, 1, d), lambda i, j: (i, 0, 0)),
            pl.BlockSpec((gg, 1, d), lambda i, j: (i, 0, 0)),
            pl.BlockSpec((d, n), lambda i, j: (0, 0)),
        ],
        out_specs=pl.BlockSpec((gg * tg, n), lambda i, j: (i * nj + j, 0)),
        out_shape=jax.ShapeDtypeStruct((b * t, n), BF16),
        compiler_params=_cparams(("arbitrary", "arbitrary")),
        name="inproj",
    )(x3, g, sh, sc, w)


def _rope_group(z, cc, ss):
    return z * cc + pltpu.roll(z, HEAD_W - QK_ROPE, axis=1) * ss


def _attn_prep_kernel(cq_ref, ckv_ref, kr_ref, cc_ref, ss_ref, gq_ref, wq_ref, gkv_ref,
                      wk_ref, wv_ref, q_ref, k_ref, v_ref, kv_ref, kro_ref, *, transposed):
    cck = cc_ref[...]
    ss = ss_ref[...]
    lane = lax.broadcasted_iota(jnp.int32, cck.shape, 1)
    ccq = jnp.where(lane < QK_NOPE, 1.0, cck)
    qn = _rmsnorm2(cq_ref[...].astype(F32), gq_ref[...]).astype(BF16)
    ckv = _rmsnorm2(ckv_ref[...].astype(F32), gkv_ref[...])
    kv_ref[...] = ckv
    ckv_b = ckv.astype(BF16)
    kr = _rope_group(kr_ref[...].astype(F32), cck, ss)
    kro_ref[...] = kr
    v = _dot(ckv_b, wv_ref[...])
    rows = v.shape[0]
    if transposed:
        ones = jnp.ones((V_ROWS - V_HEAD, rows), F32)
        vt = v.T
        for h in range(N_HEADS):
            v_ref[0, h * V_ROWS:(h + 1) * V_ROWS, :] = jnp.concatenate(
                [vt[h * V_HEAD:(h + 1) * V_HEAD, :], ones], axis=0).astype(BF16)
    else:
        v_ref[...] = v.astype(BF16)
    for h in range(N_HEADS):
        sl = slice(h * HEAD_W, (h + 1) * HEAD_W)
        zq = _dot(qn, wq_ref[:, sl])
        qh = _rope_group(zq, ccq, ss) * Q_SCALE
        if transposed:
            q_ref[sl, :] = qh.T.astype(BF16)
        else:
            q_ref[:, sl] = qh.astype(BF16)
        k_ref[:, sl] = (_dot(ckv_b, wk_ref[:, sl]) + kr).astype(BF16)


def _attn_prep(z, cc, ss, gq, wq, gkv, wk, wv, rows, transposed):
    n = z.shape[0]
    npos = cc.shape[0] // rows
    row = lambda i: (i, 0)
    col = lambda i: (0, i)
    const = lambda i: (0, 0)
    if transposed:
        q_spec = pl.BlockSpec((N_HEADS * HEAD_W, rows), col)
        q_shape = jax.ShapeDtypeStruct((N_HEADS * HEAD_W, n), BF16)
        v_spec = pl.BlockSpec((1, N_HEADS * V_ROWS, rows), lambda i: (i, 0, 0))
        v_shape = jax.ShapeDtypeStruct((n // rows, N_HEADS * V_ROWS, rows), BF16)
    else:
        q_spec = pl.BlockSpec((rows, N_HEADS * HEAD_W), row)
        q_shape = jax.ShapeDtypeStruct((n, N_HEADS * HEAD_W), BF16)
        v_spec = pl.BlockSpec((rows, N_HEADS * V_HEAD), row)
        v_shape = jax.ShapeDtypeStruct((n, N_HEADS * V_HEAD), BF16)
    return pl.pallas_call(
        functools.partial(_attn_prep_kernel, transposed=transposed),
        grid=(n // rows,),
        in_specs=[
            pl.BlockSpec((rows, Q_LORA), lambda i: (i, COL_CQ // Q_LORA)),
            pl.BlockSpec((rows, KV_LORA), lambda i: (i, COL_CKV // KV_LORA)),
            pl.BlockSpec((rows, HEAD_W), lambda i: (i, COL_KR // HEAD_W)),
            pl.BlockSpec((rows, HEAD_W), lambda i: (i % npos, 0)),
            pl.BlockSpec((rows, HEAD_W), lambda i: (i % npos, 0)),
            pl.BlockSpec((1, Q_LORA), const),
            pl.BlockSpec(wq.shape, const),
            pl.BlockSpec((1, KV_LORA), const),
            pl.BlockSpec(wk.shape, const),
            pl.BlockSpec(wv.shape, const),
        ],
        out_specs=[
            q_spec,
            pl.BlockSpec((rows, N_HEADS * HEAD_W), row),
            v_spec,
            pl.BlockSpec((rows, KV_LORA), row),
            pl.BlockSpec((rows, HEAD_W), row),
        ],
        out_shape=[
            q_shape,
            jax.ShapeDtypeStruct((n, N_HEADS * HEAD_W), BF16),
            v_shape,
            jax.ShapeDtypeStruct((n, KV_LORA), F32),
            jax.ShapeDtypeStruct((n, HEAD_W), F32),
        ],
        compiler_params=_cparams(("arbitrary",)),
        name="attn_prep",
    )(z, z, z, cc, ss, gq, wq, gkv, wk, wv)


def _cache_expand_kernel(ckv_ref, kr_ref, wk_ref, wv_ref, place_ref, k_ref, v_ref):
    ckv_b = ckv_ref[...].astype(BF16)
    krp = _dot(kr_ref[...].astype(BF16), place_ref[...])
    v_ref[...] = _dot(ckv_b, wv_ref[...]).astype(BF16)
    for h in range(N_HEADS):
        sl = slice(h * HEAD_W, (h + 1) * HEAD_W)
        k_ref[:, sl] = (_dot(ckv_b, wk_ref[:, sl]) + krp).astype(BF16)


def _cache_expand(ckv, kr, wk, wv, place, rows):
    n = ckv.shape[0]
    row = lambda i: (i, 0)
    const = lambda i: (0, 0)
    return pl.pallas_call(
        _cache_expand_kernel,
        grid=(n // rows,),
        in_specs=[
            pl.BlockSpec((rows, KV_LORA), row),
            pl.BlockSpec((rows, QK_ROPE), row),
            pl.BlockSpec(wk.shape, const),
            pl.BlockSpec(wv.shape, const),
            pl.BlockSpec(place.shape, const),
        ],
        out_specs=[
            pl.BlockSpec((rows, N_HEADS * HEAD_W), row),
            pl.BlockSpec((rows, N_HEADS * V_HEAD), row),
        ],
        out_shape=[
            jax.ShapeDtypeStruct((n, N_HEADS * HEAD_W), BF16),
            jax.ShapeDtypeStruct((n, N_HEADS * V_HEAD), BF16),
        ],
        compiler_params=_cparams(("arbitrary",)),
        name="cache_expand",
    )(ckv, kr, wk, wv, place)


def _flash_step(qt_ref, k_ref, vt_ref, m_ref, acc_ref, masked, q0, k0):
    tk, tq = k_ref.shape[0], qt_ref.shape[1]
    if masked:
        kc = (k0 + lax.broadcasted_iota(jnp.int32, (tk, tq), 0)) // CHUNK
        qc = (q0 + lax.broadcasted_iota(jnp.int32, (tk, tq), 1)) // CHUNK
        keep = kc <= qc
    def scores(h):
        sl = slice(h * HEAD_W, (h + 1) * HEAD_W)
        s = _dot(k_ref[:, sl], qt_ref[sl, :])
        return jnp.where(keep, s, NEG_INF) if masked else s

    def stats(h, s):
        m_prev = m_ref[h]
        m_new = jnp.maximum(m_prev, jnp.max(s, axis=0, keepdims=True))
        m_ref[h] = m_new
        return m_new, jnp.exp2(m_prev - m_new)

    s = {0: scores(0)}
    if N_HEADS > 1:
        s[1] = scores(1)
    st = {0: stats(0, s[0])}
    for h in range(N_HEADS):
        vs = slice(h * V_ROWS, (h + 1) * V_ROWS)
        if h + 2 < N_HEADS:
            s[h + 2] = scores(h + 2)
        if h + 1 < N_HEADS:
            st[h + 1] = stats(h + 1, s[h + 1])
        m_new, alpha = st.pop(h)
        p = jnp.exp2(s.pop(h) - m_new).astype(BF16)
        vt_h = jnp.concatenate([vt_ref[j, vs, :] for j in range(vt_ref.shape[0])], axis=1)
        acc_ref[h] = alpha * acc_ref[h] + _dot(vt_h, p)


def _flash_kernel(qi_tab, ki_tab, qt_ref, k_ref, vt_ref, o_ref, m_ref, acc_ref, *, ratio):
    step = pl.program_id(1)
    qi = qi_tab[step]
    ki = ki_tab[step]
    tk, tq = k_ref.shape[0], qt_ref.shape[1]

    @pl.when(ki == 0)
    def _():
        m_ref[...] = jnp.full(m_ref.shape, NEG_INF, F32)
        acc_ref[...] = jnp.zeros(acc_ref.shape, F32)

    @pl.when(ki < qi * ratio)
    def _():
        _flash_step(qt_ref, k_ref, vt_ref, m_ref, acc_ref, False, 0, 0)

    @pl.when(ki >= qi * ratio)
    def _():
        _flash_step(qt_ref, k_ref, vt_ref, m_ref, acc_ref, True, qi * tq, ki * tk)

    @pl.when(ki == (qi + 1) * ratio - 1)
    def _():
        for h in range(N_HEADS):
            a = acc_ref[h]
            o = a[:V_HEAD, :] / a[V_HEAD:V_HEAD + 1, :]
            o_ref[:, h * V_HEAD:(h + 1) * V_HEAD] = o.T.astype(o_ref.dtype)


def _flash_prompt(qt, k, vt, b, t, tq, tk):
    nq, nk = t // tq, t // tk
    ratio = tq // tk
    vt_tiles = tk // vt.shape[2]
    pairs = [(qi, ki) for qi in range(nq) for ki in range((qi + 1) * ratio)]
    qi_tab = jnp.asarray([p[0] for p in pairs], jnp.int32)
    ki_tab = jnp.asarray([p[1] for p in pairs], jnp.int32)
    grid_spec = pltpu.PrefetchScalarGridSpec(
        num_scalar_prefetch=2,
        grid=(b, len(pairs)),
        in_specs=[
            pl.BlockSpec((N_HEADS * HEAD_W, tq), lambda bi, s, qtab, ktab: (0, bi * nq + qtab[s])),
            pl.BlockSpec((tk, N_HEADS * HEAD_W), lambda bi, s, qtab, ktab: (bi * nk + ktab[s], 0)),
            pl.BlockSpec((vt_tiles, N_HEADS * V_ROWS, vt.shape[2]),
                         lambda bi, s, qtab, ktab: (bi * nk + ktab[s], 0, 0)),
        ],
        out_specs=pl.BlockSpec((tq, N_HEADS * V_HEAD), lambda bi, s, qtab, ktab: (bi * nq + qtab[s], 0)),
        scratch_shapes=[
            pltpu.VMEM((N_HEADS, 1, tq), F32),
            pltpu.VMEM((N_HEADS, V_ROWS, tq), F32),
        ],
    )
    return pl.pallas_call(
        functools.partial(_flash_kernel, ratio=ratio),
        grid_spec=grid_spec,
        out_shape=jax.ShapeDtypeStruct((b * t, N_HEADS * V_HEAD), BF16),
        compiler_params=_cparams(("arbitrary", "arbitrary")),
        name="flash_prompt",
    )(qi_tab, ki_tab, qt, k, vt)


def _attn_sample_kernel(q_ref, kc_ref, vc_ref, kn_ref, vn_ref, o_ref):
    for h in range(N_HEADS):
        sl = slice(h * HEAD_W, (h + 1) * HEAD_W)
        vs = slice(h * V_HEAD, (h + 1) * V_HEAD)
        qh = q_ref[:, sl]
        sc = _dot_nt(qh, kc_ref[:, sl])
        sn = _dot_nt(qh, kn_ref[:, sl])
        m = jnp.maximum(jnp.max(sc, axis=1, keepdims=True), jnp.max(sn, axis=1, keepdims=True))
        pc = jnp.exp2(sc - m)
        pn = jnp.exp2(sn - m)
        den = jnp.sum(pc, axis=1, keepdims=True) + jnp.sum(pn, axis=1, keepdims=True)
        o = _dot(pc.astype(BF16), vc_ref[:, vs]) + _dot(pn.astype(BF16), vn_ref[:, vs])
        o_ref[:, vs] = (o / den).astype(o_ref.dtype)


def _attn_sample(q, kc, vc, kn, vn, b, t, past):
    return pl.pallas_call(
        _attn_sample_kernel,
        grid=(b,),
        in_specs=[
            pl.BlockSpec((t, N_HEADS * HEAD_W), lambda i: (i, 0)),
            pl.BlockSpec((past, N_HEADS * HEAD_W), lambda i: (i, 0)),
            pl.BlockSpec((past, N_HEADS * V_HEAD), lambda i: (i, 0)),
            pl.BlockSpec((t, N_HEADS * HEAD_W), lambda i: (i, 0)),
            pl.BlockSpec((t, N_HEADS * V_HEAD), lambda i: (i, 0)),
        ],
        out_specs=pl.BlockSpec((t, N_HEADS * V_HEAD), lambda i: (i, 0)),
        out_shape=jax.ShapeDtypeStruct((b * t, N_HEADS * V_HEAD), BF16),
        compiler_params=_cparams(("arbitrary",)),
        name="attn_sample",
    )(q, kc, vc, kn, vn)


def _mix_kernel(x_ref, gt_ref, ubch_ref, g_ref, attn_ref, pu_ref, pch_ref, hp_ref, hc_ref,
                wpool_ref, pscale_ref, convw_ref, wbr_ref, wout_ref,
                xo_ref, ptail_ref, ctail_ref, extp_ref, extc_ref, *, n_hist):
    gg, tg, d = x_ref.shape
    j = pl.program_id(1)
    first = j == 0

    ubch = ubch_ref[...].astype(F32)
    u = ubch[:, COL_U:COL_U + POOL_W].reshape(gg, tg, POOL_W)
    bgate = ubch[:, COL_B:COL_B + CONV_W].reshape(gg, tg, CONV_W)
    cu = (ubch[:, COL_C:COL_C + CONV_W] * ubch[:, COL_H:COL_H + CONV_W]).reshape(gg, tg, CONV_W)

    hist_p = jnp.where(first, hp_ref[...], pu_ref[...].astype(F32).reshape(1, HIST_ROWS, POOL_W))
    pch = pch_ref[...].astype(F32)
    prev_cu = (pch[:, :CONV_W] * pch[:, CONV_W:])[HIST_ROWS - CONV_HIST_ROWS:]
    hist_c = jnp.where(first, hc_ref[...], prev_cu.reshape(1, CONV_HIST_ROWS, CONV_W))

    extp_ref[:, :HIST_ROWS, :] = hist_p
    extp_ref[:, HIST_ROWS:, :] = u
    extc_ref[:, :CONV_HIST_ROWS, :] = hist_c
    extc_ref[:, CONV_HIST_ROWS:, :] = cu
    ptail_ref[...] = u[:, tg - HIST_ROWS:, :]
    ctail_ref[...] = cu[:, tg - CONV_HIST_ROWS:, :]

    tpos = j * tg + lax.broadcasted_iota(jnp.int32, (1, tg, 1), 1)
    pooled = []
    for gi, w in enumerate(POOL_WINDOWS):
        cs = slice(gi * POOL_GROUP, (gi + 1) * POOL_GROUP)
        acc = u[:, :, cs]
        for kk in range(1, w):
            acc = acc + extp_ref[:, HIST_ROWS - kk:HIST_ROWS - kk + tg, cs]
        cnt = jnp.minimum(tpos + 1 + n_hist, w).astype(F32)
        dd = acc / cnt - u[:, :, cs]
        pooled.append(_dot(dd.reshape(gg * tg, POOL_GROUP).astype(BF16), wpool_ref[gi]))
    pool = jnp.concatenate(pooled, axis=1) * pscale_ref[...]

    cw = convw_ref[...]
    yc = (cw[0:1, :] * extc_ref[:, CONV_HIST_ROWS - 2:CONV_HIST_ROWS - 2 + tg, :]
          + cw[1:2, :] * extc_ref[:, CONV_HIST_ROWS - 1:CONV_HIST_ROWS - 1 + tg, :]
          + cw[2:3, :] * cu)
    conv = (bgate * yc).reshape(gg * tg, CONV_W)

    gates = jax.nn.sigmoid(g_ref[...].astype(F32))
    mixed = (gates[:, 0:d] * _dot(attn_ref[...], wbr_ref[0])
             + gates[:, d:2 * d] * _dot(pool.astype(BF16), wbr_ref[1])
             + gates[:, 2 * d:3 * d] * _dot(conv.astype(BF16), wbr_ref[2]))
    out = _dot(mixed.astype(BF16), wout_ref[...])
    xo_ref[...] = x_ref[...] + gt_ref[...] * out.reshape(gg, tg, d)


def _mix(x3, gt, z, attn, hist_p, hist_c, wpool, pscale, convw, wbr, wout, rows, n_hist):
    b, t, d = x3.shape
    gg, tg = _row_plan(b, t, rows)
    nj = t // tg
    rb = lambda i, j: i * nj + j
    hpb = tg // HIST_ROWS

    def prev_rows(i, j):
        return jnp.maximum(rb(i, j) * hpb - 1, 0)

    const2 = lambda i, j: (0, 0)
    const3 = lambda i, j: (0, 0, 0)
    return pl.pallas_call(
        functools.partial(_mix_kernel, n_hist=n_hist),
        grid=(b // gg, nj),
        in_specs=[
            pl.BlockSpec((gg, tg, d), lambda i, j: (i, j, 0)),
            pl.BlockSpec((gg, 1, d), lambda i, j: (i, 0, 0)),
            pl.BlockSpec((gg * tg, COL_CQ), lambda i, j: (rb(i, j), 0)),
            pl.BlockSpec((gg * tg, 3 * d), lambda i, j: (rb(i, j), COL_G // (3 * d))),
            pl.BlockSpec((gg * tg, N_HEADS * V_HEAD), lambda i, j: (rb(i, j), 0)),
            pl.BlockSpec((HIST_ROWS, POOL_W), lambda i, j: (prev_rows(i, j), 0)),
            pl.BlockSpec((HIST_ROWS, 2 * CONV_W), lambda i, j: (prev_rows(i, j), COL_C // (2 * CONV_W))),
            pl.BlockSpec((gg, HIST_ROWS, POOL_W), lambda i, j: (i, 0, 0)),
            pl.BlockSpec((gg, CONV_HIST_ROWS, CONV_W), lambda i, j: (i, 0, 0)),
            pl.BlockSpec(wpool.shape, const3),
            pl.BlockSpec(pscale.shape, const2),
            pl.BlockSpec(convw.shape, const2),
            pl.BlockSpec(wbr.shape, const3),
            pl.BlockSpec(wout.shape, const2),
        ],
        out_specs=[
            pl.BlockSpec((gg, tg, d), lambda i, j: (i, j, 0)),
            pl.BlockSpec((gg, HIST_ROWS, POOL_W), lambda i, j: (i, 0, 0)),
            pl.BlockSpec((gg, CONV_HIST_ROWS, CONV_W), lambda i, j: (i, 0, 0)),
        ],
        out_shape=[
            jax.ShapeDtypeStruct((b, t, d), F32),
            jax.ShapeDtypeStruct((b, HIST_ROWS, POOL_W), F32),
            jax.ShapeDtypeStruct((b, CONV_HIST_ROWS, CONV_W), F32),
        ],
        scratch_shapes=[
            pltpu.VMEM((gg, HIST_ROWS + tg, POOL_W), F32),
            pltpu.VMEM((gg, CONV_HIST_ROWS + tg, CONV_W), F32),
        ],
        compiler_params=_cparams(("arbitrary", "arbitrary")),
        name="mix_merge",
    )(x3, gt, z, z, attn, z, z, hist_p, hist_c, wpool, pscale, convw, wbr, wout)


def _peer_query_kernel(x_ref, g_ref, sh_ref, sc_ref, wq_ref, keys_ref, ht_ref, s1_ref, s2_ref):
    gg, tg, d = x_ref.shape
    h = _modnorm(x_ref[...], g_ref[...], sh_ref[...], sc_ref[...]).reshape(gg * tg, d)
    ht_ref[0] = h.T.astype(BF16)
    hb = h.astype(BF16)
    half = D_KEY // 2
    k1 = keys_ref[0]
    k2 = keys_ref[1]
    def query(hh):
        return _dot(hb, wq_ref[:, hh * D_KEY:(hh + 1) * D_KEY]).astype(BF16)

    q_next = query(0)
    for hh in range(PEER_HEADS):
        q = q_next
        if hh + 1 < PEER_HEADS:
            q_next = query(hh + 1)
        s1_ref[hh, 0] = _dot_nt(k1, q[:, :half])
        s2_ref[hh, 0] = _dot_nt(k2, q[:, half:])


def _peer_query(x3, g, sh, sc, wq, keys, rows):
    b, t, d = x3.shape
    gg, tg = _row_plan(b, t, rows)
    nj = t // tg
    nt = b * t // rows
    tok = lambda i, j: (0, i * nj + j, 0, 0)
    return pl.pallas_call(
        _peer_query_kernel,
        grid=(b // gg, nj),
        in_specs=[
            pl.BlockSpec((gg, tg, d), lambda i, j: (i, j, 0)),
            pl.BlockSpec((1, d), lambda i, j: (0, 0)),
            pl.BlockSpec((gg, 1, d), lambda i, j: (i, 0, 0)),
            pl.BlockSpec((gg, 1, d), lambda i, j: (i, 0, 0)),
            pl.BlockSpec(wq.shape, lambda i, j: (0, 0)),
            pl.BlockSpec(keys.shape, lambda i, j: (0, 0, 0)),
        ],
        out_specs=[
            pl.BlockSpec((1, d, rows), lambda i, j: (i * nj + j, 0, 0)),
            pl.BlockSpec((PEER_HEADS, 1, N_KEYS, rows), tok),
            pl.BlockSpec((PEER_HEADS, 1, N_KEYS, rows), tok),
        ],
        out_shape=[
            jax.ShapeDtypeStruct((nt, d, rows), BF16),
            jax.ShapeDtypeStruct((PEER_HEADS, nt, N_KEYS, rows), F32),
            jax.ShapeDtypeStruct((PEER_HEADS, nt, N_KEYS, rows), F32),
        ],
        compiler_params=_cparams(("arbitrary", "arbitrary")),
        name="peer_query",
    )(x3, g, sh, sc, wq, keys)


def _top_extract(s, with_rank):
    nk, r = s.shape
    iota = lax.broadcasted_iota(jnp.int32, (nk, r), 0).astype(F32)
    iota_k = lax.broadcasted_iota(jnp.int32, (PEER_TOPK, r), 0)
    rank = jnp.full((nk, r), NOT_TOP, F32) if with_rank else None
    vals = jnp.zeros((PEER_TOPK, r), F32)
    idxs = jnp.zeros((PEER_TOPK, r), F32)
    x = s
    for it in range(PEER_TOPK):
        m = jnp.max(x, axis=0, keepdims=True)
        idx = jnp.min(jnp.where(x == m, iota, float(nk)), axis=0, keepdims=True)
        hit = iota == idx
        if with_rank:
            rank = jnp.where(hit, float(it), rank)
        x = jnp.where(hit, -jnp.inf, x)
        vals = jnp.where(iota_k == it, m, vals)
        idxs = jnp.where(iota_k == it, idx, idxs)
    return rank, vals, idxs


def _oddeven_mergesort_pairs(n):
    pairs = []

    def merge(lo, hi, step):
        nxt = step * 2
        if nxt < hi - lo:
            merge(lo, hi, nxt)
            merge(lo + step, hi, nxt)
            for i in range(lo + step, hi - step, nxt):
                pairs.append((i, i + step))
        else:
            pairs.append((lo, lo + step))

    def sort(lo, hi):
        if hi - lo >= 1:
            mid = lo + (hi - lo) // 2
            sort(lo, mid)
            sort(mid + 1, hi)
            merge(lo, hi, 1)

    sort(0, n - 1)
    return pairs


SUBLANES = 8
SORT_PAIRS = _oddeven_mergesort_pairs(N_KEYS // SUBLANES)


def _sorted_top(s):
    n = N_KEYS // SUBLANES
    cols = [s[SUBLANES * v:SUBLANES * (v + 1), :] for v in range(n)]

    def exchange(i, j):
        hi, lo = jnp.maximum(cols[i], cols[j]), jnp.minimum(cols[i], cols[j])
        cols[i], cols[j] = hi, lo

    for i, j in SORT_PAIRS:
        exchange(i, j)
    shift = SUBLANES // 2
    while shift >= 1:
        other = [pltpu.roll(c, shift, axis=0) for c in cols]
        cols = [jnp.maximum(cols[p], other[n - 1 - p]) for p in range(n)]
        dist = n // 2
        while dist >= 1:
            for p in range(n):
                if p & dist == 0:
                    exchange(p, p + dist)
            dist //= 2
        shift //= 2
    return cols


def _ranks_from_sorted(s, top):
    n = len(top)
    ranks = []
    n_sel = None
    for v in range(N_KEYS // SUBLANES):
        x = s[SUBLANES * v:SUBLANES * (v + 1), :]
        acc = jnp.where(top[0] > x, 1.0, 0.0)
        for a in range(1, n - 1):
            acc = acc + jnp.where(top[a] > x, 1.0, 0.0)
        sel = x >= top[n - 1]
        ranks.append(jnp.where(sel, acc, NOT_TOP))
        one = jnp.where(sel, 1.0, 0.0)
        n_sel = one if n_sel is None else n_sel + one
    n_sel = jnp.sum(n_sel, axis=0, keepdims=True)
    dup = jnp.where(top[0] == top[1], 1.0, 0.0)
    for a in range(1, n - 1):
        dup = jnp.maximum(dup, jnp.where(top[a] == top[a + 1], 1.0, 0.0))
    flag = jnp.maximum(jnp.max(dup, axis=0, keepdims=True), jnp.where(n_sel != float(n), 1.0, 0.0))
    return jnp.concatenate(ranks, axis=0), flag


def _stack_rows(top):
    r = top[0].shape[1]
    iota_k = lax.broadcasted_iota(jnp.int32, (PEER_TOPK, r), 0)
    out = jnp.zeros((PEER_TOPK, r), F32)
    for a, t in enumerate(top):
        out = jnp.where(iota_k == a, jnp.concatenate([t, t], axis=0), out)
    return out


def _peer_select_kernel(s1_ref, s2_ref, c_ref, lr_ref, rk2_ref, d_ref):
    refs = (s1_ref, s2_ref, c_ref, lr_ref, rk2_ref, d_ref)
    flag = None
    for h in range(s1_ref.shape[0]):
        f = _peer_select_sorted(h, *refs)
        flag = f if flag is None else jnp.maximum(flag, f)

    @pl.when(jnp.max(flag) > 0.0)
    def _():
        for h in range(s1_ref.shape[0]):
            _peer_select_head(h, *refs)


def _peer_select_sorted(h, s1_ref, s2_ref, c_ref, lr_ref, rk2_ref, d_ref):
    s1 = s1_ref[h, 0]
    s2 = s2_ref[h, 0]
    top1 = _sorted_top(s1)
    top2 = _sorted_top(s2)
    rank1, flag1 = _ranks_from_sorted(s1, top1)
    rank2, flag2 = _ranks_from_sorted(s2, top2)
    v1 = _stack_rows(top1)
    v2 = _stack_rows(top2)
    cnt, zsum = _merge_counts(v1, v2)
    lr = jnp.zeros(rank1.shape, F32)
    for a in range(PEER_TOPK):
        lr = jnp.where(rank1 == float(a), cnt[a:a + 1, :], lr)
    c_ref[h, 0] = jnp.exp(s1 - v1[0:1, :]) / zsum
    lr_ref[h, 0] = lr
    rk2_ref[h, 0] = rank2.astype(BF16)
    d_ref[h, 0] = jnp.exp(s2 - v2[0:1, :]).astype(BF16)
    return jnp.maximum(flag1, flag2)


def _merge_counts(v1, v2):
    r = v1.shape[1]
    iota_k = lax.broadcasted_iota(jnp.int32, (PEER_TOPK, r), 0).astype(F32)
    cnt = jnp.zeros((PEER_TOPK, r), F32)
    front = v1 + v2[0:1, :]
    top = front[0:1, :]
    zsum = jnp.zeros((1, r), F32)
    for _ in range(PEER_TOPK):
        m = jnp.max(front, axis=0, keepdims=True)
        a = jnp.min(jnp.where(front == m, iota_k, float(PEER_TOPK)), axis=0, keepdims=True)
        hit = iota_k == a
        zsum = zsum + jnp.exp(m - top)
        cnt = jnp.where(hit, cnt + 1.0, cnt)
        c_hit = jnp.max(jnp.where(hit, cnt, 0.0), axis=0, keepdims=True)
        v1_hit = jnp.max(jnp.where(hit, v1, -jnp.inf), axis=0, keepdims=True)
        nxt = jnp.full((1, r), -jnp.inf, F32)
        for bcol in range(1, PEER_TOPK):
            nxt = jnp.where(c_hit == float(bcol), v2[bcol:bcol + 1, :], nxt)
        front = jnp.where(hit, v1_hit + nxt, front)
    return cnt, zsum


def _peer_select_head(h, s1_ref, s2_ref, c_ref, lr_ref, rk2_ref, d_ref):
    s1 = s1_ref[h, 0]
    s2 = s2_ref[h, 0]
    nk, r = s1.shape
    _, v1, idx1 = _top_extract(s1, False)
    rank2, v2, _ = _top_extract(s2, True)
    cnt, zsum = _merge_counts(v1, v2)
    iota = lax.broadcasted_iota(jnp.int32, (nk, r), 0).astype(F32)
    lr = jnp.zeros((nk, r), F32)
    for a in range(PEER_TOPK):
        lr = jnp.where(iota == idx1[a:a + 1, :], cnt[a:a + 1, :], lr)
    c_ref[h, 0] = jnp.exp(s1 - v1[0:1, :]) / zsum
    lr_ref[h, 0] = lr
    rk2_ref[h, 0] = rank2.astype(BF16)
    d_ref[h, 0] = jnp.exp(s2 - v2[0:1, :]).astype(BF16)


def _peer_select(s1, s2):
    hh, nt, nk, lanes = s1.shape
    heads_per_step = 2
    spec = pl.BlockSpec((heads_per_step, 1, nk, lanes), lambda i, h: (h, i, 0, 0))
    f32 = jax.ShapeDtypeStruct(s1.shape, F32)
    b16 = jax.ShapeDtypeStruct(s1.shape, BF16)
    return pl.pallas_call(
        _peer_select_kernel,
        grid=(nt, hh // heads_per_step),
        in_specs=[spec, spec],
        out_specs=[spec, spec, spec, spec],
        out_shape=[f32, f32, b16, b16],
        compiler_params=_cparams(("arbitrary", "arbitrary")),
        name="peer_select",
    )(s1, s2)


def _gelu(x):
    return 0.5 * x * (1.0 + lax.erf(x * (2.0 ** -0.5)))


def _peer_dense_kernel(x_ref, gt_ref, ht_ref, u_ref, vt_ref, c_ref, lr_ref, rk2_ref, d_ref,
                       xo_ref, a_ref, w_ref, acc_ref, *, rows_per_chunk):
    gg, tg, d = x_ref.shape
    nt, _, lanes = ht_ref.shape
    e = pl.program_id(1)
    slab = 16
    n_slab = N_KEYS // slab

    gsz = acc_ref.shape[2] // lanes
    ng = nt // gsz

    @pl.when(e == 0)
    def _():
        acc_ref[...] = jnp.zeros(acc_ref.shape, F32)

    w_ref[(ng - 1) % 2] = jnp.zeros(w_ref.shape[1:], BF16)

    def group(g):
        prev = (g + ng - 1) % ng
        rhs = jnp.concatenate([ht_ref[g * gsz + k] for k in range(gsz)], axis=1)
        w_prev = w_ref.at[prev % 2]
        w_cur = w_ref.at[g % 2]
        ec = a_ref.shape[0]
        n_part = 1
        pe, pd = ec // n_part, d // n_part
        ii_per_part = rows_per_chunk // n_part

        def key_part(q):
            a_ref[q * pe:(q + 1) * pe, :] = _dot(u_ref[q * pe:(q + 1) * pe, :], rhs)

        def value_part(q):
            acc_ref[prev, q * pd:(q + 1) * pd, :] += _dot(vt_ref[q * pd:(q + 1) * pd, :], w_prev[...])

        key_part(0)
        for q in range(n_part):
            if q + 1 < n_part:
                key_part(q + 1)
            value_part(q)
            for k in range(gsz):
                lt = g * gsz + k
                ls = slice(k * lanes, (k + 1) * lanes)
                for ii in range(q * ii_per_part, (q + 1) * ii_per_part):
                    cl = []
                    for hh in range(PEER_HEADS):
                        cl.append((
                            jnp.broadcast_to(c_ref[hh, lt, ii:ii + 1, :], (slab, lanes)).astype(BF16),
                            jnp.broadcast_to(lr_ref[hh, lt, ii:ii + 1, :], (slab, lanes)).astype(BF16)))
                    for jv in range(n_slab):
                        rs = slice(jv * slab, (jv + 1) * slab)
                        gate = None
                        for hh in range(PEER_HEADS):
                            dd = d_ref[hh, lt, rs, :]
                            term = cl[hh][0] * jnp.where(rk2_ref[hh, lt, rs, :] < cl[hh][1], dd,
                                                         jnp.zeros_like(dd))
                            gate = term if gate is None else gate + term
                        rows = slice(ii * N_KEYS + jv * slab, ii * N_KEYS + (jv + 1) * slab)
                        w_cur[rows, ls] = _gelu(a_ref[rows, ls]).astype(BF16) * gate

    for g in range(ng):
        group(g)
    acc_ref[ng - 1] += _dot(vt_ref[...], w_ref[(ng - 1) % 2])

    @pl.when(e == pl.num_programs(1) - 1)
    def _():
        for g in range(ng):
            for k in range(gsz):
                lt = g * gsz + k
                upd = acc_ref[g, :, k * lanes:(k + 1) * lanes].T
                if gg == 1:
                    rows = slice(lt * lanes, (lt + 1) * lanes)
                    xo_ref[0, rows, :] = x_ref[0, rows, :] + gt_ref[0] * upd
                else:
                    per = lanes // tg
                    rows = slice(lt * per, (lt + 1) * per)
                    xo_ref[rows] = x_ref[rows] + gt_ref[rows] * upd.reshape(per, tg, d)


def _peer_dense(x3, gt, ht, u, vt, c, lr, rk2, dd, tiles_per_block, rows_per_chunk):
    b, t, d = x3.shape
    nt_all, _, lanes = ht.shape
    gg, tg = _row_plan(b, t, tiles_per_block * lanes)
    nj = t // tg
    ne = u.shape[0]
    ec = rows_per_chunk * N_KEYS
    gsz = 1
    full = pl.BlockSpec((PEER_HEADS, tiles_per_block, N_KEYS, lanes), lambda i, e: (0, i, 0, 0))
    part = pl.BlockSpec((PEER_HEADS, tiles_per_block, rows_per_chunk, lanes),
                        lambda i, e: (0, i, e, 0))
    return pl.pallas_call(
        functools.partial(_peer_dense_kernel, rows_per_chunk=rows_per_chunk),
        grid=(nt_all // tiles_per_block, ne // ec),
        in_specs=[
            pl.BlockSpec((gg, tg, d), lambda i, e: (i // nj, i % nj, 0)),
            pl.BlockSpec((gg, 1, d), lambda i, e: (i // nj, 0, 0)),
            pl.BlockSpec((tiles_per_block, d, lanes), lambda i, e: (i, 0, 0)),
            pl.BlockSpec((ec, d), lambda i, e: (e, 0)),
            pl.BlockSpec((d, ec), lambda i, e: (0, e)),
            part, part, full, full,
        ],
        out_specs=pl.BlockSpec((gg, tg, d), lambda i, e: (i // nj, i % nj, 0)),
        out_shape=jax.ShapeDtypeStruct((b, t, d), F32),
        scratch_shapes=[
            pltpu.VMEM((ec, gsz * lanes), F32),
            pltpu.VMEM((2, ec, gsz * lanes), BF16),
            pltpu.VMEM((tiles_per_block // gsz, d, gsz * lanes), F32),
        ],
        compiler_params=_cparams(("arbitrary", "arbitrary")),
        name="peer_dense",
    )(x3, gt, ht, u, vt, c, lr, rk2, dd)


def _final_kernel(x_ref, g_ref, o_ref):
    o_ref[...] = _rmsnorm2(x_ref[...], g_ref[...])


def _final_norm(x3, g, rows):
    b, t, d = x3.shape
    x2 = x3.reshape(b * t, d)
    y = pl.pallas_call(
        _final_kernel,
        grid=(b * t // rows,),
        in_specs=[pl.BlockSpec((rows, d), lambda i: (i, 0)), pl.BlockSpec((1, d), lambda i: (0, 0))],
        out_specs=pl.BlockSpec((rows, d), lambda i: (i, 0)),
        out_shape=jax.ShapeDtypeStruct((b * t, d), F32),
        compiler_params=_cparams(("arbitrary",)),
        name="final_norm",
    )(x2, g)
    return y.reshape(b, t, d)


def _swap_halves(w):
    half = w.shape[-1] // 2
    return jnp.concatenate([w[..., half:], w[..., :half]], axis=-1)


def _prep_layer(p, l):
    d = p['w_in'].shape[1]
    w_in = p['w_in'][l]
    offs = [0]
    for nsz in (Q_LORA, KV_LORA, QK_ROPE, POOL_W, CONV_W, CONV_W, CONV_W, 3 * d):
        offs.append(offs[-1] + nsz)
    w_cq, w_ckv, w_kr, w_u, w_b, w_c, w_h, w_g = [w_in[:, offs[i]:offs[i + 1]] for i in range(8)]
    zeros = lambda n: jnp.zeros((d, n), w_in.dtype)
    w_krg = jnp.concatenate([zeros(QK_NOPE), w_kr, _swap_halves(w_kr)], axis=1)
    w_in_r = jnp.concatenate(
        [w_u, w_b, w_c, w_h, w_cq, w_ckv, w_krg, zeros(COL_G - COL_KR - HEAD_W), w_g], axis=1)
    assert w_in_r.shape[1] == D_IN_PAD

    w_uq = p['w_uq'][l]
    wq_r = jnp.concatenate(
        [w_uq[..., :QK_NOPE], w_uq[..., QK_NOPE:], _swap_halves(w_uq[..., QK_NOPE:])], axis=-1)
    wq_r = wq_r.reshape(Q_LORA, N_HEADS * HEAD_W)
    w_ukv = p['w_ukv'][l]
    wk_r = jnp.concatenate(
        [w_ukv[..., :QK_NOPE], jnp.zeros((KV_LORA, N_HEADS, HEAD_W - QK_NOPE), w_ukv.dtype)],
        axis=-1).reshape(KV_LORA, N_HEADS * HEAD_W)
    wv_r = w_ukv[..., QK_NOPE:].reshape(KV_LORA, N_HEADS * V_HEAD)
    return dict(
        w_in=w_in_r.astype(BF16), wq=wq_r.astype(BF16), wk=wk_r.astype(BF16), wv=wv_r.astype(BF16),
        g_mix=p['g_mix'][l][None, :], g_q=p['g_q'][l][None, :], g_kv=p['g_kv'][l][None, :],
        w_pool=p['w_pool'][l].astype(BF16), pool_scale=p['pool_scale'][l][None, :],
        conv_w=jnp.pad(p['conv_w'][l], ((0, 8 - CONV_K), (0, 0))),
        w_branch=p['w_branch'][l].astype(BF16), w_out=p['w_out'][l].astype(BF16),
        g_ffn=p['g_ffn'][l][None, :],
        peer_wq=p['peer_wq'][l].reshape(d, PEER_HEADS * D_KEY).astype(BF16),
        peer_keys=p['peer_keys'][l].astype(BF16),
        peer_u=p['peer_u'][l].astype(BF16),
        peer_vt=p['peer_v'][l].T.astype(BF16),
    )


def _rope_tables(pos):
    half = QK_ROPE // 2
    inv = ROPE_THETA ** (-jnp.arange(half, dtype=F32) / half)
    ang = pos.astype(F32)[:, None] * inv[None, :]
    cos, sin = jnp.cos(ang), jnp.sin(ang)
    z = lambda n: jnp.zeros((pos.shape[0], n), F32)
    cc = jnp.concatenate([z(QK_NOPE), cos, cos, z(QK_ROPE)], axis=1)
    ss = jnp.concatenate([z(QK_NOPE), -sin, sin, z(QK_ROPE)], axis=1)
    return cc, ss


def _trunk(x, mods, pos, n_hist, cache, hist_pool, hist_conv, layers, g_final, cfg):
    b, t, d = x.shape
    rows = cfg['rows']
    n = b * t
    cc, ss = _rope_tables(pos)
    if t < rows:
        cc = jnp.tile(cc, (rows // t, 1))
        ss = jnp.tile(ss, (rows // t, 1))
    place = jnp.concatenate(
        [jnp.zeros((QK_ROPE, QK_NOPE), F32), jnp.eye(QK_ROPE, dtype=F32),
         jnp.zeros((QK_ROPE, HEAD_W - QK_NOPE - QK_ROPE), F32)], axis=1).astype(BF16)
    new_kv, new_kr, new_pool, new_conv = [], [], [], []
    for l, lw in enumerate(layers):
        mod = mods[l].reshape(b, 1, 6 * d)
        sh1, sc1, gt1, sh2, sc2, gt2 = [mod[:, :, i * d:(i + 1) * d] for i in range(6)]
        z = _inproj(x, lw['g_mix'], sh1, sc1, lw['w_in'], rows)
        q, k, v, ckv, krg = _attn_prep(z, cc, ss, lw['g_q'], lw['wq'], lw['g_kv'], lw['wk'],
                                       lw['wv'], rows, cache is None)
        new_kv.append(ckv.reshape(b, t, KV_LORA))
        new_kr.append(krg[:, QK_NOPE:QK_NOPE + QK_ROPE].reshape(b, t, QK_ROPE))
        if cache is None:
            attn = _flash_prompt(q, k, v, b, t, cfg['attn_tq'], cfg['attn_tk'])
        else:
            ckv_c, kr_c = cache
            past = ckv_c.shape[2]
            kc, vc = _cache_expand(ckv_c[l].reshape(b * past, KV_LORA),
                                   kr_c[l].reshape(b * past, QK_ROPE),
                                   lw['wk'], lw['wv'], place, cfg['cache_rows'])
            attn = _attn_sample(q, kc, vc, k, v, b, t, past)
        hp = jnp.pad(hist_pool[l], ((0, 0), (HIST_ROWS - POOL_HIST, 0), (0, 0)))
        hc = jnp.pad(hist_conv[l], ((0, 0), (CONV_HIST_ROWS - (CONV_K - 1), 0), (0, 0)))
        x, ptail, ctail = _mix(x, gt1, z, attn, hp, hc, lw['w_pool'], lw['pool_scale'],
                               lw['conv_w'], lw['w_branch'], lw['w_out'], rows, n_hist)
        new_pool.append(ptail[:, HIST_ROWS - POOL_HIST:, :])
        new_conv.append(ctail[:, CONV_HIST_ROWS - (CONV_K - 1):, :])
        ht, s1, s2 = _peer_query(x, lw['g_ffn'], sh2, sc2, lw['peer_wq'], lw['peer_keys'], rows)
        c, lr, rk2, dd = _peer_select(s1, s2)
        x = _peer_dense(x, gt2, ht, lw['peer_u'], lw['peer_vt'], c, lr, rk2, dd,
                        min(cfg['peer_tiles'], n // rows), cfg['rows_per_chunk'])
    y = _final_norm(x, g_final[None, :], rows)
    return y, jnp.stack(new_kv), jnp.stack(new_kr), jnp.stack(new_pool), jnp.stack(new_conv)


def _config(t_prompt):
    rows = min(256, t_prompt)
    return dict(rows=rows, attn_tq=min(512, t_prompt), attn_tk=min(512, t_prompt), cache_rows=1024, peer_tiles=4,
                rows_per_chunk=8)


def kernel(x_prompt, x_sample, cache_kv_latent, cache_k_rope, state_pool, state_conv,
           c_prompt, c_sample, w_ada, b_ada, g_mix, w_in, g_q, w_uq, g_kv, w_ukv,
           w_pool, pool_scale, conv_w, w_branch, w_out, g_ffn, peer_wq, peer_keys,
           peer_u, peer_v, g_final):
    p = {'w_in': w_in, 'g_mix': g_mix, 'g_q': g_q, 'w_uq': w_uq, 'g_kv': g_kv, 'w_ukv': w_ukv,
         'w_pool': w_pool, 'pool_scale': pool_scale, 'conv_w': conv_w, 'w_branch': w_branch,
         'w_out': w_out, 'g_ffn': g_ffn, 'peer_wq': peer_wq, 'peer_keys': peer_keys,
         'peer_u': peer_u, 'peer_v': peer_v}
    depth = w_ada.shape[0]
    bp, tp, d = x_prompt.shape
    bs, ts, _ = x_sample.shape
    past = cache_kv_latent.shape[2]
    layers = [_prep_layer(p, l) for l in range(depth)]

    c_all = jnp.concatenate([c_prompt, c_sample], axis=0)
    pad = (-c_all.shape[0]) % 8
    c_all = jnp.pad(c_all, ((0, pad), (0, 0)))
    mods = _ada(c_all, w_ada, b_ada)
    mods_p, mods_s = mods[:, :bp], mods[:, bp:bp + bs]

    cfg = _config(tp)
    zp = jnp.zeros((depth, bp, POOL_HIST, POOL_W), x_prompt.dtype)
    zc = jnp.zeros((depth, bp, CONV_K - 1, CONV_W), x_prompt.dtype)
    y_p, p_kv, p_kr, p_pool, p_conv = _trunk(
        x_prompt, mods_p, jnp.arange(tp), 0, None, zp, zc, layers, g_final, cfg)
    y_s, s_kv, s_kr, s_pool, s_conv = _trunk(
        x_sample, mods_s, past + jnp.arange(ts), min(past, POOL_HIST),
        (cache_kv_latent, cache_k_rope), state_pool, state_conv, layers, g_final, cfg)
    return (y_p, y_s, p_kv, p_kr, p_pool, p_conv, s_kv, s_kr, s_pool, s_conv)
```

```python
import functools
import math

import jax
import jax.numpy as jnp
from jax import lax
from jax.experimental import pallas as pl
from jax.experimental.pallas import tpu as pltpu

F32 = jnp.float32
BF16 = jnp.bfloat16

EPS = 1e-6
N_HEADS = 8
QK_NOPE = 64
QK_ROPE = 32
V_HEAD = 64
Q_LORA = 512
KV_LORA = 256
ROPE_THETA = 10000.0
CHUNK = 64
ATTN_SCALE = (QK_NOPE + QK_ROPE) ** -0.5
NEG_INF = -1e30
POOL_WINDOWS = (2, 4, 8, 16)
POOL_GROUP = 128
POOL_W = 512
POOL_HIST = 15
CONV_W = 512
CONV_K = 3
PEER_HEADS = 8
N_KEYS = 128
D_KEY = 256
PEER_TOPK = 16
NOT_TOP = 99.0

LANES = 128
BF16_SUBLANES = 16
VMEM_LIMIT_BYTES = 56 * 2**20

HEAD_W = LANES
V_ROWS = V_HEAD + 16
Q_SCALE = ATTN_SCALE * math.log2(math.e)
HIST_ROWS = 16
CONV_HIST_ROWS = 8

COL_U, COL_B, COL_C, COL_H = 0, 512, 1024, 1536
COL_CQ = 2048
COL_CKV = 2560
COL_KR = 2816
COL_G = 3072
D_IN_PAD = 6144


def _cparams(sem):
    return pltpu.CompilerParams(dimension_semantics=sem, vmem_limit_bytes=VMEM_LIMIT_BYTES)


def _dot(a, b):
    return jnp.dot(a, b, preferred_element_type=F32)


def _dot_nt(a, b):
    return lax.dot_general(a, b, (((1,), (1,)), ((), ())), preferred_element_type=F32)


def _ada_kernel(c_ref, w_ref, b_ref, o_ref):
    c = c_ref[...]
    act = c * jax.nn.sigmoid(c)
    o_ref[0] = _dot(act.astype(BF16), w_ref[0].astype(BF16)) + b_ref[0]


def _ada(c_all, w_ada, b_ada):
    depth, d, n6 = w_ada.shape
    bp = c_all.shape[0]
    tn = 1536
    return pl.pallas_call(
        _ada_kernel,
        grid=(depth, n6 // tn),
        in_specs=[
            pl.BlockSpec((bp, d), lambda l, j: (0, 0)),
            pl.BlockSpec((1, d, tn), lambda l, j: (l, 0, j)),
            pl.BlockSpec((1, 1, tn), lambda l, j: (l, 0, j)),
        ],
        out_specs=pl.BlockSpec((1, bp, tn), lambda l, j: (l, 0, j)),
        out_shape=jax.ShapeDtypeStruct((depth, bp, n6), F32),
        compiler_params=_cparams(("arbitrary", "arbitrary")),
        name="ada",
    )(c_all, w_ada, b_ada.reshape(depth, 1, n6))


def _modnorm(x, g, sh, sc):
    ms = jnp.mean(x * x, axis=-1, keepdims=True)
    y = x * lax.rsqrt(ms + EPS) * g
    return y * (1.0 + sc) + sh


def _rmsnorm2(x, g):
    ms = jnp.mean(x * x, axis=-1, keepdims=True)
    return x * lax.rsqrt(ms + EPS) * g


def _row_plan(b, t, rows):
    if t >= rows:
        assert t % rows == 0
        return 1, rows
    assert rows % t == 0 and b % (rows // t) == 0
    return rows // t, t


def _inproj_kernel(x_ref, g_ref, sh_ref, sc_ref, w_ref, o_ref, *, col_chunk):
    gg, tg, d = x_ref.shape
    h = _modnorm(x_ref[...], g_ref[...], sh_ref[...], sc_ref[...])
    hb = h.reshape(gg * tg, d).astype(BF16)
    n = w_ref.shape[1]
    for c in range(0, n, col_chunk):
        o_ref[:, c:c + col_chunk] = _dot(hb, w_ref[:, c:c + col_chunk]).astype(o_ref.dtype)


def _inproj(x3, g, sh, sc, w, rows):
    b, t, d = x3.shape
    gg, tg = _row_plan(b, t, rows)
    nj = t // tg
    n = w.shape[1]
    return pl.pallas_call(
        functools.partial(_inproj_kernel, col_chunk=512),
        grid=(b // gg, nj),
        in_specs=[
            pl.BlockSpec((gg, tg, d), lambda i, j: (i, j, 0)),
            pl.BlockSpec((1, d), lambda i, j: (0, 0)),
            pl.BlockSpec((gg, 1, d), lambda i, j: (i, 0, 0)),
            pl.BlockSpec((gg, 1, d), lambda i, j: (i, 0, 0)),
            pl.BlockSpec((d, n), lambda i, j: (0, 0)),
        ],
        out_specs=pl.BlockSpec((gg * tg, n), lambda i, j: (i * nj + j, 0)),
        out_shape=jax.ShapeDtypeStruct((b * t, n), BF16),
        compiler_params=_cparams(("arbitrary", "arbitrary")),
        name="inproj",
    )(x3, g, sh, sc, w)


def _rope_group(z, cc, ss):
    return z * cc + pltpu.roll(z, HEAD_W - QK_ROPE, axis=1) * ss


def _attn_prep_kernel(cq_ref, ckv_ref, kr_ref, cc_ref, ss_ref, gq_ref, wq_ref, gkv_ref,
                      wk_ref, wv_ref, q_ref, k_ref, v_ref, kv_ref, kro_ref, *, transposed):
    cck = cc_ref[...]
    ss = ss_ref[...]
    lane = lax.broadcasted_iota(jnp.int32, cck.shape, 1)
    ccq = jnp.where(lane < QK_NOPE, 1.0, cck)
    qn = _rmsnorm2(cq_ref[...].astype(F32), gq_ref[...]).astype(BF16)
    ckv = _rmsnorm2(ckv_ref[...].astype(F32), gkv_ref[...])
    kv_ref[...] = ckv
    ckv_b = ckv.astype(BF16)
    kr = _rope_group(kr_ref[...].astype(F32), cck, ss)
    kro_ref[...] = kr
    v = _dot(ckv_b, wv_ref[...])
    rows = v.shape[0]
    if transposed:
        ones = jnp.ones((V_ROWS - V_HEAD, rows), F32)
        vt = v.T
        for h in range(N_HEADS):
            v_ref[0, h * V_ROWS:(h + 1) * V_ROWS, :] = jnp.concatenate(
                [vt[h * V_HEAD:(h + 1) * V_HEAD, :], ones], axis=0).astype(BF16)
    else:
        v_ref[...] = v.astype(BF16)
    for h in range(N_HEADS):
        sl = slice(h * HEAD_W, (h + 1) * HEAD_W)
        zq = _dot(qn, wq_ref[:, sl])
        qh = _rope_group(zq, ccq, ss) * Q_SCALE
        if transposed:
            q_ref[sl, :] = qh.T.astype(BF16)
        else:
            q_ref[:, sl] = qh.astype(BF16)
        k_ref[:, sl] = (_dot(ckv_b, wk_ref[:, sl]) + kr).astype(BF16)


def _attn_prep(z, cc, ss, gq, wq, gkv, wk, wv, rows, transposed):
    n = z.shape[0]
    npos = cc.shape[0] // rows
    row = lambda i: (i, 0)
    col = lambda i: (0, i)
    const = lambda i: (0, 0)
    if transposed:
        q_spec = pl.BlockSpec((N_HEADS * HEAD_W, rows), col)
        q_shape = jax.ShapeDtypeStruct((N_HEADS * HEAD_W, n), BF16)
        v_spec = pl.BlockSpec((1, N_HEADS * V_ROWS, rows), lambda i: (i, 0, 0))
        v_shape = jax.ShapeDtypeStruct((n // rows, N_HEADS * V_ROWS, rows), BF16)
    else:
        q_spec = pl.BlockSpec((rows, N_HEADS * HEAD_W), row)
        q_shape = jax.ShapeDtypeStruct((n, N_HEADS * HEAD_W), BF16)
        v_spec = pl.BlockSpec((rows, N_HEADS * V_HEAD), row)
        v_shape = jax.ShapeDtypeStruct((n, N_HEADS * V_HEAD), BF16)
    return pl.pallas_call(
        functools.partial(_attn_prep_kernel, transposed=transposed),
        grid=(n // rows,),
        in_specs=[
            pl.BlockSpec((rows, Q_LORA), lambda i: (i, COL_CQ // Q_LORA)),
            pl.BlockSpec((rows, KV_LORA), lambda i: (i, COL_CKV // KV_LORA)),
            pl.BlockSpec((rows, HEAD_W), lambda i: (i, COL_KR // HEAD_W)),
            pl.BlockSpec((rows, HEAD_W), lambda i: (i % npos, 0)),
            pl.BlockSpec((rows, HEAD_W), lambda i: (i % npos, 0)),
            pl.BlockSpec((1, Q_LORA), const),
            pl.BlockSpec(wq.shape, const),
            pl.BlockSpec((1, KV_LORA), const),
            pl.BlockSpec(wk.shape, const),
            pl.BlockSpec(wv.shape, const),
        ],
        out_specs=[
            q_spec,
            pl.BlockSpec((rows, N_HEADS * HEAD_W), row),
            v_spec,
            pl.BlockSpec((rows, KV_LORA), row),
            pl.BlockSpec((rows, HEAD_W), row),
        ],
        out_shape=[
            q_shape,
            jax.ShapeDtypeStruct((n, N_HEADS * HEAD_W), BF16),
            v_shape,
            jax.ShapeDtypeStruct((n, KV_LORA), F32),
            jax.ShapeDtypeStruct((n, HEAD_W), F32),
        ],
        compiler_params=_cparams(("arbitrary",)),
        name="attn_prep",
    )(z, z, z, cc, ss, gq, wq, gkv, wk, wv)


def _cache_expand_kernel(ckv_ref, kr_ref, wk_ref, wv_ref, place_ref, k_ref, v_ref):
    ckv_b = ckv_ref[...].astype(BF16)
    krp = _dot(kr_ref[...].astype(BF16), place_ref[...])
    v_ref[...] = _dot(ckv_b, wv_ref[...]).astype(BF16)
    for h in range(N_HEADS):
        sl = slice(h * HEAD_W, (h + 1) * HEAD_W)
        k_ref[:, sl] = (_dot(ckv_b, wk_ref[:, sl]) + krp).astype(BF16)


def _cache_expand(ckv, kr, wk, wv, place, rows):
    n = ckv.shape[0]
    row = lambda i: (i, 0)
    const = lambda i: (0, 0)
    return pl.pallas_call(
        _cache_expand_kernel,
        grid=(n // rows,),
        in_specs=[
            pl.BlockSpec((rows, KV_LORA), row),
            pl.BlockSpec((rows, QK_ROPE), row),
            pl.BlockSpec(wk.shape, const),
            pl.BlockSpec(wv.shape, const),
            pl.BlockSpec(place.shape, const),
        ],
        out_specs=[
            pl.BlockSpec((rows, N_HEADS * HEAD_W), row),
            pl.BlockSpec((rows, N_HEADS * V_HEAD), row),
        ],
        out_shape=[
            jax.ShapeDtypeStruct((n, N_HEADS * HEAD_W), BF16),
            jax.ShapeDtypeStruct((n, N_HEADS * V_HEAD), BF16),
        ],
        compiler_params=_cparams(("arbitrary",)),
        name="cache_expand",
    )(ckv, kr, wk, wv, place)


def _flash_step(qt_ref, k_ref, vt_ref, m_ref, acc_ref, masked, q0, k0):
    tk, tq = k_ref.shape[0], qt_ref.shape[1]
    if masked:
        kc = (k0 + lax.broadcasted_iota(jnp.int32, (tk, tq), 0)) // CHUNK
        qc = (q0 + lax.broadcasted_iota(jnp.int32, (tk, tq), 1)) // CHUNK
        keep = kc <= qc
    def scores(h):
        sl = slice(h * HEAD_W, (h + 1) * HEAD_W)
        s = _dot(k_ref[:, sl], qt_ref[sl, :])
        return jnp.where(keep, s, NEG_INF) if masked else s

    def stats(h, s):
        m_prev = m_ref[h]
        m_new = jnp.maximum(m_prev, jnp.max(s, axis=0, keepdims=True))
        m_ref[h] = m_new
        return m_new, jnp.exp2(m_prev - m_new)

    s = {0: scores(0)}
    if N_HEADS > 1:
        s[1] = scores(1)
    st = {0: stats(0, s[0])}
    for h in range(N_HEADS):
        vs = slice(h * V_ROWS, (h + 1) * V_ROWS)
        if h + 2 < N_HEADS:
            s[h + 2] = scores(h + 2)
        if h + 1 < N_HEADS:
            st[h + 1] = stats(h + 1, s[h + 1])
        m_new, alpha = st.pop(h)
        p = jnp.exp2(s.pop(h) - m_new).astype(BF16)
        vt_h = jnp.concatenate([vt_ref[j, vs, :] for j in range(vt_ref.shape[0])], axis=1)
        acc_ref[h] = alpha * acc_ref[h] + _dot(vt_h, p)


def _flash_kernel(qi_tab, ki_tab, qt_ref, k_ref, vt_ref, o_ref, m_ref, acc_ref, *, ratio):
    step = pl.program_id(1)
    qi = qi_tab[step]
    ki = ki_tab[step]
    tk, tq = k_ref.shape[0], qt_ref.shape[1]

    @pl.when(ki == 0)
    def _():
        m_ref[...] = jnp.full(m_ref.shape, NEG_INF, F32)
        acc_ref[...] = jnp.zeros(acc_ref.shape, F32)

    @pl.when(ki < qi * ratio)
    def _():
        _flash_step(qt_ref, k_ref, vt_ref, m_ref, acc_ref, False, 0, 0)

    @pl.when(ki >= qi * ratio)
    def _():
        _flash_step(qt_ref, k_ref, vt_ref, m_ref, acc_ref, True, qi * tq, ki * tk)

    @pl.when(ki == (qi + 1) * ratio - 1)
    def _():
        for h in range(N_HEADS):
            a = acc_ref[h]
            o = a[:V_HEAD, :] / a[V_HEAD:V_HEAD + 1, :]
            o_ref[:, h * V_HEAD:(h + 1) * V_HEAD] = o.T.astype(o_ref.dtype)


def _flash_prompt(qt, k, vt, b, t, tq, tk):
    nq, nk = t // tq, t // tk
    ratio = tq // tk
    vt_tiles = tk // vt.shape[2]
    pairs = [(qi, ki) for qi in range(nq) for ki in range((qi + 1) * ratio)]
    qi_tab = jnp.asarray([p[0] for p in pairs], jnp.int32)
    ki_tab = jnp.asarray([p[1] for p in pairs], jnp.int32)
    grid_spec = pltpu.PrefetchScalarGridSpec(
        num_scalar_prefetch=2,
        grid=(b, len(pairs)),
        in_specs=[
            pl.BlockSpec((N_HEADS * HEAD_W, tq), lambda bi, s, qtab, ktab: (0, bi * nq + qtab[s])),
            pl.BlockSpec((tk, N_HEADS * HEAD_W), lambda bi, s, qtab, ktab: (bi * nk + ktab[s], 0)),
            pl.BlockSpec((vt_tiles, N_HEADS * V_ROWS, vt.shape[2]),
                         lambda bi, s, qtab, ktab: (bi * nk + ktab[s], 0, 0)),
        ],
        out_specs=pl.BlockSpec((tq, N_HEADS * V_HEAD), lambda bi, s, qtab, ktab: (bi * nq + qtab[s], 0)),
        scratch_shapes=[
            pltpu.VMEM((N_HEADS, 1, tq), F32),
            pltpu.VMEM((N_HEADS, V_ROWS, tq), F32),
        ],
    )
    return pl.pallas_call(
        functools.partial(_flash_kernel, ratio=ratio),
        grid_spec=grid_spec,
        out_shape=jax.ShapeDtypeStruct((b * t, N_HEADS * V_HEAD), BF16),
        compiler_params=_cparams(("arbitrary", "arbitrary")),
        name="flash_prompt",
    )(qi_tab, ki_tab, qt, k, vt)


def _attn_sample_kernel(q_ref, kc_ref, vc_ref, kn_ref, vn_ref, o_ref):
    for h in range(N_HEADS):
        sl = slice(h * HEAD_W, (h + 1) * HEAD_W)
        vs = slice(h * V_HEAD, (h + 1) * V_HEAD)
        qh = q_ref[:, sl]
        sc = _dot_nt(qh, kc_ref[:, sl])
        sn = _dot_nt(qh, kn_ref[:, sl])
        m = jnp.maximum(jnp.max(sc, axis=1, keepdims=True), jnp.max(sn, axis=1, keepdims=True))
        pc = jnp.exp2(sc - m)
        pn = jnp.exp2(sn - m)
        den = jnp.sum(pc, axis=1, keepdims=True) + jnp.sum(pn, axis=1, keepdims=True)
        o = _dot(pc.astype(BF16), vc_ref[:, vs]) + _dot(pn.astype(BF16), vn_ref[:, vs])
        o_ref[:, vs] = (o / den).astype(o_ref.dtype)


def _attn_sample(q, kc, vc, kn, vn, b, t, past):
    return pl.pallas_call(
        _attn_sample_kernel,
        grid=(b,),
        in_specs=[
            pl.BlockSpec((t, N_HEADS * HEAD_W), lambda i: (i, 0)),
            pl.BlockSpec((past, N_HEADS * HEAD_W), lambda i: (i, 0)),
            pl.BlockSpec((past, N_HEADS * V_HEAD), lambda i: (i, 0)),
            pl.BlockSpec((t, N_HEADS * HEAD_W), lambda i: (i, 0)),
            pl.BlockSpec((t, N_HEADS * V_HEAD), lambda i: (i, 0)),
        ],
        out_specs=pl.BlockSpec((t, N_HEADS * V_HEAD), lambda i: (i, 0)),
        out_shape=jax.ShapeDtypeStruct((b * t, N_HEADS * V_HEAD), BF16),
        compiler_params=_cparams(("arbitrary",)),
        name="attn_sample",
    )(q, kc, vc, kn, vn)


def _mix_kernel(x_ref, gt_ref, ubch_ref, g_ref, attn_ref, pu_ref, pch_ref, hp_ref, hc_ref,
                wpool_ref, pscale_ref, convw_ref, wbr_ref, wout_ref,
                xo_ref, ptail_ref, ctail_ref, extp_ref, extc_ref, *, n_hist):
    gg, tg, d = x_ref.shape
    j = pl.program_id(1)
    first = j == 0

    ubch = ubch_ref[...].astype(F32)
    u = ubch[:, COL_U:COL_U + POOL_W].reshape(gg, tg, POOL_W)
    bgate = ubch[:, COL_B:COL_B + CONV_W].reshape(gg, tg, CONV_W)
    cu = (ubch[:, COL_C:COL_C + CONV_W] * ubch[:, COL_H:COL_H + CONV_W]).reshape(gg, tg, CONV_W)

    hist_p = jnp.where(first, hp_ref[...], pu_ref[...].astype(F32).reshape(1, HIST_ROWS, POOL_W))
    pch = pch_ref[...].astype(F32)
    prev_cu = (pch[:, :CONV_W] * pch[:, CONV_W:])[HIST_ROWS - CONV_HIST_ROWS:]
    hist_c = jnp.where(first, hc_ref[...], prev_cu.reshape(1, CONV_HIST_ROWS, CONV_W))

    extp_ref[:, :HIST_ROWS, :] = hist_p
    extp_ref[:, HIST_ROWS:, :] = u
    extc_ref[:, :CONV_HIST_ROWS, :] = hist_c
    extc_ref[:, CONV_HIST_ROWS:, :] = cu
    ptail_ref[...] = u[:, tg - HIST_ROWS:, :]
    ctail_ref[...] = cu[:, tg - CONV_HIST_ROWS:, :]

    tpos = j * tg + lax.broadcasted_iota(jnp.int32, (1, tg, 1), 1)
    pooled = []
    for gi, w in enumerate(POOL_WINDOWS):
        cs = slice(gi * POOL_GROUP, (gi + 1) * POOL_GROUP)
        acc = u[:, :, cs]
        for kk in range(1, w):
            acc = acc + extp_ref[:, HIST_ROWS - kk:HIST_ROWS - kk + tg, cs]
        cnt = jnp.minimum(tpos + 1 + n_hist, w).astype(F32)
        dd = acc / cnt - u[:, :, cs]
        pooled.append(_dot(dd.reshape(gg * tg, POOL_GROUP).astype(BF16), wpool_ref[gi]))
    pool = jnp.concatenate(pooled, axis=1) * pscale_ref[...]

    cw = convw_ref[...]
    yc = (cw[0:1, :] * extc_ref[:, CONV_HIST_ROWS - 2:CONV_HIST_ROWS - 2 + tg, :]
          + cw[1:2, :] * extc_ref[:, CONV_HIST_ROWS - 1:CONV_HIST_ROWS - 1 + tg, :]
          + cw[2:3, :] * cu)
    conv = (bgate * yc).reshape(gg * tg, CONV_W)

    gates = jax.nn.sigmoid(g_ref[...].astype(F32))
    mixed = (gates[:, 0:d] * _dot(attn_ref[...], wbr_ref[0])
             + gates[:, d:2 * d] * _dot(pool.astype(BF16), wbr_ref[1])
             + gates[:, 2 * d:3 * d] * _dot(conv.astype(BF16), wbr_ref[2]))
    out = _dot(mixed.astype(BF16), wout_ref[...])
    xo_ref[...] = x_ref[...] + gt_ref[...] * out.reshape(gg, tg, d)


def _mix(x3, gt, z, attn, hist_p, hist_c, wpool, pscale, convw, wbr, wout, rows, n_hist):
    b, t, d = x3.shape
    gg, tg = _row_plan(b, t, rows)
    nj = t // tg
    rb = lambda i, j: i * nj + j
    hpb = tg // HIST_ROWS

    def prev_rows(i, j):
        return jnp.maximum(rb(i, j) * hpb - 1, 0)

    const2 = lambda i, j: (0, 0)
    const3 = lambda i, j: (0, 0, 0)
    return pl.pallas_call(
        functools.partial(_mix_kernel, n_hist=n_hist),
        grid=(b // gg, nj),
        in_specs=[
            pl.BlockSpec((gg, tg, d), lambda i, j: (i, j, 0)),
            pl.BlockSpec((gg, 1, d), lambda i, j: (i, 0, 0)),
            pl.BlockSpec((gg * tg, COL_CQ), lambda i, j: (rb(i, j), 0)),
            pl.BlockSpec((gg * tg, 3 * d), lambda i, j: (rb(i, j), COL_G // (3 * d))),
            pl.BlockSpec((gg * tg, N_HEADS * V_HEAD), lambda i, j: (rb(i, j), 0)),
            pl.BlockSpec((HIST_ROWS, POOL_W), lambda i, j: (prev_rows(i, j), 0)),
            pl.BlockSpec((HIST_ROWS, 2 * CONV_W), lambda i, j: (prev_rows(i, j), COL_C // (2 * CONV_W))),
            pl.BlockSpec((gg, HIST_ROWS, POOL_W), lambda i, j: (i, 0, 0)),
            pl.BlockSpec((gg, CONV_HIST_ROWS, CONV_W), lambda i, j: (i, 0, 0)),
            pl.BlockSpec(wpool.shape, const3),
            pl.BlockSpec(pscale.shape, const2),
            pl.BlockSpec(convw.shape, const2),
            pl.BlockSpec(wbr.shape, const3),
            pl.BlockSpec(wout.shape, const2),
        ],
        out_specs=[
            pl.BlockSpec((gg, tg, d), lambda i, j: (i, j, 0)),
            pl.BlockSpec((gg, HIST_ROWS, POOL_W), lambda i, j: (i, 0, 0)),
            pl.BlockSpec((gg, CONV_HIST_ROWS, CONV_W), lambda i, j: (i, 0, 0)),
        ],
        out_shape=[
            jax.ShapeDtypeStruct((b, t, d), F32),
            jax.ShapeDtypeStruct((b, HIST_ROWS, POOL_W), F32),
            jax.ShapeDtypeStruct((b, CONV_HIST_ROWS, CONV_W), F32),
        ],
        scratch_shapes=[
            pltpu.VMEM((gg, HIST_ROWS + tg, POOL_W), F32),
            pltpu.VMEM((gg, CONV_HIST_ROWS + tg, CONV_W), F32),
        ],
        compiler_params=_cparams(("arbitrary", "arbitrary")),
        name="mix_merge",
    )(x3, gt, z, z, attn, z, z, hist_p, hist_c, wpool, pscale, convw, wbr, wout)


def _peer_query_kernel(x_ref, g_ref, sh_ref, sc_ref, wq_ref, keys_ref, ht_ref, s1_ref, s2_ref):
    gg, tg, d = x_ref.shape
    h = _modnorm(x_ref[...], g_ref[...], sh_ref[...], sc_ref[...]).reshape(gg * tg, d)
    ht_ref[0] = h.T.astype(BF16)
    hb = h.astype(BF16)
    half = D_KEY // 2
    k1 = keys_ref[0]
    k2 = keys_ref[1]
    def query(hh):
        return _dot(hb, wq_ref[:, hh * D_KEY:(hh + 1) * D_KEY]).astype(BF16)

    q_next = query(0)
    for hh in range(PEER_HEADS):
        q = q_next
        if hh + 1 < PEER_HEADS:
            q_next = query(hh + 1)
        s1_ref[hh, 0] = _dot_nt(k1, q[:, :half])
        s2_ref[hh, 0] = _dot_nt(k2, q[:, half:])


def _peer_query(x3, g, sh, sc, wq, keys, rows):
    b, t, d = x3.shape
    gg, tg = _row_plan(b, t, rows)
    nj = t // tg
    nt = b * t // rows
    tok = lambda i, j: (0, i * nj + j, 0, 0)
    return pl.pallas_call(
        _peer_query_kernel,
        grid=(b // gg, nj),
        in_specs=[
            pl.BlockSpec((gg, tg, d), lambda i, j: (i, j, 0)),
            pl.BlockSpec((1, d), lambda i, j: (0, 0)),
            pl.BlockSpec((gg, 1, d), lambda i, j: (i, 0, 0)),
            pl.BlockSpec((gg, 1, d), lambda i, j: (i, 0, 0)),
            pl.BlockSpec(wq.shape, lambda i, j: (0, 0)),
            pl.BlockSpec(keys.shape, lambda i, j: (0, 0, 0)),
        ],
        out_specs=[
            pl.BlockSpec((1, d, rows), lambda i, j: (i * nj + j, 0, 0)),
            pl.BlockSpec((PEER_HEADS, 1, N_KEYS, rows), tok),
            pl.BlockSpec((PEER_HEADS, 1, N_KEYS, rows), tok),
        ],
        out_shape=[
            jax.ShapeDtypeStruct((nt, d, rows), BF16),
            jax.ShapeDtypeStruct((PEER_HEADS, nt, N_KEYS, rows), F32),
            jax.ShapeDtypeStruct((PEER_HEADS, nt, N_KEYS, rows), F32),
        ],
        compiler_params=_cparams(("arbitrary", "arbitrary")),
        name="peer_query",
    )(x3, g, sh, sc, wq, keys)


def _top_extract(s, with_rank):
    nk, r = s.shape
    iota = lax.broadcasted_iota(jnp.int32, (nk, r), 0).astype(F32)
    iota_k = lax.broadcasted_iota(jnp.int32, (PEER_TOPK, r), 0)
    rank = jnp.full((nk, r), NOT_TOP, F32) if with_rank else None
    vals = jnp.zeros((PEER_TOPK, r), F32)
    idxs = jnp.zeros((PEER_TOPK, r), F32)
    x = s
    for it in range(PEER_TOPK):
        m = jnp.max(x, axis=0, keepdims=True)
        idx = jnp.min(jnp.where(x == m, iota, float(nk)), axis=0, keepdims=True)
        hit = iota == idx
        if with_rank:
            rank = jnp.where(hit, float(it), rank)
        x = jnp.where(hit, -jnp.inf, x)
        vals = jnp.where(iota_k == it, m, vals)
        idxs = jnp.where(iota_k == it, idx, idxs)
    return rank, vals, idxs


def _oddeven_mergesort_pairs(n):
    pairs = []

    def merge(lo, hi, step):
        nxt = step * 2
        if nxt < hi - lo:
            merge(lo, hi, nxt)
            merge(lo + step, hi, nxt)
            for i in range(lo + step, hi - step, nxt):
                pairs.append((i, i + step))
        else:
            pairs.append((lo, lo + step))

    def sort(lo, hi):
        if hi - lo >= 1:
            mid = lo + (hi - lo) // 2
            sort(lo, mid)
            sort(mid + 1, hi)
            merge(lo, hi, 1)

    sort(0, n - 1)
    return pairs


SUBLANES = 8
SORT_PAIRS = _oddeven_mergesort_pairs(N_KEYS // SUBLANES)


def _sorted_top(s):
    n = N_KEYS // SUBLANES
    cols = [s[SUBLANES * v:SUBLANES * (v + 1), :] for v in range(n)]

    def exchange(i, j):
        hi, lo = jnp.maximum(cols[i], cols[j]), jnp.minimum(cols[i], cols[j])
        cols[i], cols[j] = hi, lo

    for i, j in SORT_PAIRS:
        exchange(i, j)
    shift = SUBLANES // 2
    while shift >= 1:
        other = [pltpu.roll(c, shift, axis=0) for c in cols]
        cols = [jnp.maximum(cols[p], other[n - 1 - p]) for p in range(n)]
        dist = n // 2
        while dist >= 1:
            for p in range(n):
                if p & dist == 0:
                    exchange(p, p + dist)
            dist //= 2
        shift //= 2
    return cols


def _ranks_from_sorted(s, top):
    n = len(top)
    ranks = []
    n_sel = None
    for v in range(N_KEYS // SUBLANES):
        x = s[SUBLANES * v:SUBLANES * (v + 1), :]
        acc = jnp.where(top[0] > x, 1.0, 0.0)
        for a in range(1, n - 1):
            acc = acc + jnp.where(top[a] > x, 1.0, 0.0)
        sel = x >= top[n - 1]
        ranks.append(jnp.where(sel, acc, NOT_TOP))
        one = jnp.where(sel, 1.0, 0.0)
        n_sel = one if n_sel is None else n_sel + one
    n_sel = jnp.sum(n_sel, axis=0, keepdims=True)
    dup = jnp.where(top[0] == top[1], 1.0, 0.0)
    for a in range(1, n - 1):
        dup = jnp.maximum(dup, jnp.where(top[a] == top[a + 1], 1.0, 0.0))
    flag = jnp.maximum(jnp.max(dup, axis=0, keepdims=True), jnp.where(n_sel != float(n), 1.0, 0.0))
    return jnp.concatenate(ranks, axis=0), flag


def _stack_rows(top):
    r = top[0].shape[1]
    iota_k = lax.broadcasted_iota(jnp.int32, (PEER_TOPK, r), 0)
    out = jnp.zeros((PEER_TOPK, r), F32)
    for a, t in enumerate(top):
        out = jnp.where(iota_k == a, jnp.concatenate([t, t], axis=0), out)
    return out


def _peer_select_kernel(s1_ref, s2_ref, c_ref, lr_ref, rk2_ref, d_ref):
    refs = (s1_ref, s2_ref, c_ref, lr_ref, rk2_ref, d_ref)
    flag = None
    for h in range(s1_ref.shape[0]):
        f = _peer_select_sorted(h, *refs)
        flag = f if flag is None else jnp.maximum(flag, f)

    @pl.when(jnp.max(flag) > 0.0)
    def _():
        for h in range(s1_ref.shape[0]):
            _peer_select_head(h, *refs)


def _peer_select_sorted(h, s1_ref, s2_ref, c_ref, lr_ref, rk2_ref, d_ref):
    s1 = s1_ref[h, 0]
    s2 = s2_ref[h, 0]
    top1 = _sorted_top(s1)
    top2 = _sorted_top(s2)
    rank1, flag1 = _ranks_from_sorted(s1, top1)
    rank2, flag2 = _ranks_from_sorted(s2, top2)
    v1 = _stack_rows(top1)
    v2 = _stack_rows(top2)
    cnt, zsum = _merge_counts(v1, v2)
    lr = jnp.zeros(rank1.shape, F32)
    for a in range(PEER_TOPK):
        lr = jnp.where(rank1 == float(a), cnt[a:a + 1, :], lr)
    c_ref[h, 0] = jnp.exp(s1 - v1[0:1, :]) / zsum
    lr_ref[h, 0] = lr
    rk2_ref[h, 0] = rank2.astype(BF16)
    d_ref[h, 0] = jnp.exp(s2 - v2[0:1, :]).astype(BF16)
    return jnp.maximum(flag1, flag2)


def _merge_counts(v1, v2):
    r = v1.shape[1]
    iota_k = lax.broadcasted_iota(jnp.int32, (PEER_TOPK, r), 0).astype(F32)
    cnt = jnp.zeros((PEER_TOPK, r), F32)
    front = v1 + v2[0:1, :]
    top = front[0:1, :]
    zsum = jnp.zeros((1, r), F32)
    for _ in range(PEER_TOPK):
        m = jnp.max(front, axis=0, keepdims=True)
        a = jnp.min(jnp.where(front == m, iota_k, float(PEER_TOPK)), axis=0, keepdims=True)
        hit = iota_k == a
        zsum = zsum + jnp.exp(m - top)
        cnt = jnp.where(hit, cnt + 1.0, cnt)
        c_hit = jnp.max(jnp.where(hit, cnt, 0.0), axis=0, keepdims=True)
        v1_hit = jnp.max(jnp.where(hit, v1, -jnp.inf), axis=0, keepdims=True)
        nxt = jnp.full((1, r), -jnp.inf, F32)
        for bcol in range(1, PEER_TOPK):
            nxt = jnp.where(c_hit == float(bcol), v2[bcol:bcol + 1, :], nxt)
        front = jnp.where(hit, v1_hit + nxt, front)
    return cnt, zsum


def _peer_select_head(h, s1_ref, s2_ref, c_ref, lr_ref, rk2_ref, d_ref):
    s1 = s1_ref[h, 0]
    s2 = s2_ref[h, 0]
    nk, r = s1.shape
    _, v1, idx1 = _top_extract(s1, False)
    rank2, v2, _ = _top_extract(s2, True)
    cnt, zsum = _merge_counts(v1, v2)
    iota = lax.broadcasted_iota(jnp.int32, (nk, r), 0).astype(F32)
    lr = jnp.zeros((nk, r), F32)
    for a in range(PEER_TOPK):
        lr = jnp.where(iota == idx1[a:a + 1, :], cnt[a:a + 1, :], lr)
    c_ref[h, 0] = jnp.exp(s1 - v1[0:1, :]) / zsum
    lr_ref[h, 0] = lr
    rk2_ref[h, 0] = rank2.astype(BF16)
    d_ref[h, 0] = jnp.exp(s2 - v2[0:1, :]).astype(BF16)


def _peer_select(s1, s2):
    hh, nt, nk, lanes = s1.shape
    heads_per_step = 2
    spec = pl.BlockSpec((heads_per_step, 1, nk, lanes), lambda i, h: (h, i, 0, 0))
    f32 = jax.ShapeDtypeStruct(s1.shape, F32)
    b16 = jax.ShapeDtypeStruct(s1.shape, BF16)
    return pl.pallas_call(
        _peer_select_kernel,
        grid=(nt, hh // heads_per_step),
        in_specs=[spec, spec],
        out_specs=[spec, spec, spec, spec],
        out_shape=[f32, f32, b16, b16],
        compiler_params=_cparams(("arbitrary", "arbitrary")),
        name="peer_select",
    )(s1, s2)


def _gelu(x):
    return 0.5 * x * (1.0 + lax.erf(x * (2.0 ** -0.5)))


def _peer_dense_kernel(x_ref, gt_ref, ht_ref, u_ref, vt_ref, c_ref, lr_ref, rk2_ref, d_ref,
                       xo_ref, a_ref, w_ref, acc_ref, *, rows_per_chunk):
    gg, tg, d = x_ref.shape
    nt, _, lanes = ht_ref.shape
    e = pl.program_id(1)
    slab = BF16_SUBLANES
    n_slab = N_KEYS // slab

    @pl.when(e == 0)
    def _():
        acc_ref[...] = jnp.zeros(acc_ref.shape, F32)

    w_ref[(nt - 1) % 2] = jnp.zeros(w_ref.shape[1:], BF16)

    def tile(lt):
        prev = (lt + nt - 1) % nt
        a_ref[...] = _dot(u_ref[...], ht_ref[lt])
        acc_ref[prev] += _dot(vt_ref[...], w_ref[prev % 2])
        w_cur = w_ref.at[lt % 2]
        for ii in range(rows_per_chunk):
            cl = []
            for hh in range(PEER_HEADS):
                cl.append((
                    jnp.broadcast_to(c_ref[hh, lt, ii:ii + 1, :], (slab, lanes)).astype(BF16),
                    jnp.broadcast_to(lr_ref[hh, lt, ii:ii + 1, :], (slab, lanes)).astype(BF16)))
            for jv in range(n_slab):
                rs = slice(jv * slab, (jv + 1) * slab)
                gate = None
                for hh in range(PEER_HEADS):
                    dd = d_ref[hh, lt, rs, :]
                    term = cl[hh][0] * jnp.where(rk2_ref[hh, lt, rs, :] < cl[hh][1], dd,
                                                 jnp.zeros_like(dd))
                    gate = term if gate is None else gate + term
                rows = slice(ii * N_KEYS + jv * slab, ii * N_KEYS + (jv + 1) * slab)
                w_cur[rows, :] = _gelu(a_ref[rows, :]).astype(BF16) * gate

    for lt in range(nt):
        tile(lt)
    acc_ref[nt - 1] += _dot(vt_ref[...], w_ref[(nt - 1) % 2])

    @pl.when(e == pl.num_programs(1) - 1)
    def _():
        for lt in range(nt):
            upd = acc_ref[lt].T
            if gg == 1:
                rows = slice(lt * lanes, (lt + 1) * lanes)
                xo_ref[0, rows, :] = x_ref[0, rows, :] + gt_ref[0] * upd
            else:
                per = lanes // tg
                rows = slice(lt * per, (lt + 1) * per)
                xo_ref[rows] = x_ref[rows] + gt_ref[rows] * upd.reshape(per, tg, d)


def _peer_dense(x3, gt, ht, u, vt, c, lr, rk2, dd, tiles_per_block, rows_per_chunk):
    b, t, d = x3.shape
    nt_all, _, lanes = ht.shape
    gg, tg = _row_plan(b, t, tiles_per_block * lanes)
    nj = t // tg
    ne = u.shape[0]
    ec = rows_per_chunk * N_KEYS
    assert gg == 1 or lanes % tg == 0
    full = pl.BlockSpec((PEER_HEADS, tiles_per_block, N_KEYS, lanes), lambda i, e: (0, i, 0, 0))
    part = pl.BlockSpec((PEER_HEADS, tiles_per_block, rows_per_chunk, lanes),
                        lambda i, e: (0, i, e, 0))
    return pl.pallas_call(
        functools.partial(_peer_dense_kernel, rows_per_chunk=rows_per_chunk),
        grid=(nt_all // tiles_per_block, ne // ec),
        in_specs=[
            pl.BlockSpec((gg, tg, d), lambda i, e: (i // nj, i % nj, 0)),
            pl.BlockSpec((gg, 1, d), lambda i, e: (i // nj, 0, 0)),
            pl.BlockSpec((tiles_per_block, d, lanes), lambda i, e: (i, 0, 0)),
            pl.BlockSpec((ec, d), lambda i, e: (e, 0)),
            pl.BlockSpec((d, ec), lambda i, e: (0, e)),
            part, part, full, full,
        ],
        out_specs=pl.BlockSpec((gg, tg, d), lambda i, e: (i // nj, i % nj, 0)),
        out_shape=jax.ShapeDtypeStruct((b, t, d), F32),
        scratch_shapes=[
            pltpu.VMEM((ec, lanes), F32),
            pltpu.VMEM((2, ec, lanes), BF16),
            pltpu.VMEM((tiles_per_block, d, lanes), F32),
        ],
        compiler_params=_cparams(("arbitrary", "arbitrary")),
        name="peer_dense",
    )(x3, gt, ht, u, vt, c, lr, rk2, dd)


def _final_kernel(x_ref, g_ref, o_ref):
    o_ref[...] = _rmsnorm2(x_ref[...], g_ref[...])


def _final_norm(x3, g, rows):
    b, t, d = x3.shape
    x2 = x3.reshape(b * t, d)
    y = pl.pallas_call(
        _final_kernel,
        grid=(b * t // rows,),
        in_specs=[pl.BlockSpec((rows, d), lambda i: (i, 0)), pl.BlockSpec((1, d), lambda i: (0, 0))],
        out_specs=pl.BlockSpec((rows, d), lambda i: (i, 0)),
        out_shape=jax.ShapeDtypeStruct((b * t, d), F32),
        compiler_params=_cparams(("arbitrary",)),
        name="final_norm",
    )(x2, g)
    return y.reshape(b, t, d)


def _swap_halves(w):
    half = w.shape[-1] // 2
    return jnp.concatenate([w[..., half:], w[..., :half]], axis=-1)


def _prep_layer(p, l):
    d = p['w_in'].shape[1]
    w_in = p['w_in'][l]
    offs = [0]
    for nsz in (Q_LORA, KV_LORA, QK_ROPE, POOL_W, CONV_W, CONV_W, CONV_W, 3 * d):
        offs.append(offs[-1] + nsz)
    w_cq, w_ckv, w_kr, w_u, w_b, w_c, w_h, w_g = [w_in[:, offs[i]:offs[i + 1]] for i in range(8)]
    zeros = lambda n: jnp.zeros((d, n), w_in.dtype)
    w_krg = jnp.concatenate([zeros(QK_NOPE), w_kr, _swap_halves(w_kr)], axis=1)
    w_in_r = jnp.concatenate(
        [w_u, w_b, w_c, w_h, w_cq, w_ckv, w_krg, zeros(COL_G - COL_KR - HEAD_W), w_g], axis=1)
    assert w_in_r.shape[1] == D_IN_PAD

    w_uq = p['w_uq'][l]
    wq_r = jnp.concatenate(
        [w_uq[..., :QK_NOPE], w_uq[..., QK_NOPE:], _swap_halves(w_uq[..., QK_NOPE:])], axis=-1)
    wq_r = wq_r.reshape(Q_LORA, N_HEADS * HEAD_W)
    w_ukv = p['w_ukv'][l]
    wk_r = jnp.concatenate(
        [w_ukv[..., :QK_NOPE], jnp.zeros((KV_LORA, N_HEADS, HEAD_W - QK_NOPE), w_ukv.dtype)],
        axis=-1).reshape(KV_LORA, N_HEADS * HEAD_W)
    wv_r = w_ukv[..., QK_NOPE:].reshape(KV_LORA, N_HEADS * V_HEAD)
    return dict(
        w_in=w_in_r.astype(BF16), wq=wq_r.astype(BF16), wk=wk_r.astype(BF16), wv=wv_r.astype(BF16),
        g_mix=p['g_mix'][l][None, :], g_q=p['g_q'][l][None, :], g_kv=p['g_kv'][l][None, :],
        w_pool=p['w_pool'][l].astype(BF16), pool_scale=p['pool_scale'][l][None, :],
        conv_w=jnp.pad(p['conv_w'][l], ((0, 8 - CONV_K), (0, 0))),
        w_branch=p['w_branch'][l].astype(BF16), w_out=p['w_out'][l].astype(BF16),
        g_ffn=p['g_ffn'][l][None, :],
        peer_wq=p['peer_wq'][l].reshape(d, PEER_HEADS * D_KEY).astype(BF16),
        peer_keys=p['peer_keys'][l].astype(BF16),
        peer_u=p['peer_u'][l].astype(BF16),
        peer_vt=p['peer_v'][l].T.astype(BF16),
    )


def _rope_tables(pos):
    half = QK_ROPE // 2
    inv = ROPE_THETA ** (-jnp.arange(half, dtype=F32) / half)
    ang = pos.astype(F32)[:, None] * inv[None, :]
    cos, sin = jnp.cos(ang), jnp.sin(ang)
    z = lambda n: jnp.zeros((pos.shape[0], n), F32)
    cc = jnp.concatenate([z(QK_NOPE), cos, cos, z(QK_ROPE)], axis=1)
    ss = jnp.concatenate([z(QK_NOPE), -sin, sin, z(QK_ROPE)], axis=1)
    return cc, ss


def _trunk(x, mods, pos, n_hist, cache, hist_pool, hist_conv, layers, g_final, cfg):
    b, t, d = x.shape
    rows = cfg['rows']
    n = b * t
    cc, ss = _rope_tables(pos)
    if t < rows:
        cc = jnp.tile(cc, (rows // t, 1))
        ss = jnp.tile(ss, (rows // t, 1))
    place = jnp.concatenate(
        [jnp.zeros((QK_ROPE, QK_NOPE), F32), jnp.eye(QK_ROPE, dtype=F32),
         jnp.zeros((QK_ROPE, HEAD_W - QK_NOPE - QK_ROPE), F32)], axis=1).astype(BF16)
    new_kv, new_kr, new_pool, new_conv = [], [], [], []
    for l, lw in enumerate(layers):
        mod = mods[l].reshape(b, 1, 6 * d)
        sh1, sc1, gt1, sh2, sc2, gt2 = [mod[:, :, i * d:(i + 1) * d] for i in range(6)]
        z = _inproj(x, lw['g_mix'], sh1, sc1, lw['w_in'], rows)
        q, k, v, ckv, krg = _attn_prep(z, cc, ss, lw['g_q'], lw['wq'], lw['g_kv'], lw['wk'],
                                       lw['wv'], rows, cache is None)
        new_kv.append(ckv.reshape(b, t, KV_LORA))
        new_kr.append(krg[:, QK_NOPE:QK_NOPE + QK_ROPE].reshape(b, t, QK_ROPE))
        if cache is None:
            attn = _flash_prompt(q, k, v, b, t, cfg['attn_tq'], cfg['attn_tk'])
        else:
            ckv_c, kr_c = cache
            past = ckv_c.shape[2]
            kc, vc = _cache_expand(ckv_c[l].reshape(b * past, KV_LORA),
                                   kr_c[l].reshape(b * past, QK_ROPE),
                                   lw['wk'], lw['wv'], place, cfg['cache_rows'])
            attn = _attn_sample(q, kc, vc, k, v, b, t, past)
        hp = jnp.pad(hist_pool[l], ((0, 0), (HIST_ROWS - POOL_HIST, 0), (0, 0)))
        hc = jnp.pad(hist_conv[l], ((0, 0), (CONV_HIST_ROWS - (CONV_K - 1), 0), (0, 0)))
        x, ptail, ctail = _mix(x, gt1, z, attn, hp, hc, lw['w_pool'], lw['pool_scale'],
                               lw['conv_w'], lw['w_branch'], lw['w_out'], rows, n_hist)
        new_pool.append(ptail[:, HIST_ROWS - POOL_HIST:, :])
        new_conv.append(ctail[:, CONV_HIST_ROWS - (CONV_K - 1):, :])
        ht, s1, s2 = _peer_query(x, lw['g_ffn'], sh2, sc2, lw['peer_wq'], lw['peer_keys'], rows)
        c, lr, rk2, dd = _peer_select(s1, s2)
        x = _peer_dense(x, gt2, ht, lw['peer_u'], lw['peer_vt'], c, lr, rk2, dd,
                        min(cfg['peer_tiles'], n // rows), cfg['rows_per_chunk'])
    y = _final_norm(x, g_final[None, :], rows)
    return y, jnp.stack(new_kv), jnp.stack(new_kr), jnp.stack(new_pool), jnp.stack(new_conv)


def _config(t_prompt):
    rows = min(256, t_prompt)
    return dict(rows=rows, attn_tq=min(512, t_prompt), attn_tk=min(512, t_prompt), cache_rows=1024, peer_tiles=4,
                rows_per_chunk=8)


def kernel(x_prompt, x_sample, cache_kv_latent, cache_k_rope, state_pool, state_conv,
           c_prompt, c_sample, w_ada, b_ada, g_mix, w_in, g_q, w_uq, g_kv, w_ukv,
           w_pool, pool_scale, conv_w, w_branch, w_out, g_ffn, peer_wq, peer_keys,
           peer_u, peer_v, g_final):
    p = {'w_in': w_in, 'g_mix': g_mix, 'g_q': g_q, 'w_uq': w_uq, 'g_kv': g_kv, 'w_ukv': w_ukv,
         'w_pool': w_pool, 'pool_scale': pool_scale, 'conv_w': conv_w, 'w_branch': w_branch,
         'w_out': w_out, 'g_ffn': g_ffn, 'peer_wq': peer_wq, 'peer_keys': peer_keys,
         'peer_u': peer_u, 'peer_v': peer_v}
    depth = w_ada.shape[0]
    bp, tp, d = x_prompt.shape
    bs, ts, _ = x_sample.shape
    past = cache_kv_latent.shape[2]
    layers = [_prep_layer(p, l) for l in range(depth)]

    c_all = jnp.concatenate([c_prompt, c_sample], axis=0)
    pad = (-c_all.shape[0]) % 8
    c_all = jnp.pad(c_all, ((0, pad), (0, 0)))
    mods = _ada(c_all, w_ada, b_ada)
    mods_p, mods_s = mods[:, :bp], mods[:, bp:bp + bs]

    cfg = _config(tp)
    zp = jnp.zeros((depth, bp, POOL_HIST, POOL_W), x_prompt.dtype)
    zc = jnp.zeros((depth, bp, CONV_K - 1, CONV_W), x_prompt.dtype)
    y_p, p_kv, p_kr, p_pool, p_conv = _trunk(
        x_prompt, mods_p, jnp.arange(tp), 0, None, zp, zc, layers, g_final, cfg)
    y_s, s_kv, s_kr, s_pool, s_conv = _trunk(
        x_sample, mods_s, past + jnp.arange(ts), min(past, POOL_HIST),
        (cache_kv_latent, cache_k_rope), state_pool, state_conv, layers, g_final, cfg)
    return (y_p, y_s, p_kv, p_kr, p_pool, p_conv, s_kv, s_kr, s_pool, s_conv)
```

```python
import functools
import math

import jax
import jax.numpy as jnp
from jax import lax
from jax.experimental import pallas as pl
from jax.experimental.pallas import tpu as pltpu

F32 = jnp.float32
BF16 = jnp.bfloat16

EPS = 1e-6
N_HEADS = 8
QK_NOPE = 64
QK_ROPE = 32
V_HEAD = 64
Q_LORA = 512
KV_LORA = 256
ROPE_THETA = 10000.0
CHUNK = 64
ATTN_SCALE = (QK_NOPE + QK_ROPE) ** -0.5
NEG_INF = -1e30
POOL_WINDOWS = (2, 4, 8, 16)
POOL_GROUP = 128
POOL_W = 512
POOL_HIST = 15
CONV_W = 512
CONV_K = 3
PEER_HEADS = 8
N_KEYS = 128
D_KEY = 256
PEER_TOPK = 16
NOT_TOP = 99.0

LANES = 128
BF16_SUBLANES = 16
VMEM_LIMIT_BYTES = 56 * 2**20

HEAD_W = LANES
V_ROWS = V_HEAD + 16
Q_SCALE = ATTN_SCALE * math.log2(math.e)
HIST_ROWS = 16
CONV_HIST_ROWS = 8

COL_U, COL_B, COL_C, COL_H = 0, 512, 1024, 1536
COL_CQ = 2048
COL_CKV = 2560
COL_KR = 2816
COL_G = 3072
D_IN_PAD = 6144


def _cparams(sem):
    return pltpu.CompilerParams(dimension_semantics=sem, vmem_limit_bytes=VMEM_LIMIT_BYTES)


def _dot(a, b):
    return jnp.dot(a, b, preferred_element_type=F32)


def _dot_nt(a, b):
    return lax.dot_general(a, b, (((1,), (1,)), ((), ())), preferred_element_type=F32)


def _ada_kernel(c_ref, w_ref, b_ref, o_ref):
    c = c_ref[...]
    act = c * jax.nn.sigmoid(c)
    o_ref[0] = _dot(act.astype(BF16), w_ref[0].astype(BF16)) + b_ref[0]


def _ada(c_all, w_ada, b_ada):
    depth, d, n6 = w_ada.shape
    bp = c_all.shape[0]
    tn = 1536
    return pl.pallas_call(
        _ada_kernel,
        grid=(depth, n6 // tn),
        in_specs=[
            pl.BlockSpec((bp, d), lambda l, j: (0, 0)),
            pl.BlockSpec((1, d, tn), lambda l, j: (l, 0, j)),
            pl.BlockSpec((1, 1, tn), lambda l, j: (l, 0, j)),
        ],
        out_specs=pl.BlockSpec((1, bp, tn), lambda l, j: (l, 0, j)),
        out_shape=jax.ShapeDtypeStruct((depth, bp, n6), F32),
        compiler_params=_cparams(("arbitrary", "arbitrary")),
        name="ada",
    )(c_all, w_ada, b_ada.reshape(depth, 1, n6))


def _modnorm(x, g, sh, sc):
    ms = jnp.mean(x * x, axis=-1, keepdims=True)
    y = x * lax.rsqrt(ms + EPS) * g
    return y * (1.0 + sc) + sh


def _rmsnorm2(x, g):
    ms = jnp.mean(x * x, axis=-1, keepdims=True)
    return x * lax.rsqrt(ms + EPS) * g


def _row_plan(b, t, rows):
    if t >= rows:
        assert t % rows == 0
        return 1, rows
    assert rows % t == 0 and b % (rows // t) == 0
    return rows // t, t


def _inproj_kernel(x_ref, g_ref, sh_ref, sc_ref, w_ref, o_ref, *, col_chunk):
    gg, tg, d = x_ref.shape
    h = _modnorm(x_ref[...], g_ref[...], sh_ref[...], sc_ref[...])
    hb = h.reshape(gg * tg, d).astype(BF16)
    n = w_ref.shape[1]
    for c in range(0, n, col_chunk):
        o_ref[:, c:c + col_chunk] = _dot(hb, w_ref[:, c:c + col_chunk]).astype(o_ref.dtype)


def _inproj(x3, g, sh, sc, w, rows):
    b, t, d = x3.shape
    gg, tg = _row_plan(b, t, rows)
    nj = t // tg
    n = w.shape[1]
    return pl.pallas_call(
        functools.partial(_inproj_kernel, col_chunk=512),
        grid=(b // gg, nj),
        in_specs=[
            pl.BlockSpec((gg, tg, d), lambda i, j: (i, j, 0)),
            pl.BlockSpec((1, d), lambda i, j: (0, 0)),
            pl.BlockSpec((gg, 1, d), lambda i, j: (i, 0, 0)),
            pl.BlockSpec((gg, 1, d), lambda i, j: (i, 0, 0)),
            pl.BlockSpec((d, n), lambda i, j: (0, 0)),
        ],
        out_specs=pl.BlockSpec((gg * tg, n), lambda i, j: (i * nj + j, 0)),
        out_shape=jax.ShapeDtypeStruct((b * t, n), BF16),
        compiler_params=_cparams(("arbitrary", "arbitrary")),
        name="inproj",
    )(x3, g, sh, sc, w)


def _rope_group(z, cc, ss):
    return z * cc + pltpu.roll(z, HEAD_W - QK_ROPE, axis=1) * ss


def _attn_prep_kernel(cq_ref, ckv_ref, kr_ref, cc_ref, ss_ref, gq_ref, wq_ref, gkv_ref,
                      wk_ref, wv_ref, q_ref, k_ref, v_ref, kv_ref, kro_ref, *, transposed):
    cck = cc_ref[...]
    ss = ss_ref[...]
    lane = lax.broadcasted_iota(jnp.int32, cck.shape, 1)
    ccq = jnp.where(lane < QK_NOPE, 1.0, cck)
    qn = _rmsnorm2(cq_ref[...].astype(F32), gq_ref[...]).astype(BF16)
    ckv = _rmsnorm2(ckv_ref[...].astype(F32), gkv_ref[...])
    kv_ref[...] = ckv
    ckv_b = ckv.astype(BF16)
    kr = _rope_group(kr_ref[...].astype(F32), cck, ss)
    kro_ref[...] = kr
    v = _dot(ckv_b, wv_ref[...])
    rows = v.shape[0]
    if transposed:
        ones = jnp.ones((V_ROWS - V_HEAD, rows), F32)
        vt = v.T
        for h in range(N_HEADS):
            v_ref[0, h * V_ROWS:(h + 1) * V_ROWS, :] = jnp.concatenate(
                [vt[h * V_HEAD:(h + 1) * V_HEAD, :], ones], axis=0).astype(BF16)
    else:
        v_ref[...] = v.astype(BF16)
    for h in range(N_HEADS):
        sl = slice(h * HEAD_W, (h + 1) * HEAD_W)
        zq = _dot(qn, wq_ref[:, sl])
        qh = _rope_group(zq, ccq, ss) * Q_SCALE
        if transposed:
            q_ref[sl, :] = qh.T.astype(BF16)
        else:
            q_ref[:, sl] = qh.astype(BF16)
        k_ref[:, sl] = (_dot(ckv_b, wk_ref[:, sl]) + kr).astype(BF16)


def _attn_prep(z, cc, ss, gq, wq, gkv, wk, wv, rows, transposed):
    n = z.shape[0]
    npos = cc.shape[0] // rows
    row = lambda i: (i, 0)
    col = lambda i: (0, i)
    const = lambda i: (0, 0)
    if transposed:
        q_spec = pl.BlockSpec((N_HEADS * HEAD_W, rows), col)
        q_shape = jax.ShapeDtypeStruct((N_HEADS * HEAD_W, n), BF16)
        v_spec = pl.BlockSpec((1, N_HEADS * V_ROWS, rows), lambda i: (i, 0, 0))
        v_shape = jax.ShapeDtypeStruct((n // rows, N_HEADS * V_ROWS, rows), BF16)
    else:
        q_spec = pl.BlockSpec((rows, N_HEADS * HEAD_W), row)
        q_shape = jax.ShapeDtypeStruct((n, N_HEADS * HEAD_W), BF16)
        v_spec = pl.BlockSpec((rows, N_HEADS * V_HEAD), row)
        v_shape = jax.ShapeDtypeStruct((n, N_HEADS * V_HEAD), BF16)
    return pl.pallas_call(
        functools.partial(_attn_prep_kernel, transposed=transposed),
        grid=(n // rows,),
        in_specs=[
            pl.BlockSpec((rows, Q_LORA), lambda i: (i, COL_CQ // Q_LORA)),
            pl.BlockSpec((rows, KV_LORA), lambda i: (i, COL_CKV // KV_LORA)),
            pl.BlockSpec((rows, HEAD_W), lambda i: (i, COL_KR // HEAD_W)),
            pl.BlockSpec((rows, HEAD_W), lambda i: (i % npos, 0)),
            pl.BlockSpec((rows, HEAD_W), lambda i: (i % npos, 0)),
            pl.BlockSpec((1, Q_LORA), const),
            pl.BlockSpec(wq.shape, const),
            pl.BlockSpec((1, KV_LORA), const),
            pl.BlockSpec(wk.shape, const),
            pl.BlockSpec(wv.shape, const),
        ],
        out_specs=[
            q_spec,
            pl.BlockSpec((rows, N_HEADS * HEAD_W), row),
            v_spec,
            pl.BlockSpec((rows, KV_LORA), row),
            pl.BlockSpec((rows, HEAD_W), row),
        ],
        out_shape=[
            q_shape,
            jax.ShapeDtypeStruct((n, N_HEADS * HEAD_W), BF16),
            v_shape,
            jax.ShapeDtypeStruct((n, KV_LORA), F32),
            jax.ShapeDtypeStruct((n, HEAD_W), F32),
        ],
        compiler_params=_cparams(("arbitrary",)),
        name="attn_prep",
    )(z, z, z, cc, ss, gq, wq, gkv, wk, wv)


def _cache_expand_kernel(ckv_ref, kr_ref, wk_ref, wv_ref, place_ref, k_ref, v_ref):
    ckv_b = ckv_ref[...].astype(BF16)
    krp = _dot(kr_ref[...].astype(BF16), place_ref[...])
    v_ref[...] = _dot(ckv_b, wv_ref[...]).astype(BF16)
    for h in range(N_HEADS):
        sl = slice(h * HEAD_W, (h + 1) * HEAD_W)
        k_ref[:, sl] = (_dot(ckv_b, wk_ref[:, sl]) + krp).astype(BF16)


def _cache_expand(ckv, kr, wk, wv, place, rows):
    n = ckv.shape[0]
    row = lambda i: (i, 0)
    const = lambda i: (0, 0)
    return pl.pallas_call(
        _cache_expand_kernel,
        grid=(n // rows,),
        in_specs=[
            pl.BlockSpec((rows, KV_LORA), row),
            pl.BlockSpec((rows, QK_ROPE), row),
            pl.BlockSpec(wk.shape, const),
            pl.BlockSpec(wv.shape, const),
            pl.BlockSpec(place.shape, const),
        ],
        out_specs=[
            pl.BlockSpec((rows, N_HEADS * HEAD_W), row),
            pl.BlockSpec((rows, N_HEADS * V_HEAD), row),
        ],
        out_shape=[
            jax.ShapeDtypeStruct((n, N_HEADS * HEAD_W), BF16),
            jax.ShapeDtypeStruct((n, N_HEADS * V_HEAD), BF16),
        ],
        compiler_params=_cparams(("arbitrary",)),
        name="cache_expand",
    )(ckv, kr, wk, wv, place)


def _flash_step(qt_ref, k_ref, vt_ref, m_ref, acc_ref, masked, q0, k0):
    tk, tq = k_ref.shape[0], qt_ref.shape[1]
    if masked:
        kc = (k0 + lax.broadcasted_iota(jnp.int32, (tk, tq), 0)) // CHUNK
        qc = (q0 + lax.broadcasted_iota(jnp.int32, (tk, tq), 1)) // CHUNK
        keep = kc <= qc
    def scores(h):
        sl = slice(h * HEAD_W, (h + 1) * HEAD_W)
        s = _dot(k_ref[:, sl], qt_ref[sl, :])
        return jnp.where(keep, s, NEG_INF) if masked else s

    def stats(h, s):
        m_prev = m_ref[h]
        m_new = jnp.maximum(m_prev, jnp.max(s, axis=0, keepdims=True))
        m_ref[h] = m_new
        return m_new, jnp.exp2(m_prev - m_new)

    s = {0: scores(0)}
    if N_HEADS > 1:
        s[1] = scores(1)
    st = {0: stats(0, s[0])}
    for h in range(N_HEADS):
        vs = slice(h * V_ROWS, (h + 1) * V_ROWS)
        if h + 2 < N_HEADS:
            s[h + 2] = scores(h + 2)
        if h + 1 < N_HEADS:
            st[h + 1] = stats(h + 1, s[h + 1])
        m_new, alpha = st.pop(h)
        p = jnp.exp2(s.pop(h) - m_new).astype(BF16)
        vt_h = jnp.concatenate([vt_ref[j, vs, :] for j in range(vt_ref.shape[0])], axis=1)
        acc_ref[h] = alpha * acc_ref[h] + _dot(vt_h, p)


def _flash_kernel(qi_tab, ki_tab, qt_ref, k_ref, vt_ref, o_ref, m_ref, acc_ref, *, ratio):
    step = pl.program_id(1)
    qi = qi_tab[step]
    ki = ki_tab[step]
    tk, tq = k_ref.shape[0], qt_ref.shape[1]

    @pl.when(ki == 0)
    def _():
        m_ref[...] = jnp.full(m_ref.shape, NEG_INF, F32)
        acc_ref[...] = jnp.zeros(acc_ref.shape, F32)

    @pl.when(ki < qi * ratio)
    def _():
        _flash_step(qt_ref, k_ref, vt_ref, m_ref, acc_ref, False, 0, 0)

    @pl.when(ki >= qi * ratio)
    def _():
        _flash_step(qt_ref, k_ref, vt_ref, m_ref, acc_ref, True, qi * tq, ki * tk)

    @pl.when(ki == (qi + 1) * ratio - 1)
    def _():
        for h in range(N_HEADS):
            a = acc_ref[h]
            o = a[:V_HEAD, :] / a[V_HEAD:V_HEAD + 1, :]
            o_ref[:, h * V_HEAD:(h + 1) * V_HEAD] = o.T.astype(o_ref.dtype)


def _flash_prompt(qt, k, vt, b, t, tq, tk):
    nq, nk = t // tq, t // tk
    ratio = tq // tk
    vt_tiles = tk // vt.shape[2]
    pairs = [(qi, ki) for qi in range(nq) for ki in range((qi + 1) * ratio)]
    qi_tab = jnp.asarray([p[0] for p in pairs], jnp.int32)
    ki_tab = jnp.asarray([p[1] for p in pairs], jnp.int32)
    grid_spec = pltpu.PrefetchScalarGridSpec(
        num_scalar_prefetch=2,
        grid=(b, len(pairs)),
        in_specs=[
            pl.BlockSpec((N_HEADS * HEAD_W, tq), lambda bi, s, qtab, ktab: (0, bi * nq + qtab[s])),
            pl.BlockSpec((tk, N_HEADS * HEAD_W), lambda bi, s, qtab, ktab: (bi * nk + ktab[s], 0)),
            pl.BlockSpec((vt_tiles, N_HEADS * V_ROWS, vt.shape[2]),
                         lambda bi, s, qtab, ktab: (bi * nk + ktab[s], 0, 0)),
        ],
        out_specs=pl.BlockSpec((tq, N_HEADS * V_HEAD), lambda bi, s, qtab, ktab: (bi * nq + qtab[s], 0)),
        scratch_shapes=[
            pltpu.VMEM((N_HEADS, 1, tq), F32),
            pltpu.VMEM((N_HEADS, V_ROWS, tq), F32),
        ],
    )
    return pl.pallas_call(
        functools.partial(_flash_kernel, ratio=ratio),
        grid_spec=grid_spec,
        out_shape=jax.ShapeDtypeStruct((b * t, N_HEADS * V_HEAD), BF16),
        compiler_params=_cparams(("arbitrary", "arbitrary")),
        name="flash_prompt",
    )(qi_tab, ki_tab, qt, k, vt)


def _attn_sample_kernel(q_ref, kc_ref, vc_ref, kn_ref, vn_ref, o_ref):
    for h in range(N_HEADS):
        sl = slice(h * HEAD_W, (h + 1) * HEAD_W)
        vs = slice(h * V_HEAD, (h + 1) * V_HEAD)
        qh = q_ref[:, sl]
        sc = _dot_nt(qh, kc_ref[:, sl])
        sn = _dot_nt(qh, kn_ref[:, sl])
        m = jnp.maximum(jnp.max(sc, axis=1, keepdims=True), jnp.max(sn, axis=1, keepdims=True))
        pc = jnp.exp2(sc - m)
        pn = jnp.exp2(sn - m)
        den = jnp.sum(pc, axis=1, keepdims=True) + jnp.sum(pn, axis=1, keepdims=True)
        o = _dot(pc.astype(BF16), vc_ref[:, vs]) + _dot(pn.astype(BF16), vn_ref[:, vs])
        o_ref[:, vs] = (o / den).astype(o_ref.dtype)


def _attn_sample(q, kc, vc, kn, vn, b, t, past):
    return pl.pallas_call(
        _attn_sample_kernel,
        grid=(b,),
        in_specs=[
            pl.BlockSpec((t, N_HEADS * HEAD_W), lambda i: (i, 0)),
            pl.BlockSpec((past, N_HEADS * HEAD_W), lambda i: (i, 0)),
            pl.BlockSpec((past, N_HEADS * V_HEAD), lambda i: (i, 0)),
            pl.BlockSpec((t, N_HEADS * HEAD_W), lambda i: (i, 0)),
            pl.BlockSpec((t, N_HEADS * V_HEAD), lambda i: (i, 0)),
        ],
        out_specs=pl.BlockSpec((t, N_HEADS * V_HEAD), lambda i: (i, 0)),
        out_shape=jax.ShapeDtypeStruct((b * t, N_HEADS * V_HEAD), BF16),
        compiler_params=_cparams(("arbitrary",)),
        name="attn_sample",
    )(q, kc, vc, kn, vn)


def _mix_kernel(x_ref, gt_ref, ubch_ref, g_ref, attn_ref, pu_ref, pch_ref, hp_ref, hc_ref,
                wpool_ref, pscale_ref, convw_ref, wbr_ref, wout_ref,
                xo_ref, ptail_ref, ctail_ref, extp_ref, extc_ref, *, n_hist):
    gg, tg, d = x_ref.shape
    j = pl.program_id(1)
    first = j == 0

    ubch = ubch_ref[...].astype(F32)
    u = ubch[:, COL_U:COL_U + POOL_W].reshape(gg, tg, POOL_W)
    bgate = ubch[:, COL_B:COL_B + CONV_W].reshape(gg, tg, CONV_W)
    cu = (ubch[:, COL_C:COL_C + CONV_W] * ubch[:, COL_H:COL_H + CONV_W]).reshape(gg, tg, CONV_W)

    hist_p = jnp.where(first, hp_ref[...], pu_ref[...].astype(F32).reshape(1, HIST_ROWS, POOL_W))
    pch = pch_ref[...].astype(F32)
    prev_cu = (pch[:, :CONV_W] * pch[:, CONV_W:])[HIST_ROWS - CONV_HIST_ROWS:]
    hist_c = jnp.where(first, hc_ref[...], prev_cu.reshape(1, CONV_HIST_ROWS, CONV_W))

    extp_ref[:, :HIST_ROWS, :] = hist_p
    extp_ref[:, HIST_ROWS:, :] = u
    extc_ref[:, :CONV_HIST_ROWS, :] = hist_c
    extc_ref[:, CONV_HIST_ROWS:, :] = cu
    ptail_ref[...] = u[:, tg - HIST_ROWS:, :]
    ctail_ref[...] = cu[:, tg - CONV_HIST_ROWS:, :]

    tpos = j * tg + lax.broadcasted_iota(jnp.int32, (1, tg, 1), 1)
    pooled = []
    for gi, w in enumerate(POOL_WINDOWS):
        cs = slice(gi * POOL_GROUP, (gi + 1) * POOL_GROUP)
        acc = u[:, :, cs]
        for kk in range(1, w):
            acc = acc + extp_ref[:, HIST_ROWS - kk:HIST_ROWS - kk + tg, cs]
        cnt = jnp.minimum(tpos + 1 + n_hist, w).astype(F32)
        dd = acc / cnt - u[:, :, cs]
        pooled.append(_dot(dd.reshape(gg * tg, POOL_GROUP).astype(BF16), wpool_ref[gi]))
    pool = jnp.concatenate(pooled, axis=1) * pscale_ref[...]

    cw = convw_ref[...]
    yc = (cw[0:1, :] * extc_ref[:, CONV_HIST_ROWS - 2:CONV_HIST_ROWS - 2 + tg, :]
          + cw[1:2, :] * extc_ref[:, CONV_HIST_ROWS - 1:CONV_HIST_ROWS - 1 + tg, :]
          + cw[2:3, :] * cu)
    conv = (bgate * yc).reshape(gg * tg, CONV_W)

    gates = jax.nn.sigmoid(g_ref[...].astype(F32))
    mixed = (gates[:, 0:d] * _dot(attn_ref[...], wbr_ref[0])
             + gates[:, d:2 * d] * _dot(pool.astype(BF16), wbr_ref[1])
             + gates[:, 2 * d:3 * d] * _dot(conv.astype(BF16), wbr_ref[2]))
    out = _dot(mixed.astype(BF16), wout_ref[...])
    xo_ref[...] = x_ref[...] + gt_ref[...] * out.reshape(gg, tg, d)


def _mix(x3, gt, z, attn, hist_p, hist_c, wpool, pscale, convw, wbr, wout, rows, n_hist):
    b, t, d = x3.shape
    gg, tg = _row_plan(b, t, rows)
    nj = t // tg
    rb = lambda i, j: i * nj + j
    hpb = tg // HIST_ROWS

    def prev_rows(i, j):
        return jnp.maximum(rb(i, j) * hpb - 1, 0)

    const2 = lambda i, j: (0, 0)
    const3 = lambda i, j: (0, 0, 0)
    return pl.pallas_call(
        functools.partial(_mix_kernel, n_hist=n_hist),
        grid=(b // gg, nj),
        in_specs=[
            pl.BlockSpec((gg, tg, d), lambda i, j: (i, j, 0)),
            pl.BlockSpec((gg, 1, d), lambda i, j: (i, 0, 0)),
            pl.BlockSpec((gg * tg, COL_CQ), lambda i, j: (rb(i, j), 0)),
            pl.BlockSpec((gg * tg, 3 * d), lambda i, j: (rb(i, j), COL_G // (3 * d))),
            pl.BlockSpec((gg * tg, N_HEADS * V_HEAD), lambda i, j: (rb(i, j), 0)),
            pl.BlockSpec((HIST_ROWS, POOL_W), lambda i, j: (prev_rows(i, j), 0)),
            pl.BlockSpec((HIST_ROWS, 2 * CONV_W), lambda i, j: (prev_rows(i, j), COL_C // (2 * CONV_W))),
            pl.BlockSpec((gg, HIST_ROWS, POOL_W), lambda i, j: (i, 0, 0)),
            pl.BlockSpec((gg, CONV_HIST_ROWS, CONV_W), lambda i, j: (i, 0, 0)),
            pl.BlockSpec(wpool.shape, const3),
            pl.BlockSpec(pscale.shape, const2),
            pl.BlockSpec(convw.shape, const2),
            pl.BlockSpec(wbr.shape, const3),
            pl.BlockSpec(wout.shape, const2),
        ],
        out_specs=[
            pl.BlockSpec((gg, tg, d), lambda i, j: (i, j, 0)),
            pl.BlockSpec((gg, HIST_ROWS, POOL_W), lambda i, j: (i, 0, 0)),
            pl.BlockSpec((gg, CONV_HIST_ROWS, CONV_W), lambda i, j: (i, 0, 0)),
        ],
        out_shape=[
            jax.ShapeDtypeStruct((b, t, d), F32),
            jax.ShapeDtypeStruct((b, HIST_ROWS, POOL_W), F32),
            jax.ShapeDtypeStruct((b, CONV_HIST_ROWS, CONV_W), F32),
        ],
        scratch_shapes=[
            pltpu.VMEM((gg, HIST_ROWS + tg, POOL_W), F32),
            pltpu.VMEM((gg, CONV_HIST_ROWS + tg, CONV_W), F32),
        ],
        compiler_params=_cparams(("arbitrary", "arbitrary")),
        name="mix_merge",
    )(x3, gt, z, z, attn, z, z, hist_p, hist_c, wpool, pscale, convw, wbr, wout)


def _peer_query_kernel(x_ref, g_ref, sh_ref, sc_ref, wq_ref, keys_ref, ht_ref, s1_ref, s2_ref):
    gg, tg, d = x_ref.shape
    h = _modnorm(x_ref[...], g_ref[...], sh_ref[...], sc_ref[...]).reshape(gg * tg, d)
    ht_ref[0] = h.T.astype(BF16)
    hb = h.astype(BF16)
    half = D_KEY // 2
    k1 = keys_ref[0]
    k2 = keys_ref[1]
    def query(hh):
        return _dot(hb, wq_ref[:, hh * D_KEY:(hh + 1) * D_KEY]).astype(BF16)

    q_next = query(0)
    for hh in range(PEER_HEADS):
        q = q_next
        if hh + 1 < PEER_HEADS:
            q_next = query(hh + 1)
        s1_ref[hh, 0] = _dot_nt(k1, q[:, :half])
        s2_ref[hh, 0] = _dot_nt(k2, q[:, half:])


def _peer_query(x3, g, sh, sc, wq, keys, rows):
    b, t, d = x3.shape
    gg, tg = _row_plan(b, t, rows)
    nj = t // tg
    nt = b * t // rows
    tok = lambda i, j: (0, i * nj + j, 0, 0)
    return pl.pallas_call(
        _peer_query_kernel,
        grid=(b // gg, nj),
        in_specs=[
            pl.BlockSpec((gg, tg, d), lambda i, j: (i, j, 0)),
            pl.BlockSpec((1, d), lambda i, j: (0, 0)),
            pl.BlockSpec((gg, 1, d), lambda i, j: (i, 0, 0)),
            pl.BlockSpec((gg, 1, d), lambda i, j: (i, 0, 0)),
            pl.BlockSpec(wq.shape, lambda i, j: (0, 0)),
            pl.BlockSpec(keys.shape, lambda i, j: (0, 0, 0)),
        ],
        out_specs=[
            pl.BlockSpec((1, d, rows), lambda i, j: (i * nj + j, 0, 0)),
            pl.BlockSpec((PEER_HEADS, 1, N_KEYS, rows), tok),
            pl.BlockSpec((PEER_HEADS, 1, N_KEYS, rows), tok),
        ],
        out_shape=[
            jax.ShapeDtypeStruct((nt, d, rows), BF16),
            jax.ShapeDtypeStruct((PEER_HEADS, nt, N_KEYS, rows), F32),
            jax.ShapeDtypeStruct((PEER_HEADS, nt, N_KEYS, rows), F32),
        ],
        compiler_params=_cparams(("arbitrary", "arbitrary")),
        name="peer_query",
    )(x3, g, sh, sc, wq, keys)


def _top_extract(s, with_rank):
    nk, r = s.shape
    iota = lax.broadcasted_iota(jnp.int32, (nk, r), 0).astype(F32)
    iota_k = lax.broadcasted_iota(jnp.int32, (PEER_TOPK, r), 0)
    rank = jnp.full((nk, r), NOT_TOP, F32) if with_rank else None
    vals = jnp.zeros((PEER_TOPK, r), F32)
    idxs = jnp.zeros((PEER_TOPK, r), F32)
    x = s
    for it in range(PEER_TOPK):
        m = jnp.max(x, axis=0, keepdims=True)
        idx = jnp.min(jnp.where(x == m, iota, float(nk)), axis=0, keepdims=True)
        hit = iota == idx
        if with_rank:
            rank = jnp.where(hit, float(it), rank)
        x = jnp.where(hit, -jnp.inf, x)
        vals = jnp.where(iota_k == it, m, vals)
        idxs = jnp.where(iota_k == it, idx, idxs)
    return rank, vals, idxs


def _oddeven_mergesort_pairs(n):
    pairs = []

    def merge(lo, hi, step):
        nxt = step * 2
        if nxt < hi - lo:
            merge(lo, hi, nxt)
            merge(lo + step, hi, nxt)
            for i in range(lo + step, hi - step, nxt):
                pairs.append((i, i + step))
        else:
            pairs.append((lo, lo + step))

    def sort(lo, hi):
        if hi - lo >= 1:
            mid = lo + (hi - lo) // 2
            sort(lo, mid)
            sort(mid + 1, hi)
            merge(lo, hi, 1)

    sort(0, n - 1)
    return pairs


SUBLANES = 8
SORT_PAIRS = _oddeven_mergesort_pairs(N_KEYS // SUBLANES)


def _sorted_top(s):
    n = N_KEYS // SUBLANES
    cols = [s[SUBLANES * v:SUBLANES * (v + 1), :] for v in range(n)]

    def exchange(i, j):
        hi, lo = jnp.maximum(cols[i], cols[j]), jnp.minimum(cols[i], cols[j])
        cols[i], cols[j] = hi, lo

    for i, j in SORT_PAIRS:
        exchange(i, j)
    shift = SUBLANES // 2
    while shift >= 1:
        other = [pltpu.roll(c, shift, axis=0) for c in cols]
        cols = [jnp.maximum(cols[p], other[n - 1 - p]) for p in range(n)]
        dist = n // 2
        while dist >= 1:
            for p in range(n):
                if p & dist == 0:
                    exchange(p, p + dist)
            dist //= 2
        shift //= 2
    return cols


def _ranks_from_sorted(s, top):
    n = len(top)
    assert n == 16
    ranks = []
    n_sel = None
    for v in range(N_KEYS // SUBLANES):
        x = s[SUBLANES * v:SUBLANES * (v + 1), :]
        g0, g1, g2 = top[3] > x, top[7] > x, top[11] > x
        acc = jnp.where(g2, 12.0, jnp.where(g1, 8.0, jnp.where(g0, 4.0, 0.0)))
        for r_ in range(3):
            th = jnp.where(g2, top[12 + r_], jnp.where(g1, top[8 + r_], jnp.where(g0, top[4 + r_], top[r_])))
            acc = acc + jnp.where(th > x, 1.0, 0.0)
        sel = x >= top[n - 1]
        ranks.append(jnp.where(sel, acc, NOT_TOP))
        one = jnp.where(sel, 1.0, 0.0)
        n_sel = one if n_sel is None else n_sel + one
    n_sel = jnp.sum(n_sel, axis=0, keepdims=True)
    dup = jnp.where(top[0] == top[1], 1.0, 0.0)
    for a in range(1, n - 1):
        dup = jnp.maximum(dup, jnp.where(top[a] == top[a + 1], 1.0, 0.0))
    flag = jnp.maximum(jnp.max(dup, axis=0, keepdims=True), jnp.where(n_sel != float(n), 1.0, 0.0))
    return jnp.concatenate(ranks, axis=0), flag


def _stack_rows(top):
    r = top[0].shape[1]
    iota_k = lax.broadcasted_iota(jnp.int32, (PEER_TOPK, r), 0)
    out = jnp.zeros((PEER_TOPK, r), F32)
    for a, t in enumerate(top):
        out = jnp.where(iota_k == a, jnp.concatenate([t, t], axis=0), out)
    return out


def _peer_select_kernel(s1_ref, s2_ref, c_ref, lr_ref, rk2_ref, d_ref):
    refs = (s1_ref, s2_ref, c_ref, lr_ref, rk2_ref, d_ref)
    flag = None
    for h in range(s1_ref.shape[0]):
        f = _peer_select_sorted(h, *refs)
        flag = f if flag is None else jnp.maximum(flag, f)

    @pl.when(jnp.max(flag) > 0.0)
    def _():
        for h in range(s1_ref.shape[0]):
            _peer_select_head(h, *refs)


def _peer_select_sorted(h, s1_ref, s2_ref, c_ref, lr_ref, rk2_ref, d_ref):
    s1 = s1_ref[h, 0]
    s2 = s2_ref[h, 0]
    top1 = _sorted_top(s1)
    top2 = _sorted_top(s2)
    rank1, flag1 = _ranks_from_sorted(s1, top1)
    rank2, flag2 = _ranks_from_sorted(s2, top2)
    v1 = _stack_rows(top1)
    v2 = _stack_rows(top2)
    cnt, zsum = _merge_counts(v1, v2)
    lr = jnp.zeros(rank1.shape, F32)
    for a in range(PEER_TOPK):
        lr = jnp.where(rank1 == float(a), cnt[a:a + 1, :], lr)
    c_ref[h, 0] = jnp.exp(s1 - v1[0:1, :]) / zsum
    lr_ref[h, 0] = lr
    rk2_ref[h, 0] = rank2.astype(BF16)
    d_ref[h, 0] = jnp.exp(s2 - v2[0:1, :]).astype(BF16)
    return jnp.maximum(flag1, flag2)


def _merge_counts(v1, v2):
    r = v1.shape[1]
    iota_k = lax.broadcasted_iota(jnp.int32, (PEER_TOPK, r), 0).astype(F32)
    cnt = jnp.zeros((PEER_TOPK, r), F32)
    front = v1 + v2[0:1, :]
    top = front[0:1, :]
    zsum = jnp.zeros((1, r), F32)
    for _ in range(PEER_TOPK):
        m = jnp.max(front, axis=0, keepdims=True)
        a = jnp.min(jnp.where(front == m, iota_k, float(PEER_TOPK)), axis=0, keepdims=True)
        hit = iota_k == a
        zsum = zsum + jnp.exp(m - top)
        cnt = jnp.where(hit, cnt + 1.0, cnt)
        c_hit = jnp.max(jnp.where(hit, cnt, 0.0), axis=0, keepdims=True)
        v1_hit = jnp.max(jnp.where(hit, v1, -jnp.inf), axis=0, keepdims=True)
        nxt = jnp.full((1, r), -jnp.inf, F32)
        for bcol in range(1, PEER_TOPK):
            nxt = jnp.where(c_hit == float(bcol), v2[bcol:bcol + 1, :], nxt)
        front = jnp.where(hit, v1_hit + nxt, front)
    return cnt, zsum


def _peer_select_head(h, s1_ref, s2_ref, c_ref, lr_ref, rk2_ref, d_ref):
    s1 = s1_ref[h, 0]
    s2 = s2_ref[h, 0]
    nk, r = s1.shape
    _, v1, idx1 = _top_extract(s1, False)
    rank2, v2, _ = _top_extract(s2, True)
    cnt, zsum = _merge_counts(v1, v2)
    iota = lax.broadcasted_iota(jnp.int32, (nk, r), 0).astype(F32)
    lr = jnp.zeros((nk, r), F32)
    for a in range(PEER_TOPK):
        lr = jnp.where(iota == idx1[a:a + 1, :], cnt[a:a + 1, :], lr)
    c_ref[h, 0] = jnp.exp(s1 - v1[0:1, :]) / zsum
    lr_ref[h, 0] = lr
    rk2_ref[h, 0] = rank2.astype(BF16)
    d_ref[h, 0] = jnp.exp(s2 - v2[0:1, :]).astype(BF16)


def _peer_select(s1, s2):
    hh, nt, nk, lanes = s1.shape
    heads_per_step = 2
    spec = pl.BlockSpec((heads_per_step, 1, nk, lanes), lambda i, h: (h, i, 0, 0))
    f32 = jax.ShapeDtypeStruct(s1.shape, F32)
    b16 = jax.ShapeDtypeStruct(s1.shape, BF16)
    return pl.pallas_call(
        _peer_select_kernel,
        grid=(nt, hh // heads_per_step),
        in_specs=[spec, spec],
        out_specs=[spec, spec, spec, spec],
        out_shape=[f32, f32, b16, b16],
        compiler_params=_cparams(("arbitrary", "arbitrary")),
        name="peer_select",
    )(s1, s2)


def _gelu(x):
    return 0.5 * x * (1.0 + lax.erf(x * (2.0 ** -0.5)))


def _peer_dense_kernel(x_ref, gt_ref, ht_ref, u_ref, vt_ref, c_ref, lr_ref, rk2_ref, d_ref,
                       xo_ref, a_ref, w_ref, acc_ref, *, rows_per_chunk):
    gg, tg, d = x_ref.shape
    nt, _, lanes = ht_ref.shape
    e = pl.program_id(1)
    slab = BF16_SUBLANES
    n_slab = N_KEYS // slab

    @pl.when(e == 0)
    def _():
        acc_ref[...] = jnp.zeros(acc_ref.shape, F32)

    w_ref[(nt - 1) % 2] = jnp.zeros(w_ref.shape[1:], BF16)

    def tile(lt):
        prev = (lt + nt - 1) % nt
        a_ref[...] = _dot(u_ref[...], ht_ref[lt])
        acc_ref[prev] += _dot(vt_ref[...], w_ref[prev % 2])
        w_cur = w_ref.at[lt % 2]
        for ii in range(rows_per_chunk):
            cl = []
            for hh in range(PEER_HEADS):
                cl.append((
                    jnp.broadcast_to(c_ref[hh, lt, ii:ii + 1, :], (slab, lanes)).astype(BF16),
                    jnp.broadcast_to(lr_ref[hh, lt, ii:ii + 1, :], (slab, lanes)).astype(BF16)))
            for jv in range(n_slab):
                rs = slice(jv * slab, (jv + 1) * slab)
                gate = None
                for hh in range(PEER_HEADS):
                    dd = d_ref[hh, lt, rs, :]
                    term = cl[hh][0] * jnp.where(rk2_ref[hh, lt, rs, :] < cl[hh][1], dd,
                                                 jnp.zeros_like(dd))
                    gate = term if gate is None else gate + term
                rows = slice(ii * N_KEYS + jv * slab, ii * N_KEYS + (jv + 1) * slab)
                w_cur[rows, :] = _gelu(a_ref[rows, :]).astype(BF16) * gate

    for lt in range(nt):
        tile(lt)
    acc_ref[nt - 1] += _dot(vt_ref[...], w_ref[(nt - 1) % 2])

    @pl.when(e == pl.num_programs(1) - 1)
    def _():
        for lt in range(nt):
            upd = acc_ref[lt].T
            if gg == 1:
                rows = slice(lt * lanes, (lt + 1) * lanes)
                xo_ref[0, rows, :] = x_ref[0, rows, :] + gt_ref[0] * upd
            else:
                per = lanes // tg
                rows = slice(lt * per, (lt + 1) * per)
                xo_ref[rows] = x_ref[rows] + gt_ref[rows] * upd.reshape(per, tg, d)


def _peer_dense(x3, gt, ht, u, vt, c, lr, rk2, dd, tiles_per_block, rows_per_chunk):
    b, t, d = x3.shape
    nt_all, _, lanes = ht.shape
    gg, tg = _row_plan(b, t, tiles_per_block * lanes)
    nj = t // tg
    ne = u.shape[0]
    ec = rows_per_chunk * N_KEYS
    assert gg == 1 or lanes % tg == 0
    full = pl.BlockSpec((PEER_HEADS, tiles_per_block, N_KEYS, lanes), lambda i, e: (0, i, 0, 0))
    part = pl.BlockSpec((PEER_HEADS, tiles_per_block, rows_per_chunk, lanes),
                        lambda i, e: (0, i, e, 0))
    return pl.pallas_call(
        functools.partial(_peer_dense_kernel, rows_per_chunk=rows_per_chunk),
        grid=(nt_all // tiles_per_block, ne // ec),
        in_specs=[
            pl.BlockSpec((gg, tg, d), lambda i, e: (i // nj, i % nj, 0)),
            pl.BlockSpec((gg, 1, d), lambda i, e: (i // nj, 0, 0)),
            pl.BlockSpec((tiles_per_block, d, lanes), lambda i, e: (i, 0, 0)),
            pl.BlockSpec((ec, d), lambda i, e: (e, 0)),
            pl.BlockSpec((d, ec), lambda i, e: (0, e)),
            part, part, full, full,
        ],
        out_specs=pl.BlockSpec((gg, tg, d), lambda i, e: (i // nj, i % nj, 0)),
        out_shape=jax.ShapeDtypeStruct((b, t, d), F32),
        scratch_shapes=[
            pltpu.VMEM((ec, lanes), F32),
            pltpu.VMEM((2, ec, lanes), BF16),
            pltpu.VMEM((tiles_per_block, d, lanes), F32),
        ],
        compiler_params=_cparams(("arbitrary", "arbitrary")),
        name="peer_dense",
    )(x3, gt, ht, u, vt, c, lr, rk2, dd)


def _final_kernel(x_ref, g_ref, o_ref):
    o_ref[...] = _rmsnorm2(x_ref[...], g_ref[...])


def _final_norm(x3, g, rows):
    b, t, d = x3.shape
    x2 = x3.reshape(b * t, d)
    y = pl.pallas_call(
        _final_kernel,
        grid=(b * t // rows,),
        in_specs=[pl.BlockSpec((rows, d), lambda i: (i, 0)), pl.BlockSpec((1, d), lambda i: (0, 0))],
        out_specs=pl.BlockSpec((rows, d), lambda i: (i, 0)),
        out_shape=jax.ShapeDtypeStruct((b * t, d), F32),
        compiler_params=_cparams(("arbitrary",)),
        name="final_norm",
    )(x2, g)
    return y.reshape(b, t, d)


def _swap_halves(w):
    half = w.shape[-1] // 2
    return jnp.concatenate([w[..., half:], w[..., :half]], axis=-1)


def _prep_layer(p, l):
    d = p['w_in'].shape[1]
    w_in = p['w_in'][l]
    offs = [0]
    for nsz in (Q_LORA, KV_LORA, QK_ROPE, POOL_W, CONV_W, CONV_W, CONV_W, 3 * d):
        offs.append(offs[-1] + nsz)
    w_cq, w_ckv, w_kr, w_u, w_b, w_c, w_h, w_g = [w_in[:, offs[i]:offs[i + 1]] for i in range(8)]
    zeros = lambda n: jnp.zeros((d, n), w_in.dtype)
    w_krg = jnp.concatenate([zeros(QK_NOPE), w_kr, _swap_halves(w_kr)], axis=1)
    w_in_r = jnp.concatenate(
        [w_u, w_b, w_c, w_h, w_cq, w_ckv, w_krg, zeros(COL_G - COL_KR - HEAD_W), w_g], axis=1)
    assert w_in_r.shape[1] == D_IN_PAD

    w_uq = p['w_uq'][l]
    wq_r = jnp.concatenate(
        [w_uq[..., :QK_NOPE], w_uq[..., QK_NOPE:], _swap_halves(w_uq[..., QK_NOPE:])], axis=-1)
    wq_r = wq_r.reshape(Q_LORA, N_HEADS * HEAD_W)
    w_ukv = p['w_ukv'][l]
    wk_r = jnp.concatenate(
        [w_ukv[..., :QK_NOPE], jnp.zeros((KV_LORA, N_HEADS, HEAD_W - QK_NOPE), w_ukv.dtype)],
        axis=-1).reshape(KV_LORA, N_HEADS * HEAD_W)
    wv_r = w_ukv[..., QK_NOPE:].reshape(KV_LORA, N_HEADS * V_HEAD)
    return dict(
        w_in=w_in_r.astype(BF16), wq=wq_r.astype(BF16), wk=wk_r.astype(BF16), wv=wv_r.astype(BF16),
        g_mix=p['g_mix'][l][None, :], g_q=p['g_q'][l][None, :], g_kv=p['g_kv'][l][None, :],
        w_pool=p['w_pool'][l].astype(BF16), pool_scale=p['pool_scale'][l][None, :],
        conv_w=jnp.pad(p['conv_w'][l], ((0, 8 - CONV_K), (0, 0))),
        w_branch=p['w_branch'][l].astype(BF16), w_out=p['w_out'][l].astype(BF16),
        g_ffn=p['g_ffn'][l][None, :],
        peer_wq=p['peer_wq'][l].reshape(d, PEER_HEADS * D_KEY).astype(BF16),
        peer_keys=p['peer_keys'][l].astype(BF16),
        peer_u=p['peer_u'][l].astype(BF16),
        peer_vt=p['peer_v'][l].T.astype(BF16),
    )


def _rope_tables(pos):
    half = QK_ROPE // 2
    inv = ROPE_THETA ** (-jnp.arange(half, dtype=F32) / half)
    ang = pos.astype(F32)[:, None] * inv[None, :]
    cos, sin = jnp.cos(ang), jnp.sin(ang)
    z = lambda n: jnp.zeros((pos.shape[0], n), F32)
    cc = jnp.concatenate([z(QK_NOPE), cos, cos, z(QK_ROPE)], axis=1)
    ss = jnp.concatenate([z(QK_NOPE), -sin, sin, z(QK_ROPE)], axis=1)
    return cc, ss


def _trunk(x, mods, pos, n_hist, cache, hist_pool, hist_conv, layers, g_final, cfg):
    b, t, d = x.shape
    rows = cfg['rows']
    n = b * t
    cc, ss = _rope_tables(pos)
    if t < rows:
        cc = jnp.tile(cc, (rows // t, 1))
        ss = jnp.tile(ss, (rows // t, 1))
    place = jnp.concatenate(
        [jnp.zeros((QK_ROPE, QK_NOPE), F32), jnp.eye(QK_ROPE, dtype=F32),
         jnp.zeros((QK_ROPE, HEAD_W - QK_NOPE - QK_ROPE), F32)], axis=1).astype(BF16)
    new_kv, new_kr, new_pool, new_conv = [], [], [], []
    for l, lw in enumerate(layers):
        mod = mods[l].reshape(b, 1, 6 * d)
        sh1, sc1, gt1, sh2, sc2, gt2 = [mod[:, :, i * d:(i + 1) * d] for i in range(6)]
        z = _inproj(x, lw['g_mix'], sh1, sc1, lw['w_in'], rows)
        q, k, v, ckv, krg = _attn_prep(z, cc, ss, lw['g_q'], lw['wq'], lw['g_kv'], lw['wk'],
                                       lw['wv'], rows, cache is None)
        new_kv.append(ckv.reshape(b, t, KV_LORA))
        new_kr.append(krg[:, QK_NOPE:QK_NOPE + QK_ROPE].reshape(b, t, QK_ROPE))
        if cache is None:
            attn = _flash_prompt(q, k, v, b, t, cfg['attn_tq'], cfg['attn_tk'])
        else:
            ckv_c, kr_c = cache
            past = ckv_c.shape[2]
            kc, vc = _cache_expand(ckv_c[l].reshape(b * past, KV_LORA),
                                   kr_c[l].reshape(b * past, QK_ROPE),
                                   lw['wk'], lw['wv'], place, cfg['cache_rows'])
            attn = _attn_sample(q, kc, vc, k, v, b, t, past)
        hp = jnp.pad(hist_pool[l], ((0, 0), (HIST_ROWS - POOL_HIST, 0), (0, 0)))
        hc = jnp.pad(hist_conv[l], ((0, 0), (CONV_HIST_ROWS - (CONV_K - 1), 0), (0, 0)))
        x, ptail, ctail = _mix(x, gt1, z, attn, hp, hc, lw['w_pool'], lw['pool_scale'],
                               lw['conv_w'], lw['w_branch'], lw['w_out'], rows, n_hist)
        new_pool.append(ptail[:, HIST_ROWS - POOL_HIST:, :])
        new_conv.append(ctail[:, CONV_HIST_ROWS - (CONV_K - 1):, :])
        ht, s1, s2 = _peer_query(x, lw['g_ffn'], sh2, sc2, lw['peer_wq'], lw['peer_keys'], rows)
        c, lr, rk2, dd = _peer_select(s1, s2)
        x = _peer_dense(x, gt2, ht, lw['peer_u'], lw['peer_vt'], c, lr, rk2, dd,
                        min(cfg['peer_tiles'], n // rows), cfg['rows_per_chunk'])
    y = _final_norm(x, g_final[None, :], rows)
    return y, jnp.stack(new_kv), jnp.stack(new_kr), jnp.stack(new_pool), jnp.stack(new_conv)


def _config(t_prompt):
    rows = min(256, t_prompt)
    return dict(rows=rows, attn_tq=min(512, t_prompt), attn_tk=min(512, t_prompt), cache_rows=1024, peer_tiles=4,
                rows_per_chunk=8)


def kernel(x_prompt, x_sample, cache_kv_latent, cache_k_rope, state_pool, state_conv,
           c_prompt, c_sample, w_ada, b_ada, g_mix, w_in, g_q, w_uq, g_kv, w_ukv,
           w_pool, pool_scale, conv_w, w_branch, w_out, g_ffn, peer_wq, peer_keys,
           peer_u, peer_v, g_final):
    p = {'w_in': w_in, 'g_mix': g_mix, 'g_q': g_q, 'w_uq': w_uq, 'g_kv': g_kv, 'w_ukv': w_ukv,
         'w_pool': w_pool, 'pool_scale': pool_scale, 'conv_w': conv_w, 'w_branch': w_branch,
         'w_out': w_out, 'g_ffn': g_ffn, 'peer_wq': peer_wq, 'peer_keys': peer_keys,
         'peer_u': peer_u, 'peer_v': peer_v}
    depth = w_ada.shape[0]
    bp, tp, d = x_prompt.shape
    bs, ts, _ = x_sample.shape
    past = cache_kv_latent.shape[2]
    layers = [_prep_layer(p, l) for l in range(depth)]

    c_all = jnp.concatenate([c_prompt, c_sample], axis=0)
    pad = (-c_all.shape[0]) % 8
    c_all = jnp.pad(c_all, ((0, pad), (0, 0)))
    mods = _ada(c_all, w_ada, b_ada)
    mods_p, mods_s = mods[:, :bp], mods[:, bp:bp + bs]

    cfg = _config(tp)
    zp = jnp.zeros((depth, bp, POOL_HIST, POOL_W), x_prompt.dtype)
    zc = jnp.zeros((depth, bp, CONV_K - 1, CONV_W), x_prompt.dtype)
    y_p, p_kv, p_kr, p_pool, p_conv = _trunk(
        x_prompt, mods_p, jnp.arange(tp), 0, None, zp, zc, layers, g_final, cfg)
    y_s, s_kv, s_kr, s_pool, s_conv = _trunk(
        x_sample, mods_s, past + jnp.arange(ts), min(past, POOL_HIST),
        (cache_kv_latent, cache_k_rope), state_pool, state_conv, layers, g_final, cfg)
    return (y_p, y_s, p_kv, p_kr, p_pool, p_conv, s_kv, s_kr, s_pool, s_conv)
```

```python
import functools
import math

import jax
import jax.numpy as jnp
from jax import lax
from jax.experimental import pallas as pl
from jax.experimental.pallas import tpu as pltpu

F32 = jnp.float32
BF16 = jnp.bfloat16

EPS = 1e-6
N_HEADS = 8
QK_NOPE = 64
QK_ROPE = 32
V_HEAD = 64
Q_LORA = 512
KV_LORA = 256
ROPE_THETA = 10000.0
CHUNK = 64
ATTN_SCALE = (QK_NOPE + QK_ROPE) ** -0.5
NEG_INF = -1e30
POOL_WINDOWS = (2, 4, 8, 16)
POOL_GROUP = 128
POOL_W = 512
POOL_HIST = 15
CONV_W = 512
CONV_K = 3
PEER_HEADS = 8
N_KEYS = 128
D_KEY = 256
PEER_TOPK = 16
NOT_TOP = 99.0

LANES = 128
BF16_SUBLANES = 16
VMEM_LIMIT_BYTES = 56 * 2**20

HEAD_W = LANES
V_ROWS = V_HEAD + 16
Q_SCALE = ATTN_SCALE * math.log2(math.e)
HIST_ROWS = 16
CONV_HIST_ROWS = 8

COL_U, COL_B, COL_C, COL_H = 0, 512, 1024, 1536
COL_CQ = 2048
COL_CKV = 2560
COL_KR = 2816
COL_G = 3072
D_IN_PAD = 6144


def _cparams(sem):
    return pltpu.CompilerParams(dimension_semantics=sem, vmem_limit_bytes=VMEM_LIMIT_BYTES)


def _dot(a, b):
    return jnp.dot(a, b, preferred_element_type=F32)


def _dot_nt(a, b):
    return lax.dot_general(a, b, (((1,), (1,)), ((), ())), preferred_element_type=F32)


def _ada_kernel(c_ref, w_ref, b_ref, o_ref):
    c = c_ref[...]
    act = c * jax.nn.sigmoid(c)
    o_ref[0] = _dot(act.astype(BF16), w_ref[0].astype(BF16)) + b_ref[0]


def _ada(c_all, w_ada, b_ada):
    depth, d, n6 = w_ada.shape
    bp = c_all.shape[0]
    tn = 1536
    return pl.pallas_call(
        _ada_kernel,
        grid=(depth, n6 // tn),
        in_specs=[
            pl.BlockSpec((bp, d), lambda l, j: (0, 0)),
            pl.BlockSpec((1, d, tn), lambda l, j: (l, 0, j)),
            pl.BlockSpec((1, 1, tn), lambda l, j: (l, 0, j)),
        ],
        out_specs=pl.BlockSpec((1, bp, tn), lambda l, j: (l, 0, j)),
        out_shape=jax.ShapeDtypeStruct((depth, bp, n6), F32),
        compiler_params=_cparams(("arbitrary", "arbitrary")),
        name="ada",
    )(c_all, w_ada, b_ada.reshape(depth, 1, n6))


def _modnorm(x, g, sh, sc):
    ms = jnp.mean(x * x, axis=-1, keepdims=True)
    y = x * lax.rsqrt(ms + EPS) * g
    return y * (1.0 + sc) + sh


def _rmsnorm2(x, g):
    ms = jnp.mean(x * x, axis=-1, keepdims=True)
    return x * lax.rsqrt(ms + EPS) * g


def _row_plan(b, t, rows):
    if t >= rows:
        assert t % rows == 0
        return 1, rows
    assert rows % t == 0 and b % (rows // t) == 0
    return rows // t, t


def _inproj_kernel(x_ref, g_ref, sh_ref, sc_ref, w_ref, o_ref, *, col_chunk):
    gg, tg, d = x_ref.shape
    h = _modnorm(x_ref[...], g_ref[...], sh_ref[...], sc_ref[...])
    hb = h.reshape(gg * tg, d).astype(BF16)
    n = w_ref.shape[1]
    for c in range(0, n, col_chunk):
        o_ref[:, c:c + col_chunk] = _dot(hb, w_ref[:, c:c + col_chunk]).astype(o_ref.dtype)


def _inproj(x3, g, sh, sc, w, rows):
    b, t, d = x3.shape
    gg, tg = _row_plan(b, t, rows)
    nj = t // tg
    n = w.shape[1]
    return pl.pallas_call(
        functools.partial(_inproj_kernel, col_chunk=512),
        grid=(b // gg, nj),
        in_specs=[
            pl.BlockSpec((gg, tg, d), lambda i, j: (i, j, 0)),
            pl.BlockSpec((1, d), lambda i, j: (0, 0)),
            pl.BlockSpec((gg, 1, d), lambda i, j: (i, 0, 0)),
            pl.BlockSpec((gg, 1, d), lambda i, j: (i, 0, 0)),
            pl.BlockSpec((d, n), lambda i, j: (0, 0)),
        ],
        out_specs=pl.BlockSpec((gg * tg, n), lambda i, j: (i * nj + j, 0)),
        out_shape=jax.ShapeDtypeStruct((b * t, n), BF16),
        compiler_params=_cparams(("arbitrary", "arbitrary")),
        name="inproj",
    )(x3, g, sh, sc, w)


def _rope_group(z, cc, ss):
    return z * cc + pltpu.roll(z, HEAD_W - QK_ROPE, axis=1) * ss


def _attn_prep_kernel(cq_ref, ckv_ref, kr_ref, cc_ref, ss_ref, gq_ref, wq_ref, gkv_ref,
                      wk_ref, wv_ref, q_ref, k_ref, v_ref, kv_ref, kro_ref, *, transposed):
    cck = cc_ref[...]
    ss = ss_ref[...]
    lane = lax.broadcasted_iota(jnp.int32, cck.shape, 1)
    ccq = jnp.where(lane < QK_NOPE, 1.0, cck)
    qn = _rmsnorm2(cq_ref[...].astype(F32), gq_ref[...]).astype(BF16)
    ckv = _rmsnorm2(ckv_ref[...].astype(F32), gkv_ref[...])
    kv_ref[...] = ckv
    ckv_b = ckv.astype(BF16)
    kr = _rope_group(kr_ref[...].astype(F32), cck, ss)
    kro_ref[...] = kr
    v = _dot(ckv_b, wv_ref[...])
    rows = v.shape[0]
    if transposed:
        ones = jnp.ones((V_ROWS - V_HEAD, rows), F32)
        vt = v.T
        for h in range(N_HEADS):
            v_ref[0, h * V_ROWS:(h + 1) * V_ROWS, :] = jnp.concatenate(
                [vt[h * V_HEAD:(h + 1) * V_HEAD, :], ones], axis=0).astype(BF16)
    else:
        v_ref[...] = v.astype(BF16)
    for h in range(N_HEADS):
        sl = slice(h * HEAD_W, (h + 1) * HEAD_W)
        zq = _dot(qn, wq_ref[:, sl])
        qh = _rope_group(zq, ccq, ss) * Q_SCALE
        if transposed:
            q_ref[sl, :] = qh.T.astype(BF16)
        else:
            q_ref[:, sl] = qh.astype(BF16)
        k_ref[:, sl] = (_dot(ckv_b, wk_ref[:, sl]) + kr).astype(BF16)


def _attn_prep(z, cc, ss, gq, wq, gkv, wk, wv, rows, transposed):
    n = z.shape[0]
    npos = cc.shape[0] // rows
    row = lambda i: (i, 0)
    col = lambda i: (0, i)
    const = lambda i: (0, 0)
    if transposed:
        q_spec = pl.BlockSpec((N_HEADS * HEAD_W, rows), col)
        q_shape = jax.ShapeDtypeStruct((N_HEADS * HEAD_W, n), BF16)
        v_spec = pl.BlockSpec((1, N_HEADS * V_ROWS, rows), lambda i: (i, 0, 0))
        v_shape = jax.ShapeDtypeStruct((n // rows, N_HEADS * V_ROWS, rows), BF16)
    else:
        q_spec = pl.BlockSpec((rows, N_HEADS * HEAD_W), row)
        q_shape = jax.ShapeDtypeStruct((n, N_HEADS * HEAD_W), BF16)
        v_spec = pl.BlockSpec((rows, N_HEADS * V_HEAD), row)
        v_shape = jax.ShapeDtypeStruct((n, N_HEADS * V_HEAD), BF16)
    return pl.pallas_call(
        functools.partial(_attn_prep_kernel, transposed=transposed),
        grid=(n // rows,),
        in_specs=[
            pl.BlockSpec((rows, Q_LORA), lambda i: (i, COL_CQ // Q_LORA)),
            pl.BlockSpec((rows, KV_LORA), lambda i: (i, COL_CKV // KV_LORA)),
            pl.BlockSpec((rows, HEAD_W), lambda i: (i, COL_KR // HEAD_W)),
            pl.BlockSpec((rows, HEAD_W), lambda i: (i % npos, 0)),
            pl.BlockSpec((rows, HEAD_W), lambda i: (i % npos, 0)),
            pl.BlockSpec((1, Q_LORA), const),
            pl.BlockSpec(wq.shape, const),
            pl.BlockSpec((1, KV_LORA), const),
            pl.BlockSpec(wk.shape, const),
            pl.BlockSpec(wv.shape, const),
        ],
        out_specs=[
            q_spec,
            pl.BlockSpec((rows, N_HEADS * HEAD_W), row),
            v_spec,
            pl.BlockSpec((rows, KV_LORA), row),
            pl.BlockSpec((rows, HEAD_W), row),
        ],
        out_shape=[
            q_shape,
            jax.ShapeDtypeStruct((n, N_HEADS * HEAD_W), BF16),
            v_shape,
            jax.ShapeDtypeStruct((n, KV_LORA), F32),
            jax.ShapeDtypeStruct((n, HEAD_W), F32),
        ],
        compiler_params=_cparams(("arbitrary",)),
        name="attn_prep",
    )(z, z, z, cc, ss, gq, wq, gkv, wk, wv)


def _cache_expand_kernel(ckv_ref, kr_ref, wk_ref, wv_ref, place_ref, k_ref, v_ref):
    ckv_b = ckv_ref[...].astype(BF16)
    krp = _dot(kr_ref[...].astype(BF16), place_ref[...])
    v_ref[...] = _dot(ckv_b, wv_ref[...]).astype(BF16)
    for h in range(N_HEADS):
        sl = slice(h * HEAD_W, (h + 1) * HEAD_W)
        k_ref[:, sl] = (_dot(ckv_b, wk_ref[:, sl]) + krp).astype(BF16)


def _cache_expand(ckv, kr, wk, wv, place, rows):
    n = ckv.shape[0]
    row = lambda i: (i, 0)
    const = lambda i: (0, 0)
    return pl.pallas_call(
        _cache_expand_kernel,
        grid=(n // rows,),
        in_specs=[
            pl.BlockSpec((rows, KV_LORA), row),
            pl.BlockSpec((rows, QK_ROPE), row),
            pl.BlockSpec(wk.shape, const),
            pl.BlockSpec(wv.shape, const),
            pl.BlockSpec(place.shape, const),
        ],
        out_specs=[
            pl.BlockSpec((rows, N_HEADS * HEAD_W), row),
            pl.BlockSpec((rows, N_HEADS * V_HEAD), row),
        ],
        out_shape=[
            jax.ShapeDtypeStruct((n, N_HEADS * HEAD_W), BF16),
            jax.ShapeDtypeStruct((n, N_HEADS * V_HEAD), BF16),
        ],
        compiler_params=_cparams(("arbitrary",)),
        name="cache_expand",
    )(ckv, kr, wk, wv, place)


def _flash_step(qt_ref, k_ref, vt_ref, m_ref, acc_ref, masked, q0, k0):
    tk, tq = k_ref.shape[0], qt_ref.shape[1]
    if masked:
        kc = (k0 + lax.broadcasted_iota(jnp.int32, (tk, tq), 0)) // CHUNK
        qc = (q0 + lax.broadcasted_iota(jnp.int32, (tk, tq), 1)) // CHUNK
        keep = kc <= qc
    n_half = 2 if tq % (2 * LANES) == 0 else 1
    hw = tq // n_half
    n_chain = N_HEADS * n_half

    def lanes_of(c):
        return slice((c % n_half) * hw, (c % n_half + 1) * hw)

    def scores(c):
        h = c // n_half
        sl = slice(h * HEAD_W, (h + 1) * HEAD_W)
        s = _dot(k_ref[:, sl], qt_ref[sl, lanes_of(c)])
        return jnp.where(keep[:, lanes_of(c)], s, NEG_INF) if masked else s

    def stats(c, s):
        h = c // n_half
        m_prev = m_ref[h, :, lanes_of(c)]
        m_new = jnp.maximum(m_prev, jnp.max(s, axis=0, keepdims=True))
        m_ref[h, :, lanes_of(c)] = m_new
        return m_new, jnp.exp2(m_prev - m_new)

    s = {0: scores(0), 1: scores(1)}
    st = {0: stats(0, s[0])}
    for c in range(n_chain):
        h = c // n_half
        vs = slice(h * V_ROWS, (h + 1) * V_ROWS)
        if c + 2 < n_chain:
            s[c + 2] = scores(c + 2)
        if c + 1 < n_chain:
            st[c + 1] = stats(c + 1, s[c + 1])
        m_new, alpha = st.pop(c)
        p = jnp.exp2(s.pop(c) - m_new).astype(BF16)
        vt_h = jnp.concatenate([vt_ref[j, vs, :] for j in range(vt_ref.shape[0])], axis=1)
        acc_ref[h, :, lanes_of(c)] = alpha * acc_ref[h, :, lanes_of(c)] + _dot(vt_h, p)


def _flash_kernel(qi_tab, ki_tab, qt_ref, k_ref, vt_ref, o_ref, m_ref, acc_ref, *, ratio):
    step = pl.program_id(1)
    qi = qi_tab[step]
    ki = ki_tab[step]
    tk, tq = k_ref.shape[0], qt_ref.shape[1]

    @pl.when(ki == 0)
    def _():
        m_ref[...] = jnp.full(m_ref.shape, NEG_INF, F32)
        acc_ref[...] = jnp.zeros(acc_ref.shape, F32)

    @pl.when(ki < qi * ratio)
    def _():
        _flash_step(qt_ref, k_ref, vt_ref, m_ref, acc_ref, False, 0, 0)

    @pl.when(ki >= qi * ratio)
    def _():
        _flash_step(qt_ref, k_ref, vt_ref, m_ref, acc_ref, True, qi * tq, ki * tk)

    @pl.when(ki == (qi + 1) * ratio - 1)
    def _():
        for h in range(N_HEADS):
            a = acc_ref[h]
            o = a[:V_HEAD, :] / a[V_HEAD:V_HEAD + 1, :]
            o_ref[:, h * V_HEAD:(h + 1) * V_HEAD] = o.T.astype(o_ref.dtype)


def _flash_prompt(qt, k, vt, b, t, tq, tk):
    nq, nk = t // tq, t // tk
    ratio = tq // tk
    vt_tiles = tk // vt.shape[2]
    pairs = [(qi, ki) for qi in range(nq) for ki in range((qi + 1) * ratio)]
    qi_tab = jnp.asarray([p[0] for p in pairs], jnp.int32)
    ki_tab = jnp.asarray([p[1] for p in pairs], jnp.int32)
    grid_spec = pltpu.PrefetchScalarGridSpec(
        num_scalar_prefetch=2,
        grid=(b, len(pairs)),
        in_specs=[
            pl.BlockSpec((N_HEADS * HEAD_W, tq), lambda bi, s, qtab, ktab: (0, bi * nq + qtab[s])),
            pl.BlockSpec((tk, N_HEADS * HEAD_W), lambda bi, s, qtab, ktab: (bi * nk + ktab[s], 0)),
            pl.BlockSpec((vt_tiles, N_HEADS * V_ROWS, vt.shape[2]),
                         lambda bi, s, qtab, ktab: (bi * nk + ktab[s], 0, 0)),
        ],
        out_specs=pl.BlockSpec((tq, N_HEADS * V_HEAD), lambda bi, s, qtab, ktab: (bi * nq + qtab[s], 0)),
        scratch_shapes=[
            pltpu.VMEM((N_HEADS, 1, tq), F32),
            pltpu.VMEM((N_HEADS, V_ROWS, tq), F32),
        ],
    )
    return pl.pallas_call(
        functools.partial(_flash_kernel, ratio=ratio),
        grid_spec=grid_spec,
        out_shape=jax.ShapeDtypeStruct((b * t, N_HEADS * V_HEAD), BF16),
        compiler_params=_cparams(("arbitrary", "arbitrary")),
        name="flash_prompt",
    )(qi_tab, ki_tab, qt, k, vt)


def _attn_sample_kernel(q_ref, kc_ref, vc_ref, kn_ref, vn_ref, o_ref):
    for h in range(N_HEADS):
        sl = slice(h * HEAD_W, (h + 1) * HEAD_W)
        vs = slice(h * V_HEAD, (h + 1) * V_HEAD)
        qh = q_ref[:, sl]
        sc = _dot_nt(qh, kc_ref[:, sl])
        sn = _dot_nt(qh, kn_ref[:, sl])
        m = jnp.maximum(jnp.max(sc, axis=1, keepdims=True), jnp.max(sn, axis=1, keepdims=True))
        pc = jnp.exp2(sc - m)
        pn = jnp.exp2(sn - m)
        den = jnp.sum(pc, axis=1, keepdims=True) + jnp.sum(pn, axis=1, keepdims=True)
        o = _dot(pc.astype(BF16), vc_ref[:, vs]) + _dot(pn.astype(BF16), vn_ref[:, vs])
        o_ref[:, vs] = (o / den).astype(o_ref.dtype)


def _attn_sample(q, kc, vc, kn, vn, b, t, past):
    return pl.pallas_call(
        _attn_sample_kernel,
        grid=(b,),
        in_specs=[
            pl.BlockSpec((t, N_HEADS * HEAD_W), lambda i: (i, 0)),
            pl.BlockSpec((past, N_HEADS * HEAD_W), lambda i: (i, 0)),
            pl.BlockSpec((past, N_HEADS * V_HEAD), lambda i: (i, 0)),
            pl.BlockSpec((t, N_HEADS * HEAD_W), lambda i: (i, 0)),
            pl.BlockSpec((t, N_HEADS * V_HEAD), lambda i: (i, 0)),
        ],
        out_specs=pl.BlockSpec((t, N_HEADS * V_HEAD), lambda i: (i, 0)),
        out_shape=jax.ShapeDtypeStruct((b * t, N_HEADS * V_HEAD), BF16),
        compiler_params=_cparams(("arbitrary",)),
        name="attn_sample",
    )(q, kc, vc, kn, vn)


def _mix_kernel(x_ref, gt_ref, ubch_ref, g_ref, attn_ref, pu_ref, pch_ref, hp_ref, hc_ref,
                wpool_ref, pscale_ref, convw_ref, wbr_ref, wout_ref,
                xo_ref, ptail_ref, ctail_ref, extp_ref, extc_ref, *, n_hist):
    gg, tg, d = x_ref.shape
    j = pl.program_id(1)
    first = j == 0

    ubch = ubch_ref[...].astype(F32)
    u = ubch[:, COL_U:COL_U + POOL_W].reshape(gg, tg, POOL_W)
    bgate = ubch[:, COL_B:COL_B + CONV_W].reshape(gg, tg, CONV_W)
    cu = (ubch[:, COL_C:COL_C + CONV_W] * ubch[:, COL_H:COL_H + CONV_W]).reshape(gg, tg, CONV_W)

    hist_p = jnp.where(first, hp_ref[...], pu_ref[...].astype(F32).reshape(1, HIST_ROWS, POOL_W))
    pch = pch_ref[...].astype(F32)
    prev_cu = (pch[:, :CONV_W] * pch[:, CONV_W:])[HIST_ROWS - CONV_HIST_ROWS:]
    hist_c = jnp.where(first, hc_ref[...], prev_cu.reshape(1, CONV_HIST_ROWS, CONV_W))

    extp_ref[:, :HIST_ROWS, :] = hist_p
    extp_ref[:, HIST_ROWS:, :] = u
    extc_ref[:, :CONV_HIST_ROWS, :] = hist_c
    extc_ref[:, CONV_HIST_ROWS:, :] = cu
    ptail_ref[...] = u[:, tg - HIST_ROWS:, :]
    ctail_ref[...] = cu[:, tg - CONV_HIST_ROWS:, :]

    tpos = j * tg + lax.broadcasted_iota(jnp.int32, (1, tg, 1), 1)
    pooled = []
    for gi, w in enumerate(POOL_WINDOWS):
        cs = slice(gi * POOL_GROUP, (gi + 1) * POOL_GROUP)
        acc = u[:, :, cs]
        for kk in range(1, w):
            acc = acc + extp_ref[:, HIST_ROWS - kk:HIST_ROWS - kk + tg, cs]
        cnt = jnp.minimum(tpos + 1 + n_hist, w).astype(F32)
        dd = acc / cnt - u[:, :, cs]
        pooled.append(_dot(dd.reshape(gg * tg, POOL_GROUP).astype(BF16), wpool_ref[gi]))
    pool = jnp.concatenate(pooled, axis=1) * pscale_ref[...]

    cw = convw_ref[...]
    yc = (cw[0:1, :] * extc_ref[:, CONV_HIST_ROWS - 2:CONV_HIST_ROWS - 2 + tg, :]
          + cw[1:2, :] * extc_ref[:, CONV_HIST_ROWS - 1:CONV_HIST_ROWS - 1 + tg, :]
          + cw[2:3, :] * cu)
    conv = (bgate * yc).reshape(gg * tg, CONV_W)

    gates = jax.nn.sigmoid(g_ref[...].astype(F32))
    mixed = (gates[:, 0:d] * _dot(attn_ref[...], wbr_ref[0])
             + gates[:, d:2 * d] * _dot(pool.astype(BF16), wbr_ref[1])
             + gates[:, 2 * d:3 * d] * _dot(conv.astype(BF16), wbr_ref[2]))
    out = _dot(mixed.astype(BF16), wout_ref[...])
    xo_ref[...] = x_ref[...] + gt_ref[...] * out.reshape(gg, tg, d)


def _mix(x3, gt, z, attn, hist_p, hist_c, wpool, pscale, convw, wbr, wout, rows, n_hist):
    b, t, d = x3.shape
    gg, tg = _row_plan(b, t, rows)
    nj = t // tg
    rb = lambda i, j: i * nj + j
    hpb = tg // HIST_ROWS

    def prev_rows(i, j):
        return jnp.maximum(rb(i, j) * hpb - 1, 0)

    const2 = lambda i, j: (0, 0)
    const3 = lambda i, j: (0, 0, 0)
    return pl.pallas_call(
        functools.partial(_mix_kernel, n_hist=n_hist),
        grid=(b // gg, nj),
        in_specs=[
            pl.BlockSpec((gg, tg, d), lambda i, j: (i, j, 0)),
            pl.BlockSpec((gg, 1, d), lambda i, j: (i, 0, 0)),
            pl.BlockSpec((gg * tg, COL_CQ), lambda i, j: (rb(i, j), 0)),
            pl.BlockSpec((gg * tg, 3 * d), lambda i, j: (rb(i, j), COL_G // (3 * d))),
            pl.BlockSpec((gg * tg, N_HEADS * V_HEAD), lambda i, j: (rb(i, j), 0)),
            pl.BlockSpec((HIST_ROWS, POOL_W), lambda i, j: (prev_rows(i, j), 0)),
            pl.BlockSpec((HIST_ROWS, 2 * CONV_W), lambda i, j: (prev_rows(i, j), COL_C // (2 * CONV_W))),
            pl.BlockSpec((gg, HIST_ROWS, POOL_W), lambda i, j: (i, 0, 0)),
            pl.BlockSpec((gg, CONV_HIST_ROWS, CONV_W), lambda i, j: (i, 0, 0)),
            pl.BlockSpec(wpool.shape, const3),
            pl.BlockSpec(pscale.shape, const2),
            pl.BlockSpec(convw.shape, const2),
            pl.BlockSpec(wbr.shape, const3),
            pl.BlockSpec(wout.shape, const2),
        ],
        out_specs=[
            pl.BlockSpec((gg, tg, d), lambda i, j: (i, j, 0)),
            pl.BlockSpec((gg, HIST_ROWS, POOL_W), lambda i, j: (i, 0, 0)),
            pl.BlockSpec((gg, CONV_HIST_ROWS, CONV_W), lambda i, j: (i, 0, 0)),
        ],
        out_shape=[
            jax.ShapeDtypeStruct((b, t, d), F32),
            jax.ShapeDtypeStruct((b, HIST_ROWS, POOL_W), F32),
            jax.ShapeDtypeStruct((b, CONV_HIST_ROWS, CONV_W), F32),
        ],
        scratch_shapes=[
            pltpu.VMEM((gg, HIST_ROWS + tg, POOL_W), F32),
            pltpu.VMEM((gg, CONV_HIST_ROWS + tg, CONV_W), F32),
        ],
        compiler_params=_cparams(("arbitrary", "arbitrary")),
        name="mix_merge",
    )(x3, gt, z, z, attn, z, z, hist_p, hist_c, wpool, pscale, convw, wbr, wout)


def _peer_query_kernel(x_ref, g_ref, sh_ref, sc_ref, wq_ref, keys_ref, ht_ref, s1_ref, s2_ref):
    gg, tg, d = x_ref.shape
    h = _modnorm(x_ref[...], g_ref[...], sh_ref[...], sc_ref[...]).reshape(gg * tg, d)
    ht_ref[0] = h.T.astype(BF16)
    hb = h.astype(BF16)
    half = D_KEY // 2
    k1 = keys_ref[0]
    k2 = keys_ref[1]
    def query(hh):
        return _dot(hb, wq_ref[:, hh * D_KEY:(hh + 1) * D_KEY]).astype(BF16)

    q_next = query(0)
    for hh in range(PEER_HEADS):
        q = q_next
        if hh + 1 < PEER_HEADS:
            q_next = query(hh + 1)
        s1_ref[hh, 0] = _dot_nt(k1, q[:, :half])
        s2_ref[hh, 0] = _dot_nt(k2, q[:, half:])


def _peer_query(x3, g, sh, sc, wq, keys, rows):
    b, t, d = x3.shape
    gg, tg = _row_plan(b, t, rows)
    nj = t // tg
    nt = b * t // rows
    tok = lambda i, j: (0, i * nj + j, 0, 0)
    return pl.pallas_call(
        _peer_query_kernel,
        grid=(b // gg, nj),
        in_specs=[
            pl.BlockSpec((gg, tg, d), lambda i, j: (i, j, 0)),
            pl.BlockSpec((1, d), lambda i, j: (0, 0)),
            pl.BlockSpec((gg, 1, d), lambda i, j: (i, 0, 0)),
            pl.BlockSpec((gg, 1, d), lambda i, j: (i, 0, 0)),
            pl.BlockSpec(wq.shape, lambda i, j: (0, 0)),
            pl.BlockSpec(keys.shape, lambda i, j: (0, 0, 0)),
        ],
        out_specs=[
            pl.BlockSpec((1, d, rows), lambda i, j: (i * nj + j, 0, 0)),
            pl.BlockSpec((PEER_HEADS, 1, N_KEYS, rows), tok),
            pl.BlockSpec((PEER_HEADS, 1, N_KEYS, rows), tok),
        ],
        out_shape=[
            jax.ShapeDtypeStruct((nt, d, rows), BF16),
            jax.ShapeDtypeStruct((PEER_HEADS, nt, N_KEYS, rows), F32),
            jax.ShapeDtypeStruct((PEER_HEADS, nt, N_KEYS, rows), F32),
        ],
        compiler_params=_cparams(("arbitrary", "arbitrary")),
        name="peer_query",
    )(x3, g, sh, sc, wq, keys)


def _top_extract(s, with_rank):
    nk, r = s.shape
    iota = lax.broadcasted_iota(jnp.int32, (nk, r), 0).astype(F32)
    iota_k = lax.broadcasted_iota(jnp.int32, (PEER_TOPK, r), 0)
    rank = jnp.full((nk, r), NOT_TOP, F32) if with_rank else None
    vals = jnp.zeros((PEER_TOPK, r), F32)
    idxs = jnp.zeros((PEER_TOPK, r), F32)
    x = s
    for it in range(PEER_TOPK):
        m = jnp.max(x, axis=0, keepdims=True)
        idx = jnp.min(jnp.where(x == m, iota, float(nk)), axis=0, keepdims=True)
        hit = iota == idx
        if with_rank:
            rank = jnp.where(hit, float(it), rank)
        x = jnp.where(hit, -jnp.inf, x)
        vals = jnp.where(iota_k == it, m, vals)
        idxs = jnp.where(iota_k == it, idx, idxs)
    return rank, vals, idxs


def _oddeven_mergesort_pairs(n):
    pairs = []

    def merge(lo, hi, step):
        nxt = step * 2
        if nxt < hi - lo:
            merge(lo, hi, nxt)
            merge(lo + step, hi, nxt)
            for i in range(lo + step, hi - step, nxt):
                pairs.append((i, i + step))
        else:
            pairs.append((lo, lo + step))

    def sort(lo, hi):
        if hi - lo >= 1:
            mid = lo + (hi - lo) // 2
            sort(lo, mid)
            sort(mid + 1, hi)
            merge(lo, hi, 1)

    sort(0, n - 1)
    return pairs


SUBLANES = 8
SORT_PAIRS = _oddeven_mergesort_pairs(N_KEYS // SUBLANES)


def _sorted_top(s):
    n = N_KEYS // SUBLANES
    cols = [s[SUBLANES * v:SUBLANES * (v + 1), :] for v in range(n)]

    def exchange(i, j):
        hi, lo = jnp.maximum(cols[i], cols[j]), jnp.minimum(cols[i], cols[j])
        cols[i], cols[j] = hi, lo

    for i, j in SORT_PAIRS:
        exchange(i, j)
    shift = SUBLANES // 2
    while shift >= 1:
        other = [pltpu.roll(c, shift, axis=0) for c in cols]
        cols = [jnp.maximum(cols[p], other[n - 1 - p]) for p in range(n)]
        dist = n // 2
        while dist >= 1:
            for p in range(n):
                if p & dist == 0:
                    exchange(p, p + dist)
            dist //= 2
        shift //= 2
    return cols


def _ranks_from_sorted(s, top):
    n = len(top)
    assert n == 16
    ranks = []
    n_sel = None
    for v in range(N_KEYS // SUBLANES):
        x = s[SUBLANES * v:SUBLANES * (v + 1), :]
        g0, g1, g2 = top[3] > x, top[7] > x, top[11] > x
        acc = jnp.where(g2, 12.0, jnp.where(g1, 8.0, jnp.where(g0, 4.0, 0.0)))
        for r_ in range(3):
            th = jnp.where(g2, top[12 + r_], jnp.where(g1, top[8 + r_], jnp.where(g0, top[4 + r_], top[r_])))
            acc = acc + jnp.where(th > x, 1.0, 0.0)
        sel = x >= top[n - 1]
        ranks.append(jnp.where(sel, acc, NOT_TOP))
        one = jnp.where(sel, 1.0, 0.0)
        n_sel = one if n_sel is None else n_sel + one
    n_sel = jnp.sum(n_sel, axis=0, keepdims=True)
    dup = jnp.where(top[0] == top[1], 1.0, 0.0)
    for a in range(1, n - 1):
        dup = jnp.maximum(dup, jnp.where(top[a] == top[a + 1], 1.0, 0.0))
    flag = jnp.maximum(jnp.max(dup, axis=0, keepdims=True), jnp.where(n_sel != float(n), 1.0, 0.0))
    return jnp.concatenate(ranks, axis=0), flag


def _stack_rows(top):
    r = top[0].shape[1]
    iota_k = lax.broadcasted_iota(jnp.int32, (PEER_TOPK, r), 0)
    out = jnp.zeros((PEER_TOPK, r), F32)
    for a, t in enumerate(top):
        out = jnp.where(iota_k == a, jnp.concatenate([t, t], axis=0), out)
    return out


def _peer_select_kernel(s1_ref, s2_ref, c_ref, lr_ref, rk2_ref, d_ref):
    refs = (s1_ref, s2_ref, c_ref, lr_ref, rk2_ref, d_ref)
    flag = None
    for h in range(s1_ref.shape[0]):
        f = _peer_select_sorted(h, *refs)
        flag = f if flag is None else jnp.maximum(flag, f)

    @pl.when(jnp.max(flag) > 0.0)
    def _():
        for h in range(s1_ref.shape[0]):
            _peer_select_head(h, *refs)


def _peer_select_sorted(h, s1_ref, s2_ref, c_ref, lr_ref, rk2_ref, d_ref):
    s1 = s1_ref[h, 0]
    s2 = s2_ref[h, 0]
    top1 = _sorted_top(s1)
    top2 = _sorted_top(s2)
    rank1, flag1 = _ranks_from_sorted(s1, top1)
    rank2, flag2 = _ranks_from_sorted(s2, top2)
    v1 = _stack_rows(top1)
    v2 = _stack_rows(top2)
    cnt, zsum = _merge_counts(v1, v2)
    lr = jnp.zeros(rank1.shape, F32)
    for a in range(PEER_TOPK):
        lr = jnp.where(rank1 == float(a), cnt[a:a + 1, :], lr)
    c_ref[h, 0] = jnp.exp(s1 - v1[0:1, :]) / zsum
    lr_ref[h, 0] = lr
    rk2_ref[h, 0] = rank2.astype(BF16)
    d_ref[h, 0] = jnp.exp(s2 - v2[0:1, :]).astype(BF16)
    return jnp.maximum(flag1, flag2)


def _merge_counts(v1, v2):
    r = v1.shape[1]
    iota_k = lax.broadcasted_iota(jnp.int32, (PEER_TOPK, r), 0).astype(F32)
    cnt = jnp.zeros((PEER_TOPK, r), F32)
    front = v1 + v2[0:1, :]
    top = front[0:1, :]
    zsum = jnp.zeros((1, r), F32)
    for _ in range(PEER_TOPK):
        m = jnp.max(front, axis=0, keepdims=True)
        a = jnp.min(jnp.where(front == m, iota_k, float(PEER_TOPK)), axis=0, keepdims=True)
        hit = iota_k == a
        zsum = zsum + jnp.exp(m - top)
        cnt = jnp.where(hit, cnt + 1.0, cnt)
        c_hit = jnp.max(jnp.where(hit, cnt, 0.0), axis=0, keepdims=True)
        v1_hit = jnp.max(jnp.where(hit, v1, -jnp.inf), axis=0, keepdims=True)
        nxt = jnp.full((1, r), -jnp.inf, F32)
        for bcol in range(1, PEER_TOPK):
            nxt = jnp.where(c_hit == float(bcol), v2[bcol:bcol + 1, :], nxt)
        front = jnp.where(hit, v1_hit + nxt, front)
    return cnt, zsum


def _peer_select_head(h, s1_ref, s2_ref, c_ref, lr_ref, rk2_ref, d_ref):
    s1 = s1_ref[h, 0]
    s2 = s2_ref[h, 0]
    nk, r = s1.shape
    _, v1, idx1 = _top_extract(s1, False)
    rank2, v2, _ = _top_extract(s2, True)
    cnt, zsum = _merge_counts(v1, v2)
    iota = lax.broadcasted_iota(jnp.int32, (nk, r), 0).astype(F32)
    lr = jnp.zeros((nk, r), F32)
    for a in range(PEER_TOPK):
        lr = jnp.where(iota == idx1[a:a + 1, :], cnt[a:a + 1, :], lr)
    c_ref[h, 0] = jnp.exp(s1 - v1[0:1, :]) / zsum
    lr_ref[h, 0] = lr
    rk2_ref[h, 0] = rank2.astype(BF16)
    d_ref[h, 0] = jnp.exp(s2 - v2[0:1, :]).astype(BF16)


def _peer_select(s1, s2):
    hh, nt, nk, lanes = s1.shape
    heads_per_step = 2
    spec = pl.BlockSpec((heads_per_step, 1, nk, lanes), lambda i, h: (h, i, 0, 0))
    f32 = jax.ShapeDtypeStruct(s1.shape, F32)
    b16 = jax.ShapeDtypeStruct(s1.shape, BF16)
    return pl.pallas_call(
        _peer_select_kernel,
        grid=(nt, hh // heads_per_step),
        in_specs=[spec, spec],
        out_specs=[spec, spec, spec, spec],
        out_shape=[f32, f32, b16, b16],
        compiler_params=_cparams(("arbitrary", "arbitrary")),
        name="peer_select",
    )(s1, s2)


def _gelu(x):
    return 0.5 * x * (1.0 + lax.erf(x * (2.0 ** -0.5)))


def _peer_dense_kernel(x_ref, gt_ref, ht_ref, u_ref, vt_ref, c_ref, lr_ref, rk2_ref, d_ref,
                       xo_ref, a_ref, w_ref, acc_ref, *, rows_per_chunk):
    gg, tg, d = x_ref.shape
    nt, _, lanes = ht_ref.shape
    e = pl.program_id(1)
    slab = BF16_SUBLANES
    n_slab = N_KEYS // slab

    @pl.when(e == 0)
    def _():
        acc_ref[...] = jnp.zeros(acc_ref.shape, F32)

    w_ref[(nt - 1) % 2] = jnp.zeros(w_ref.shape[1:], BF16)

    def tile(lt):
        prev = (lt + nt - 1) % nt
        a_ref[...] = _dot(u_ref[...], ht_ref[lt])
        acc_ref[prev] += _dot(vt_ref[...], w_ref[prev % 2])
        w_cur = w_ref.at[lt % 2]
        for ii in range(rows_per_chunk):
            cl = []
            for hh in range(PEER_HEADS):
                cl.append((
                    jnp.broadcast_to(c_ref[hh, lt, ii:ii + 1, :], (slab, lanes)).astype(BF16),
                    jnp.broadcast_to(lr_ref[hh, lt, ii:ii + 1, :], (slab, lanes)).astype(BF16)))
            for jv in range(n_slab):
                rs = slice(jv * slab, (jv + 1) * slab)
                gate = None
                for hh in range(PEER_HEADS):
                    dd = d_ref[hh, lt, rs, :]
                    term = cl[hh][0] * jnp.where(rk2_ref[hh, lt, rs, :] < cl[hh][1], dd,
                                                 jnp.zeros_like(dd))
                    gate = term if gate is None else gate + term
                rows = slice(ii * N_KEYS + jv * slab, ii * N_KEYS + (jv + 1) * slab)
                w_cur[rows, :] = _gelu(a_ref[rows, :]).astype(BF16) * gate

    for lt in range(nt):
        tile(lt)
    acc_ref[nt - 1] += _dot(vt_ref[...], w_ref[(nt - 1) % 2])

    @pl.when(e == pl.num_programs(1) - 1)
    def _():
        for lt in range(nt):
            upd = acc_ref[lt].T
            if gg == 1:
                rows = slice(lt * lanes, (lt + 1) * lanes)
                xo_ref[0, rows, :] = x_ref[0, rows, :] + gt_ref[0] * upd
            else:
                per = lanes // tg
                rows = slice(lt * per, (lt + 1) * per)
                xo_ref[rows] = x_ref[rows] + gt_ref[rows] * upd.reshape(per, tg, d)


def _peer_dense(x3, gt, ht, u, vt, c, lr, rk2, dd, tiles_per_block, rows_per_chunk):
    b, t, d = x3.shape
    nt_all, _, lanes = ht.shape
    gg, tg = _row_plan(b, t, tiles_per_block * lanes)
    nj = t // tg
    ne = u.shape[0]
    ec = rows_per_chunk * N_KEYS
    assert gg == 1 or lanes % tg == 0
    full = pl.BlockSpec((PEER_HEADS, tiles_per_block, N_KEYS, lanes), lambda i, e: (0, i, 0, 0))
    part = pl.BlockSpec((PEER_HEADS, tiles_per_block, rows_per_chunk, lanes),
                        lambda i, e: (0, i, e, 0))
    return pl.pallas_call(
        functools.partial(_peer_dense_kernel, rows_per_chunk=rows_per_chunk),
        grid=(nt_all // tiles_per_block, ne // ec),
        in_specs=[
            pl.BlockSpec((gg, tg, d), lambda i, e: (i // nj, i % nj, 0)),
            pl.BlockSpec((gg, 1, d), lambda i, e: (i // nj, 0, 0)),
            pl.BlockSpec((tiles_per_block, d, lanes), lambda i, e: (i, 0, 0)),
            pl.BlockSpec((ec, d), lambda i, e: (e, 0)),
            pl.BlockSpec((d, ec), lambda i, e: (0, e)),
            part, part, full, full,
        ],
        out_specs=pl.BlockSpec((gg, tg, d), lambda i, e: (i // nj, i % nj, 0)),
        out_shape=jax.ShapeDtypeStruct((b, t, d), F32),
        scratch_shapes=[
            pltpu.VMEM((ec, lanes), F32),
            pltpu.VMEM((2, ec, lanes), BF16),
            pltpu.VMEM((tiles_per_block, d, lanes), F32),
        ],
        compiler_params=_cparams(("arbitrary", "arbitrary")),
        name="peer_dense",
    )(x3, gt, ht, u, vt, c, lr, rk2, dd)


def _final_kernel(x_ref, g_ref, o_ref):
    o_ref[...] = _rmsnorm2(x_ref[...], g_ref[...])


def _final_norm(x3, g, rows):
    b, t, d = x3.shape
    x2 = x3.reshape(b * t, d)
    y = pl.pallas_call(
        _final_kernel,
        grid=(b * t // rows,),
        in_specs=[pl.BlockSpec((rows, d), lambda i: (i, 0)), pl.BlockSpec((1, d), lambda i: (0, 0))],
        out_specs=pl.BlockSpec((rows, d), lambda i: (i, 0)),
        out_shape=jax.ShapeDtypeStruct((b * t, d), F32),
        compiler_params=_cparams(("arbitrary",)),
        name="final_norm",
    )(x2, g)
    return y.reshape(b, t, d)


def _swap_halves(w):
    half = w.shape[-1] // 2
    return jnp.concatenate([w[..., half:], w[..., :half]], axis=-1)


def _prep_layer(p, l):
    d = p['w_in'].shape[1]
    w_in = p['w_in'][l]
    offs = [0]
    for nsz in (Q_LORA, KV_LORA, QK_ROPE, POOL_W, CONV_W, CONV_W, CONV_W, 3 * d):
        offs.append(offs[-1] + nsz)
    w_cq, w_ckv, w_kr, w_u, w_b, w_c, w_h, w_g = [w_in[:, offs[i]:offs[i + 1]] for i in range(8)]
    zeros = lambda n: jnp.zeros((d, n), w_in.dtype)
    w_krg = jnp.concatenate([zeros(QK_NOPE), w_kr, _swap_halves(w_kr)], axis=1)
    w_in_r = jnp.concatenate(
        [w_u, w_b, w_c, w_h, w_cq, w_ckv, w_krg, zeros(COL_G - COL_KR - HEAD_W), w_g], axis=1)
    assert w_in_r.shape[1] == D_IN_PAD

    w_uq = p['w_uq'][l]
    wq_r = jnp.concatenate(
        [w_uq[..., :QK_NOPE], w_uq[..., QK_NOPE:], _swap_halves(w_uq[..., QK_NOPE:])], axis=-1)
    wq_r = wq_r.reshape(Q_LORA, N_HEADS * HEAD_W)
    w_ukv = p['w_ukv'][l]
    wk_r = jnp.concatenate(
        [w_ukv[..., :QK_NOPE], jnp.zeros((KV_LORA, N_HEADS, HEAD_W - QK_NOPE), w_ukv.dtype)],
        axis=-1).reshape(KV_LORA, N_HEADS * HEAD_W)
    wv_r = w_ukv[..., QK_NOPE:].reshape(KV_LORA, N_HEADS * V_HEAD)
    return dict(
        w_in=w_in_r.astype(BF16), wq=wq_r.astype(BF16), wk=wk_r.astype(BF16), wv=wv_r.astype(BF16),
        g_mix=p['g_mix'][l][None, :], g_q=p['g_q'][l][None, :], g_kv=p['g_kv'][l][None, :],
        w_pool=p['w_pool'][l].astype(BF16), pool_scale=p['pool_scale'][l][None, :],
        conv_w=jnp.pad(p['conv_w'][l], ((0, 8 - CONV_K), (0, 0))),
        w_branch=p['w_branch'][l].astype(BF16), w_out=p['w_out'][l].astype(BF16),
        g_ffn=p['g_ffn'][l][None, :],
        peer_wq=p['peer_wq'][l].reshape(d, PEER_HEADS * D_KEY).astype(BF16),
        peer_keys=p['peer_keys'][l].astype(BF16),
        peer_u=p['peer_u'][l].astype(BF16),
        peer_vt=p['peer_v'][l].T.astype(BF16),
    )


def _rope_tables(pos):
    half = QK_ROPE // 2
    inv = ROPE_THETA ** (-jnp.arange(half, dtype=F32) / half)
    ang = pos.astype(F32)[:, None] * inv[None, :]
    cos, sin = jnp.cos(ang), jnp.sin(ang)
    z = lambda n: jnp.zeros((pos.shape[0], n), F32)
    cc = jnp.concatenate([z(QK_NOPE), cos, cos, z(QK_ROPE)], axis=1)
    ss = jnp.concatenate([z(QK_NOPE), -sin, sin, z(QK_ROPE)], axis=1)
    return cc, ss


def _trunk(x, mods, pos, n_hist, cache, hist_pool, hist_conv, layers, g_final, cfg):
    b, t, d = x.shape
    rows = cfg['rows']
    n = b * t
    cc, ss = _rope_tables(pos)
    if t < rows:
        cc = jnp.tile(cc, (rows // t, 1))
        ss = jnp.tile(ss, (rows // t, 1))
    place = jnp.concatenate(
        [jnp.zeros((QK_ROPE, QK_NOPE), F32), jnp.eye(QK_ROPE, dtype=F32),
         jnp.zeros((QK_ROPE, HEAD_W - QK_NOPE - QK_ROPE), F32)], axis=1).astype(BF16)
    new_kv, new_kr, new_pool, new_conv = [], [], [], []
    for l, lw in enumerate(layers):
        mod = mods[l].reshape(b, 1, 6 * d)
        sh1, sc1, gt1, sh2, sc2, gt2 = [mod[:, :, i * d:(i + 1) * d] for i in range(6)]
        z = _inproj(x, lw['g_mix'], sh1, sc1, lw['w_in'], rows)
        q, k, v, ckv, krg = _attn_prep(z, cc, ss, lw['g_q'], lw['wq'], lw['g_kv'], lw['wk'],
                                       lw['wv'], rows, cache is None)
        new_kv.append(ckv.reshape(b, t, KV_LORA))
        new_kr.append(krg[:, QK_NOPE:QK_NOPE + QK_ROPE].reshape(b, t, QK_ROPE))
        if cache is None:
            attn = _flash_prompt(q, k, v, b, t, cfg['attn_tq'], cfg['attn_tk'])
        else:
            ckv_c, kr_c = cache
            past = ckv_c.shape[2]
            kc, vc = _cache_expand(ckv_c[l].reshape(b * past, KV_LORA),
                                   kr_c[l].reshape(b * past, QK_ROPE),
                                   lw['wk'], lw['wv'], place, cfg['cache_rows'])
            attn = _attn_sample(q, kc, vc, k, v, b, t, past)
        hp = jnp.pad(hist_pool[l], ((0, 0), (HIST_ROWS - POOL_HIST, 0), (0, 0)))
        hc = jnp.pad(hist_conv[l], ((0, 0), (CONV_HIST_ROWS - (CONV_K - 1), 0), (0, 0)))
        x, ptail, ctail = _mix(x, gt1, z, attn, hp, hc, lw['w_pool'], lw['pool_scale'],
                               lw['conv_w'], lw['w_branch'], lw['w_out'], rows, n_hist)
        new_pool.append(ptail[:, HIST_ROWS - POOL_HIST:, :])
        new_conv.append(ctail[:, CONV_HIST_ROWS - (CONV_K - 1):, :])
        ht, s1, s2 = _peer_query(x, lw['g_ffn'], sh2, sc2, lw['peer_wq'], lw['peer_keys'], rows)
        c, lr, rk2, dd = _peer_select(s1, s2)
        x = _peer_dense(x, gt2, ht, lw['peer_u'], lw['peer_vt'], c, lr, rk2, dd,
                        min(cfg['peer_tiles'], n // rows), cfg['rows_per_chunk'])
    y = _final_norm(x, g_final[None, :], rows)
    return y, jnp.stack(new_kv), jnp.stack(new_kr), jnp.stack(new_pool), jnp.stack(new_conv)


def _config(t_prompt):
    rows = min(256, t_prompt)
    return dict(rows=rows, attn_tq=min(512, t_prompt), attn_tk=min(512, t_prompt), cache_rows=1024, peer_tiles=4,
                rows_per_chunk=8)


def kernel(x_prompt, x_sample, cache_kv_latent, cache_k_rope, state_pool, state_conv,
           c_prompt, c_sample, w_ada, b_ada, g_mix, w_in, g_q, w_uq, g_kv, w_ukv,
           w_pool, pool_scale, conv_w, w_branch, w_out, g_ffn, peer_wq, peer_keys,
           peer_u, peer_v, g_final):
    p = {'w_in': w_in, 'g_mix': g_mix, 'g_q': g_q, 'w_uq': w_uq, 'g_kv': g_kv, 'w_ukv': w_ukv,
         'w_pool': w_pool, 'pool_scale': pool_scale, 'conv_w': conv_w, 'w_branch': w_branch,
         'w_out': w_out, 'g_ffn': g_ffn, 'peer_wq': peer_wq, 'peer_keys': peer_keys,
         'peer_u': peer_u, 'peer_v': peer_v}
    depth = w_ada.shape[0]
    bp, tp, d = x_prompt.shape
    bs, ts, _ = x_sample.shape
    past = cache_kv_latent.shape[2]
    layers = [_prep_layer(p, l) for l in range(depth)]

    c_all = jnp.concatenate([c_prompt, c_sample], axis=0)
    pad = (-c_all.shape[0]) % 8
    c_all = jnp.pad(c_all, ((0, pad), (0, 0)))
    mods = _ada(c_all, w_ada, b_ada)
    mods_p, mods_s = mods[:, :bp], mods[:, bp:bp + bs]

    cfg = _config(tp)
    zp = jnp.zeros((depth, bp, POOL_HIST, POOL_W), x_prompt.dtype)
    zc = jnp.zeros((depth, bp, CONV_K - 1, CONV_W), x_prompt.dtype)
    y_p, p_kv, p_kr, p_pool, p_conv = _trunk(
        x_prompt, mods_p, jnp.arange(tp), 0, None, zp, zc, layers, g_final, cfg)
    y_s, s_kv, s_kr, s_pool, s_conv = _trunk(
        x_sample, mods_s, past + jnp.arange(ts), min(past, POOL_HIST),
        (cache_kv_latent, cache_k_rope), state_pool, state_conv, layers, g_final, cfg)
    return (y_p, y_s, p_kv, p_kr, p_pool, p_conv, s_kv, s_kr, s_pool, s_conv)
```
